```python
import math
import jax
import jax.numpy as jnp
from jax import lax
import numpy as np

D_MODEL = 2048
BATCH = 4
SEQ = 8192
DEPTH = 2

CTX_LEN = 256
GRID_W = 64
N_BRANCH = 3
BRANCH_WIDTH = 1024
N_HEADS = 8
N_KV_HEADS = 2
HEAD_DIM = 128
GROUP = N_HEADS // N_KV_HEADS
WINDOW = 128
Q_BLOCK = 128
ROPE_BASE = 10000.0
RNN_WIDTH = BRANCH_WIDTH
RNN_BLOCKS = 8
RNN_BLOCK = RNN_WIDTH // RNN_BLOCKS
RNN_CONV = 4
RNN_CONV_LEFT = 2
LRU_C = 8.0
SC_WIDTH = BRANCH_WIDTH
SC_CONV = 3
SC_CONV_LEFT = 1
D_FF = 5504
N_MOD = 9
EPS = 1e-6
NEG_INF = -1e30
IN_SIZES = (RNN_WIDTH, RNN_WIDTH, SC_WIDTH, SC_WIDTH, SC_WIDTH,
            N_HEADS * HEAD_DIM, N_KV_HEADS * HEAD_DIM, N_KV_HEADS * HEAD_DIM,
            N_BRANCH * D_MODEL)
IN_COLS = sum(IN_SIZES)

kernel_name = "hybrid_rglru_shortconv_swa_diffusion_block"


def rmsnorm(x, g):
    xf = x.astype(jnp.float32)
    y = xf * lax.rsqrt(jnp.mean(xf * xf, axis=-1, keepdims=True) + EPS)
    return (y * g.astype(jnp.float32)).astype(x.dtype)


def modulate(x, shift, scale):
    return x * (1 + scale) + shift


def swiglu(u, w13, w2):
    gu = u @ w13
    gate, up = gu[..., :D_FF], gu[..., D_FF:]
    return (jax.nn.silu(gate) * up) @ w2


def split_in(z):
    out, start = [], 0
    for n in IN_SIZES:
        out.append(z[..., start:start + n])
        start += n
    return out


def dwconv(x, w, left):
    k_w, ch = w.shape
    return lax.conv_general_dilated(
        x, w[:, None, :], window_strides=(1,), padding=[(left, k_w - 1 - left)],
        dimension_numbers=("NWC", "WIO", "NWC"), feature_group_count=ch)


def axial_rope(n_tok):
    rows = n_tok // GRID_W
    row = jnp.repeat(jnp.arange(rows), GRID_W).astype(jnp.float32)
    col = jnp.tile(jnp.arange(GRID_W), rows).astype(jnp.float32)
    half = HEAD_DIM // 2
    inv = ROPE_BASE ** (-jnp.arange(0, half, 2, dtype=jnp.float32) / half)
    ang = jnp.concatenate([row[:, None] * inv, col[:, None] * inv], axis=-1)
    ang = ang.reshape(n_tok, 2, half // 2)
    return jnp.cos(ang), jnp.sin(ang)


def apply_rope(x, cos, sin):
    b, l, h, d = x.shape
    xr = x.astype(jnp.float32).reshape(b, l, h, 2, 2, d // 4)
    x1, x2 = xr[..., 0, :], xr[..., 1, :]
    cs, sn = cos[None, :, None], sin[None, :, None]
    out = jnp.stack([x1 * cs - x2 * sn, x2 * cs + x1 * sn], axis=-2)
    return out.reshape(b, l, h, d).astype(x.dtype)


def linear_scan(a, b, h0):
    def combine(e1, e2):
        a1, b1 = e1
        a2, b2 = e2
        return a1 * a2, a2 * b1 + b2
    a_cum, b_cum = lax.associative_scan(combine, (a, b), axis=1)
    return b_cum + a_cum * h0[:, None, :]


def rglru(x, w_a, b_a, w_x, b_x, lam, h0, reverse):
    b, l, r = x.shape
    xb = x.reshape(b, l, RNN_BLOCKS, RNN_BLOCK)
    rg = jax.nn.sigmoid((jnp.einsum("blnd,nde->blne", xb, w_a).reshape(b, l, r) + b_a).astype(jnp.float32))
    ig = jax.nn.sigmoid((jnp.einsum("blnd,nde->blne", xb, w_x).reshape(b, l, r) + b_x).astype(jnp.float32))
    log_a = -LRU_C * rg * jax.nn.softplus(-lam.astype(jnp.float32))
    a = jnp.exp(log_a)
    u = jnp.sqrt(-jnp.expm1(2.0 * log_a)) * (ig * x.astype(jnp.float32))
    if reverse:
        a, u = jnp.flip(a, axis=1), jnp.flip(u, axis=1)
    h = linear_scan(a, u, h0)
    h_last = h[:, -1]
    if reverse:
        h = jnp.flip(h, axis=1)
    return h.astype(x.dtype), h_last


def sink_softmax(logits, sink):
    s = jnp.broadcast_to(sink.astype(jnp.float32)[None, :, :, None, None], logits.shape[:-1] + (1,))
    p = jax.nn.softmax(jnp.concatenate([s, logits], axis=-1), axis=-1)
    return p[..., 1:]


def banded_attention(q, k, v, kc, vc, sink):
    b, l = q.shape[0], q.shape[1]
    nblk = l // Q_BLOCK
    scale = HEAD_DIM ** -0.5
    qb = q.reshape(b, nblk, Q_BLOCK, N_KV_HEADS, GROUP, HEAD_DIM)
    pad = ((0, 0), (Q_BLOCK, Q_BLOCK), (0, 0), (0, 0))
    kp, vp = jnp.pad(k, pad), jnp.pad(v, pad)
    span = 3 * Q_BLOCK
    offs_q = jnp.arange(Q_BLOCK)
    offs_k = jnp.arange(span) - Q_BLOCK

    def block(n):
        qn = lax.dynamic_index_in_dim(qb, n, axis=1, keepdims=False)
        kn = lax.dynamic_slice_in_dim(kp, n * Q_BLOCK, span, axis=1)
        vn = lax.dynamic_slice_in_dim(vp, n * Q_BLOCK, span, axis=1)
        qpos = n * Q_BLOCK + offs_q
        kpos = n * Q_BLOCK + offs_k
        valid = (jnp.abs(qpos[:, None] - kpos[None, :]) <= WINDOW) & (kpos >= 0)[None, :] & (kpos < l)[None, :]
        s_loc = jnp.einsum("bqkgd,bskd->bkgqs", qn, kn).astype(jnp.float32) * scale
        s_loc = jnp.where(valid, s_loc, NEG_INF)
        s_ctx = jnp.einsum("bqkgd,bckd->bkgqc", qn, kc).astype(jnp.float32) * scale
        p = sink_softmax(jnp.concatenate([s_loc, s_ctx], axis=-1), sink).astype(v.dtype)
        return (jnp.einsum("bkgqs,bskd->bqkgd", p[..., :span], vn)
                + jnp.einsum("bkgqc,bckd->bqkgd", p[..., span:], vc))

    out = lax.map(block, jnp.arange(nblk))
    return jnp.moveaxis(out, 0, 1).reshape(b, l, N_HEADS * HEAD_DIM)


def context_attention(qc, kc, vc, sink):
    b, n = qc.shape[0], qc.shape[1]
    s = jnp.einsum("bqkgd,bckd->bkgqc", qc, kc).astype(jnp.float32) * (HEAD_DIM ** -0.5)
    p = sink_softmax(s, sink).astype(vc.dtype)
    return jnp.einsum("bkgqc,bckd->bqkgd", p, vc).reshape(b, n, N_HEADS * HEAD_DIM)


def merge_branches(ys, g, b_merge, w_branch, w_out):
    gates = jax.nn.sigmoid(g.reshape(g.shape[:-1] + (N_BRANCH, D_MODEL)) + b_merge)
    merged = gates[..., 0, :] * (ys[0] @ w_branch[0])
    for i in range(1, N_BRANCH):
        merged = merged + gates[..., i, :] * (ys[i] @ w_branch[i])
    return merged @ w_out


def token_mixer(u, uc, cos, sin, w_in, b_merge, rnn_conv_w, rnn_conv_b, lru_w_a, lru_b_a,
                lru_w_x, lru_b_x, lru_lambda, sc_conv_w, attn_sink, w_branch, w_out, with_ctx_out):
    b, l, _ = u.shape
    n_ctx = uc.shape[1]
    rx, rg, sb, scg, sx, q, k, v, g = split_in(u @ w_in)
    rxc, rgc, sbc, scgc, sxc, qc, kc, vc, gc = split_in(uc @ w_in)

    xa = dwconv(rx, rnn_conv_w, RNN_CONV_LEFT) + rnn_conv_b
    xac = dwconv(rxc, rnn_conv_w, RNN_CONV_LEFT) + rnn_conv_b
    h0 = jnp.zeros((b, RNN_WIDTH), jnp.float32)
    hc_f, last_f = rglru(xac, lru_w_a[0], lru_b_a[0], lru_w_x[0], lru_b_x[0], lru_lambda[0], h0, False)
    hc_b, last_b = rglru(xac, lru_w_a[1], lru_b_a[1], lru_w_x[1], lru_b_x[1], lru_lambda[1], h0, True)
    hl_f, _ = rglru(xa, lru_w_a[0], lru_b_a[0], lru_w_x[0], lru_b_x[0], lru_lambda[0], last_f, False)
    hl_b, _ = rglru(xa, lru_w_a[1], lru_b_a[1], lru_w_x[1], lru_b_x[1], lru_lambda[1], last_b, True)
    ya = (hl_f + hl_b) * jax.nn.gelu(rg)

    yb = sb * dwconv(scg * sx, sc_conv_w, SC_CONV_LEFT)

    sink = attn_sink.reshape(N_KV_HEADS, GROUP)
    q = apply_rope(q.reshape(b, l, N_HEADS, HEAD_DIM), cos, sin).reshape(b, l, N_KV_HEADS, GROUP, HEAD_DIM)
    k = apply_rope(k.reshape(b, l, N_KV_HEADS, HEAD_DIM), cos, sin)
    v = v.reshape(b, l, N_KV_HEADS, HEAD_DIM)
    kc = kc.reshape(b, n_ctx, N_KV_HEADS, HEAD_DIM)
    vc = vc.reshape(b, n_ctx, N_KV_HEADS, HEAD_DIM)
    yatt = banded_attention(q, k, v, kc, vc, sink)

    y = merge_branches((ya, yb, yatt), g, b_merge, w_branch, w_out)
    if not with_ctx_out:
        return y, None

    yac = (hc_f + hc_b) * jax.nn.gelu(rgc)
    ybc = sbc * dwconv(scgc * sxc, sc_conv_w, SC_CONV_LEFT)
    yattc = context_attention(qc.reshape(b, n_ctx, N_KV_HEADS, GROUP, HEAD_DIM), kc, vc, sink)
    yc = merge_branches((yac, ybc, yattc), gc, b_merge, w_branch, w_out)
    return y, yc


def setup_inputs(seed: int = 0) -> dict:
    key = jax.random.key(seed)
    ks = jax.random.split(key, 32)
    f32 = jnp.float32

    def nrm(k, shape, scale):
        return jax.random.normal(k, shape, f32) * scale

    a_c = jax.random.uniform(ks[15], (DEPTH, 2, RNN_WIDTH), f32, minval=0.9, maxval=0.999)
    s_l = a_c ** (1.0 / LRU_C)
    lam = jnp.log(s_l) - jnp.log1p(-s_l)
    return {
        "x": nrm(ks[0], (BATCH, SEQ, D_MODEL), 1.0),
        "c": nrm(ks[1], (BATCH, D_MODEL), 1.0),
        "ctx": nrm(ks[2], (BATCH, CTX_LEN, D_MODEL), 1.0),
        "c_ctx": nrm(ks[3], (D_MODEL,), 1.0),
        "ada_w": nrm(ks[4], (DEPTH, D_MODEL, N_MOD * D_MODEL), 0.5 * D_MODEL ** -0.5),
        "ada_b": nrm(ks[5], (DEPTH, N_MOD * D_MODEL), 0.02),
        "norm_g": 1.0 + nrm(ks[6], (DEPTH, 3, D_MODEL), 0.02),
        "ffn1_w13": nrm(ks[7], (DEPTH, D_MODEL, 2 * D_FF), D_MODEL ** -0.5),
        "ffn1_w2": nrm(ks[8], (DEPTH, D_FF, D_MODEL), D_FF ** -0.5),
        "w_in": nrm(ks[9], (DEPTH, D_MODEL, IN_COLS), D_MODEL ** -0.5),
        "b_merge": nrm(ks[10], (DEPTH, N_BRANCH, D_MODEL), 0.02),
        "rnn_conv_w": nrm(ks[11], (DEPTH, RNN_CONV, RNN_WIDTH), RNN_CONV ** -0.5),
        "rnn_conv_b": nrm(ks[12], (DEPTH, RNN_WIDTH), 0.02),
        "lru_w_a": nrm(ks[13], (DEPTH, 2, RNN_BLOCKS, RNN_BLOCK, RNN_BLOCK), RNN_BLOCK ** -0.5),
        "lru_b_a": nrm(ks[14], (DEPTH, 2, RNN_WIDTH), 0.02),
        "lru_w_x": nrm(ks[16], (DEPTH, 2, RNN_BLOCKS, RNN_BLOCK, RNN_BLOCK), RNN_BLOCK ** -0.5),
        "lru_b_x": nrm(ks[17], (DEPTH, 2, RNN_WIDTH), 0.02),
        "lru_lambda": lam,
        "sc_conv_w": nrm(ks[18], (DEPTH, SC_CONV, SC_WIDTH), SC_CONV ** -0.5),
        "attn_sink": nrm(ks[19], (DEPTH, N_HEADS), 0.5),
        "w_branch": nrm(ks[20], (DEPTH, N_BRANCH, BRANCH_WIDTH, D_MODEL), BRANCH_WIDTH ** -0.5),
        "w_out": nrm(ks[21], (DEPTH, D_MODEL, D_MODEL), D_MODEL ** -0.5),
        "ffn2_w13": nrm(ks[22], (DEPTH, D_MODEL, 2 * D_FF), D_MODEL ** -0.5),
        "ffn2_w2": nrm(ks[23], (DEPTH, D_FF, D_MODEL), D_FF ** -0.5),
        "final_norm_g": 1.0 + nrm(ks[24], (D_MODEL,), 0.02),
    }


def reference(x, c, ctx, c_ctx, ada_w, ada_b, norm_g, ffn1_w13, ffn1_w2, w_in, b_merge,
              rnn_conv_w, rnn_conv_b, lru_w_a, lru_b_a, lru_w_x, lru_b_x, lru_lambda,
              sc_conv_w, attn_sink, w_branch, w_out, ffn2_w13, ffn2_w2, final_norm_g):
    b, l, _ = x.shape
    cos, sin = axial_rope(l)
    silu_c = jax.nn.silu(c)
    silu_cc = jax.nn.silu(c_ctx)
    h, hc = x, ctx
    for layer in range(DEPTH):
        last = layer == DEPTH - 1
        mod = (silu_c @ ada_w[layer] + ada_b[layer]).reshape(b, N_MOD, 1, D_MODEL)
        modc = (silu_cc @ ada_w[layer] + ada_b[layer]).reshape(N_MOD, D_MODEL)

        u = modulate(rmsnorm(h, norm_g[layer, 0]), mod[:, 0], mod[:, 1])
        h = h + 0.5 * mod[:, 2] * swiglu(u, ffn1_w13[layer], ffn1_w2[layer])
        uc = modulate(rmsnorm(hc, norm_g[layer, 0]), modc[0], modc[1])
        hc = hc + 0.5 * modc[2] * swiglu(uc, ffn1_w13[layer], ffn1_w2[layer])

        u = modulate(rmsnorm(h, norm_g[layer, 1]), mod[:, 3], mod[:, 4])
        uc = modulate(rmsnorm(hc, norm_g[layer, 1]), modc[3], modc[4])
        y, yc = token_mixer(u, uc, cos, sin, w_in[layer], b_merge[layer], rnn_conv_w[layer],
                            rnn_conv_b[layer], lru_w_a[layer], lru_b_a[layer], lru_w_x[layer],
                            lru_b_x[layer], lru_lambda[layer], sc_conv_w[layer], attn_sink[layer],
                            w_branch[layer], w_out[layer], not last)
        h = h + mod[:, 5] * y

        u = modulate(rmsnorm(h, norm_g[layer, 2]), mod[:, 6], mod[:, 7])
        h = h + 0.5 * mod[:, 8] * swiglu(u, ffn2_w13[layer], ffn2_w2[layer])
        if not last:
            hc = hc + modc[5] * yc
            uc = modulate(rmsnorm(hc, norm_g[layer, 2]), modc[6], modc[7])
            hc = hc + 0.5 * modc[8] * swiglu(uc, ffn2_w13[layer], ffn2_w2[layer])
    return rmsnorm(h, final_norm_g)
```

```python
import functools

import jax
import jax.numpy as jnp
from jax import lax
from jax.experimental import pallas as pl
from jax.experimental.pallas import tpu as pltpu

F32 = jnp.float32
BF16 = jnp.bfloat16

HEAD_DIM = 128
GROUP = 4
WINDOW = 128
GRID_W = 64
ROPE_BASE = 10000.0
RNN_BLOCK = 128
RNN_CONV = 4
RNN_CONV_LEFT = 2
SC_CONV = 3
SC_CONV_LEFT = 1
LRU_C = 8.0
N_BRANCH = 3
N_MOD = 9
EPS = 1e-6
NEG_INF = -1e30

LANES = 128
SUBLANES_F32 = 8
SUBLANES_BF16 = 16
HALO = SUBLANES_BF16
VMEM_LIMIT_BYTES = 56 * 1024 * 1024


def _cparams(semantics):
    return pltpu.CompilerParams(dimension_semantics=semantics, vmem_limit_bytes=VMEM_LIMIT_BYTES)


def _largest_tile(n, cap, quantum):
    best = None
    t = quantum
    while t <= min(n, cap):
        if n % t == 0:
            best = t
        t += quantum
    assert best is not None, (n, cap, quantum)
    return best


def _rmsnorm(x, g):
    return x * lax.rsqrt(jnp.mean(x * x, axis=-1, keepdims=True) + EPS) * g


def _norm_mod(x, g, shift, scale):
    return _rmsnorm(x, g) * (1.0 + scale) + shift


def _silu(x):
    return x * jax.nn.sigmoid(x)


def _gelu_tanh(x):
    return 0.5 * x * (1.0 + jnp.tanh(0.7978845608028654 * (x + 0.044715 * (x * x * x))))


def _softplus(x):
    return jnp.maximum(x, 0.0) + jnp.log1p(jnp.exp(-jnp.abs(x)))


def _one_minus_exp2(x, ex):
    kahan = (ex - 1.0) * x / jnp.log(ex)
    em1 = jnp.where(x < -1.0, ex - 1.0, jnp.where(ex == 1.0, x, kahan))
    return -em1 * (ex + 1.0)


def _ada_kernel(c_ref, w_ref, b_ref, o_ref):
    s = _silu(c_ref[...]).astype(BF16)
    o_ref[...] = jnp.dot(s, w_ref[...].astype(BF16), preferred_element_type=F32) + b_ref[...]


def _ada_call(cvec, ada_w, ada_b):
    depth, d, nm = ada_w.shape
    tn = _largest_tile(nm, 1024, LANES)
    return pl.pallas_call(
        _ada_kernel,
        grid=(depth, nm // tn),
        in_specs=[
            pl.BlockSpec((8, d), lambda l, j: (0, 0)),
            pl.BlockSpec((None, d, tn), lambda l, j: (l, 0, j)),
            pl.BlockSpec((None, 1, tn), lambda l, j: (l, 0, j)),
        ],
        out_specs=pl.BlockSpec((None, 8, tn), lambda l, j: (l, 0, j)),
        out_shape=jax.ShapeDtypeStruct((depth, 8, nm), F32),
        compiler_params=_cparams(("arbitrary", "arbitrary")),
        name="adaln",
    )(cvec, ada_w, ada_b.reshape(depth, 1, nm))


def _ffn_kernel(h_ref, g_ref, shift_ref, scale_ref, gate_ref, w13_ref, w2_ref, *rest, tf, final_norm):
    if final_norm:
        fg_ref, o_ref, u_ref = rest
    else:
        o_ref, u_ref = rest
    j = pl.program_id(1)

    @pl.when(j == 0)
    def _():
        u_ref[...] = _norm_mod(h_ref[...], g_ref[...], shift_ref[...], scale_ref[...]).astype(BF16)

    gu = jnp.dot(u_ref[...], w13_ref[...], preferred_element_type=F32)
    act = (_silu(gu[:, :tf]) * gu[:, tf:]).astype(BF16)
    part = jnp.dot(act, w2_ref[...], preferred_element_type=F32)

    @pl.when(j == 0)
    def _():
        o_ref[...] = part

    @pl.when(j > 0)
    def _():
        o_ref[...] += part

    @pl.when(j == pl.num_programs(1) - 1)
    def _():
        hn = h_ref[...] + (0.5 * gate_ref[...]) * o_ref[...]
        if final_norm:
            hn = _rmsnorm(hn, fg_ref[...])
        o_ref[...] = hn


def _ffn_call(h, g, shift, scale, gate, w13, w2, tf, final_g=None):
    m, d = h.shape
    nb = shift.shape[0]
    fp = w2.shape[0]
    tm = _largest_tile(m // nb, 512, SUBLANES_F32)
    tpb = (m // nb) // tm
    row = lambda i, j: (i // tpb, 0, 0)
    in_specs = [
        pl.BlockSpec((tm, d), lambda i, j: (i, 0)),
        pl.BlockSpec((1, d), lambda i, j: (0, 0)),
        pl.BlockSpec((None, 1, d), row),
        pl.BlockSpec((None, 1, d), row),
        pl.BlockSpec((None, 1, d), row),
        pl.BlockSpec((d, 2 * tf), lambda i, j: (0, j)),
        pl.BlockSpec((tf, d), lambda i, j: (j, 0)),
    ]
    args = [h, g.reshape(1, d), shift, scale, gate, w13, w2]
    if final_g is not None:
        in_specs.append(pl.BlockSpec((1, d), lambda i, j: (0, 0)))
        args.append(final_g.reshape(1, d))
    return pl.pallas_call(
        functools.partial(_ffn_kernel, tf=tf, final_norm=final_g is not None),
        grid=(m // tm, fp // tf),
        in_specs=in_specs,
        out_specs=pl.BlockSpec((tm, d), lambda i, j: (i, 0)),
        out_shape=jax.ShapeDtypeStruct((m, d), F32),
        scratch_shapes=[pltpu.VMEM((tm, d), BF16)],
        compiler_params=_cparams(("parallel", "arbitrary")),
        name="ffn",
    )(*args)


def _inproj_kernel(h_ref, g_ref, shift_ref, scale_ref, w_ref, o_ref, u_ref):
    @pl.when(pl.program_id(1) == 0)
    def _():
        u_ref[...] = _norm_mod(h_ref[...], g_ref[...], shift_ref[...], scale_ref[...]).astype(BF16)

    o_ref[...] = jnp.dot(u_ref[...], w_ref[...], preferred_element_type=F32).astype(o_ref.dtype)


def _inproj_call(h, g, shift, scale, w):
    m, d = h.shape
    nb = shift.shape[0]
    nc = w.shape[1]
    tm = _largest_tile(m // nb, 1024, SUBLANES_BF16)
    tpb = (m // nb) // tm
    tn = _largest_tile(nc, 1280, LANES)
    row = lambda i, j: (i // tpb, 0, 0)
    return pl.pallas_call(
        _inproj_kernel,
        grid=(m // tm, nc // tn),
        in_specs=[
            pl.BlockSpec((tm, d), lambda i, j: (i, 0)),
            pl.BlockSpec((1, d), lambda i, j: (0, 0)),
            pl.BlockSpec((None, 1, d), row),
            pl.BlockSpec((None, 1, d), row),
            pl.BlockSpec((d, tn), lambda i, j: (0, j)),
        ],
        out_specs=pl.BlockSpec((tm, tn), lambda i, j: (i, j)),
        out_shape=jax.ShapeDtypeStruct((m, nc), BF16),
        scratch_shapes=[pltpu.VMEM((tm, d), BF16)],
        compiler_params=_cparams(("parallel", "arbitrary")),
        name="in_proj",
    )(h, g.reshape(1, d), shift, scale, w)


def _fill_padded(pad_ref, main, prev_tail, next_head, has_prev, has_next, t):
    pad_ref[0:8, :] = jnp.where(has_prev, prev_tail, 0.0)
    pad_ref[8:8 + t, :] = main
    pad_ref[8 + t:16 + t, :] = jnp.where(has_next, next_head, 0.0)


def _rglru_kernel(zf_ref, zfp_ref, zfn_ref, zb_ref, zbp_ref, zbn_ref, cw_ref, cb_ref, wa_ref, ba_ref,
                  wx_ref, bx_ref, lam_ref, h0_ref, of_ref, ob_ref, last_ref,
                  xpad_ref, xa_ref, a_ref, u_ref, hs_ref, car_ref, *, t, n_t, n_blk):
    i = pl.program_id(1)

    @pl.when(i == 0)
    def _():
        car_ref[...] = h0_ref[...]

    sp = _softplus(-lam_ref[...])

    def gates(d, z_ref, zp_ref, zn_ref, tile):
        _fill_padded(xpad_ref, z_ref[...].astype(F32), zp_ref[...].astype(F32)[8:16],
                     zn_ref[...].astype(F32)[0:8], tile > 0, tile < n_t - 1, t)
        xa = cb_ref[...]
        for k in range(RNN_CONV):
            off = 8 - RNN_CONV_LEFT + k
            xa = xa + cw_ref[k:k + 1, :] * xpad_ref[off:off + t, :]
        xa_ref[...] = xa
        for n in range(n_blk):
            sl = slice(n * RNN_BLOCK, (n + 1) * RNN_BLOCK)
            xs = xa_ref[:, sl]
            xb = xs.astype(BF16)
            rg = jax.nn.sigmoid(jnp.dot(xb, wa_ref[d, n], preferred_element_type=F32) + ba_ref[d:d + 1, sl])
            ig = jax.nn.sigmoid(jnp.dot(xb, wx_ref[d, n], preferred_element_type=F32) + bx_ref[d:d + 1, sl])
            log_a = (-LRU_C * rg) * sp[d:d + 1, sl]
            a = jnp.exp(log_a)
            a_ref[d, :, sl] = a
            u_ref[d, :, sl] = jnp.sqrt(_one_minus_exp2(log_a, a)) * (ig * xs)

    gates(0, zf_ref, zfp_ref, zfn_ref, i)
    gates(1, zb_ref, zbp_ref, zbn_ref, n_t - 1 - i)

    def step(s, carry):
        hf, hb = carry
        sb = t - 1 - s
        hf = a_ref[0, pl.ds(s, 1), :] * hf + u_ref[0, pl.ds(s, 1), :]
        hb = a_ref[1, pl.ds(sb, 1), :] * hb + u_ref[1, pl.ds(sb, 1), :]
        hs_ref[0, pl.ds(s, 1), :] = hf
        hs_ref[1, pl.ds(sb, 1), :] = hb
        return hf, hb

    hf, hb = lax.fori_loop(0, t, step, (car_ref[0:1, :], car_ref[1:2, :]), unroll=8)
    car_ref[0:1, :] = hf
    car_ref[1:2, :] = hb
    of_ref[...] = hs_ref[0].astype(of_ref.dtype)
    ob_ref[...] = hs_ref[1].astype(ob_ref.dtype)
    last_ref[0:1, :] = hf
    last_ref[1:2, :] = hb


def _rglru_call(z, col_rx, cw, cb, wa, ba, wx, bx, lam, h0):
    b, l, _ = z.shape
    c = cw.shape[1]
    n_blk = c // RNN_BLOCK
    t = _largest_tile(l, 512, HALO)
    n_t = l // t
    hb_per = t // HALO
    n_h = l // HALO

    def main(rev):
        return lambda bb, i: (bb, (n_t - 1 - i) if rev else i, col_rx)

    def prev(rev):
        return lambda bb, i: (bb, jnp.maximum(((n_t - 1 - i) if rev else i) * hb_per - 1, 0), col_rx)

    def nxt(rev):
        return lambda bb, i: (bb, jnp.minimum((((n_t - 1 - i) if rev else i) + 1) * hb_per, n_h - 1), col_rx)

    const2 = lambda bb, i: (0, 0)
    const4 = lambda bb, i: (0, 0, 0, 0)
    in_specs = []
    for rev in (False, True):
        in_specs += [pl.BlockSpec((None, t, c), main(rev)),
                     pl.BlockSpec((None, HALO, c), prev(rev)),
                     pl.BlockSpec((None, HALO, c), nxt(rev))]
    in_specs += [
        pl.BlockSpec((RNN_CONV, c), const2),
        pl.BlockSpec((1, c), const2),
        pl.BlockSpec((2, n_blk, RNN_BLOCK, RNN_BLOCK), const4),
        pl.BlockSpec((2, c), const2),
        pl.BlockSpec((2, n_blk, RNN_BLOCK, RNN_BLOCK), const4),
        pl.BlockSpec((2, c), const2),
        pl.BlockSpec((2, c), const2),
        pl.BlockSpec((None, 2, c), lambda bb, i: (bb, 0, 0)),
    ]
    return pl.pallas_call(
        functools.partial(_rglru_kernel, t=t, n_t=n_t, n_blk=n_blk),
        grid=(b, n_t),
        in_specs=in_specs,
        out_specs=[
            pl.BlockSpec((None, t, c), lambda bb, i: (bb, i, 0)),
            pl.BlockSpec((None, t, c), lambda bb, i: (bb, n_t - 1 - i, 0)),
            pl.BlockSpec((None, 2, c), lambda bb, i: (bb, 0, 0)),
        ],
        out_shape=[
            jax.ShapeDtypeStruct((b, l, c), BF16),
            jax.ShapeDtypeStruct((b, l, c), BF16),
            jax.ShapeDtypeStruct((b, 2, c), F32),
        ],
        scratch_shapes=[
            pltpu.VMEM((t + 16, c), F32),
            pltpu.VMEM((t, c), F32),
            pltpu.VMEM((2, t, c), F32),
            pltpu.VMEM((2, t, c), F32),
            pltpu.VMEM((2, t, c), F32),
            pltpu.VMEM((2, c), F32),
        ],
        compiler_params=_cparams(("parallel", "arbitrary")),
        name="rglru",
    )(z, z, z, z, z, z, cw, cb.reshape(1, c), wa, ba, wx, bx, lam, h0)


def _rope(x, cos_t, sin_t):
    lane = lax.broadcasted_iota(jnp.int32, x.shape, 1)
    partner = jnp.where((lane & 32) == 0, pltpu.roll(x, LANES - 32, 1), pltpu.roll(x, 32, 1))
    return x * cos_t + partner * sin_t


def _softmax_pv(s_parts, v_parts, sink):
    m = sink
    for s in s_parts:
        m = jnp.maximum(m, jnp.max(s, axis=-1, keepdims=True))
    den = jnp.exp(sink - m)
    acc = None
    for s, v in zip(s_parts, v_parts):
        p = jnp.exp(s - m)
        den = den + jnp.sum(p, axis=-1, keepdims=True)
        pv = jnp.dot(p.astype(BF16), v, preferred_element_type=F32)
        acc = pv if acc is None else acc + pv
    return acc / den


def _attn_kernel(sink_ref, q_ref, *rest, tq, n_q, n_kv, seq, local):
    if local:
        (km_ref, kp_ref, kn_ref, vm_ref, vp_ref, vn_ref, kc_ref, vc_ref,
         cm_ref, cp_ref, cn_ref, sm_ref, sp_ref, sn_ref, o_ref) = rest
    else:
        kc_ref, vc_ref, o_ref = rest
    i = pl.program_id(1)
    scale = HEAD_DIM ** -0.5
    nt = (((1,), (1,)), ((), ()))

    if local:
        cos_q, sin_q = cm_ref[...], sm_ref[...]
        cos_k = jnp.concatenate([cp_ref[...], cos_q, cn_ref[...]], axis=0)
        sin_k = jnp.concatenate([sp_ref[...], sin_q, sn_ref[...]], axis=0)
        span = tq + 2 * WINDOW
        qi = lax.broadcasted_iota(jnp.int32, (tq, span), 0)
        kj = lax.broadcasted_iota(jnp.int32, (tq, span), 1)
        kpos = i * tq - WINDOW + kj
        valid = (jnp.abs(qi + WINDOW - kj) <= WINDOW) & (kpos >= 0) & (kpos < seq)
        bias = jnp.where(valid, 0.0, NEG_INF)

    for kh in range(n_kv):
        ks = slice(kh * HEAD_DIM, (kh + 1) * HEAD_DIM)
        kc = kc_ref[:, ks]
        vc = vc_ref[:, ks]
        qs = []
        for g in range(GROUP):
            h = kh * GROUP + g
            qh = q_ref[:, h * HEAD_DIM:(h + 1) * HEAD_DIM]
            if local:
                qh = _rope(qh.astype(F32), cos_q, sin_q).astype(BF16)
            qs.append(qh)
        qs = jnp.concatenate(qs, axis=0)
        s_ctx = lax.dot_general(qs, kc, nt, preferred_element_type=F32) * scale
        if local:
            k_span = jnp.concatenate([kp_ref[:, ks], km_ref[:, ks], kn_ref[:, ks]], axis=0)
            k_span = _rope(k_span.astype(F32), cos_k, sin_k).astype(BF16)
            v_span = jnp.concatenate([vp_ref[:, ks], vm_ref[:, ks], vn_ref[:, ks]], axis=0)
            s_loc = lax.dot_general(qs, k_span, nt, preferred_element_type=F32) * scale
        for g in range(GROUP):
            h = kh * GROUP + g
            rows = slice(g * tq, (g + 1) * tq)
            sink = sink_ref[h]
            if local:
                o = _softmax_pv([s_loc[rows] + bias, s_ctx[rows]], [v_span, vc], sink)
            else:
                o = _softmax_pv([s_ctx[rows]], [vc], sink)
            o_ref[:, h * HEAD_DIM:(h + 1) * HEAD_DIM] = o.astype(o_ref.dtype)


def _attn_call(sink, z, zc, cols, cos_t, sin_t, local):
    b, l, _ = z.shape
    n_ctx = zc.shape[1]
    n_heads = sink.shape[0]
    n_kv = n_heads // GROUP
    qw, kw = n_heads * HEAD_DIM, n_kv * HEAD_DIM
    col_q, col_k, col_v = cols
    tq = _largest_tile(l, 256, WINDOW)
    n_q = l // tq
    wb = tq // WINDOW
    n_w = l // WINDOW
    prev = lambda bb, i: jnp.maximum(i * wb - 1, 0)
    nxt = lambda bb, i: jnp.minimum((i + 1) * wb, n_w - 1)
    in_specs = [
        pl.BlockSpec(memory_space=pltpu.SMEM),
        pl.BlockSpec((None, tq, qw), lambda bb, i: (bb, i, col_q)),
    ]
    args = [sink, z]
    if local:
        for col in (col_k, col_v):
            in_specs += [
                pl.BlockSpec((None, tq, kw), lambda bb, i, col=col: (bb, i, col)),
                pl.BlockSpec((None, WINDOW, kw), lambda bb, i, col=col: (bb, prev(bb, i), col)),
                pl.BlockSpec((None, WINDOW, kw), lambda bb, i, col=col: (bb, nxt(bb, i), col)),
            ]
            args += [z, z, z]
    in_specs += [
        pl.BlockSpec((None, n_ctx, kw), lambda bb, i: (bb, 0, col_k)),
        pl.BlockSpec((None, n_ctx, kw), lambda bb, i: (bb, 0, col_v)),
    ]
    args += [zc, zc]
    if local:
        for tab in (cos_t, sin_t):
            in_specs += [
                pl.BlockSpec((tq, HEAD_DIM), lambda bb, i: (i, 0)),
                pl.BlockSpec((WINDOW, HEAD_DIM), lambda bb, i: (prev(bb, i), 0)),
                pl.BlockSpec((WINDOW, HEAD_DIM), lambda bb, i: (nxt(bb, i), 0)),
            ]
            args += [tab, tab, tab]
    return pl.pallas_call(
        functools.partial(_attn_kernel, tq=tq, n_q=n_q, n_kv=n_kv, seq=l, local=local),
        grid=(b, n_q),
        in_specs=in_specs,
        out_specs=pl.BlockSpec((None, tq, qw), lambda bb, i: (bb, i, 0)),
        out_shape=jax.ShapeDtypeStruct((b, l, qw), BF16),
        compiler_params=_cparams(("parallel", "parallel")),
        name="attn_local" if local else "attn_ctx",
    )(*args)


def _merge_kernel(h_ref, gate_ref, hf_ref, hb_ref, rg_ref, sb_ref, cg_ref, cgp_ref, cgn_ref,
                  sx_ref, sxp_ref, sxn_ref, att_ref, g_ref, bm_ref, scw_ref, wb_ref, wo_ref,
                  o_ref, ppad_ref, *, tm, n_t, d):
    i = pl.program_id(1)
    ya = (hf_ref[...].astype(F32) + hb_ref[...].astype(F32)) * _gelu_tanh(rg_ref[...].astype(F32))

    _fill_padded(ppad_ref,
                 cg_ref[...].astype(F32) * sx_ref[...].astype(F32),
                 (cgp_ref[...].astype(F32) * sxp_ref[...].astype(F32))[8:16],
                 (cgn_ref[...].astype(F32) * sxn_ref[...].astype(F32))[0:8],
                 i > 0, i < n_t - 1, tm)
    conv = None
    for k in range(SC_CONV):
        off = 8 - SC_CONV_LEFT + k
        term = scw_ref[k:k + 1, :] * ppad_ref[off:off + tm, :]
        conv = term if conv is None else conv + term
    yb = sb_ref[...].astype(F32) * conv

    ys = (ya.astype(BF16), yb.astype(BF16), att_ref[...])
    merged = None
    for br in range(N_BRANCH):
        gates = jax.nn.sigmoid(g_ref[:, br * d:(br + 1) * d].astype(F32) + bm_ref[br:br + 1, :])
        term = gates * jnp.dot(ys[br], wb_ref[br], preferred_element_type=F32)
        merged = term if merged is None else merged + term
    y = jnp.dot(merged.astype(BF16), wo_ref[...], preferred_element_type=F32)
    o_ref[...] = h_ref[...] + gate_ref[...] * y


def _merge_call(h, gate, hf, hb, z, att, cols, b_merge, sc_w, w_branch, w_out):
    b, l, d = h.shape
    bw = hf.shape[2]
    col_g, col_rg, col_sb, col_cg, col_sx = cols
    tm = _largest_tile(l, 256, HALO)
    n_t = l // tm
    hb_per = tm // HALO
    n_h = l // HALO
    per_batch = gate.shape[0] == b
    tile = lambda col: (lambda bb, i: (bb, i, col))
    prev = lambda col: (lambda bb, i: (bb, jnp.maximum(i * hb_per - 1, 0), col))
    nxt = lambda col: (lambda bb, i: (bb, jnp.minimum((i + 1) * hb_per, n_h - 1), col))
    once = pl.Buffered(1)
    in_specs = [
        pl.BlockSpec((None, tm, d), tile(0)),
        pl.BlockSpec((None, 1, d), lambda bb, i: (bb if per_batch else 0, 0, 0)),
        pl.BlockSpec((None, tm, bw), tile(0)),
        pl.BlockSpec((None, tm, bw), tile(0)),
        pl.BlockSpec((None, tm, bw), tile(col_rg)),
        pl.BlockSpec((None, tm, bw), tile(col_sb)),
        pl.BlockSpec((None, tm, bw), tile(col_cg)),
        pl.BlockSpec((None, HALO, bw), prev(col_cg)),
        pl.BlockSpec((None, HALO, bw), nxt(col_cg)),
        pl.BlockSpec((None, tm, bw), tile(col_sx)),
        pl.BlockSpec((None, HALO, bw), prev(col_sx)),
        pl.BlockSpec((None, HALO, bw), nxt(col_sx)),
        pl.BlockSpec((None, tm, bw), tile(0)),
        pl.BlockSpec((None, tm, N_BRANCH * d), tile(col_g)),
        pl.BlockSpec((N_BRANCH, d), lambda bb, i: (0, 0)),
        pl.BlockSpec((SC_CONV, bw), lambda bb, i: (0, 0)),
        pl.BlockSpec((N_BRANCH, bw, d), lambda bb, i: (0, 0, 0), pipeline_mode=once),
        pl.BlockSpec((d, d), lambda bb, i: (0, 0), pipeline_mode=once),
    ]
    return pl.pallas_call(
        functools.partial(_merge_kernel, tm=tm, n_t=n_t, d=d),
        grid=(b, n_t),
        in_specs=in_specs,
        out_specs=pl.BlockSpec((None, tm, d), tile(0)),
        out_shape=jax.ShapeDtypeStruct((b, l, d), F32),
        scratch_shapes=[pltpu.VMEM((tm + 16, bw), F32)],
        compiler_params=_cparams(("parallel", "parallel")),
        name="merge",
    )(h, gate, hf, hb, z, z, z, z, z, z, z, z, att, z, b_merge, sc_w, w_branch, w_out)


def _prep_ffn(w13, w2, tf):
    d, f2 = w13.shape
    f = f2 // 2
    fp = -(-f // tf) * tf
    pad = ((0, 0), (0, fp - f))
    gate = jnp.pad(w13[:, :f].astype(BF16), pad).reshape(d, fp // tf, 1, tf)
    up = jnp.pad(w13[:, f:].astype(BF16), pad).reshape(d, fp // tf, 1, tf)
    w13p = jnp.concatenate([gate, up], axis=2).reshape(d, 2 * fp)
    w2p = jnp.pad(w2.astype(BF16), ((0, fp - f), (0, 0)))
    return w13p, w2p


def _prep_w_in(w_in, d, bw, kw):
    o = 0
    pieces = []
    for n in (bw, bw, bw, bw, bw, bw, kw, kw, N_BRANCH * d):
        pieces.append(w_in[:, o:o + n])
        o += n
    return jnp.concatenate([pieces[8]] + pieces[:8], axis=1).astype(BF16)


def _rope_tables(l):
    pos = jnp.arange(l)
    row = (pos // GRID_W).astype(F32)
    col = (pos % GRID_W).astype(F32)
    half = HEAD_DIM // 2
    inv = ROPE_BASE ** (-jnp.arange(0, half, 2, dtype=F32) / half)
    ar, ac = row[:, None] * inv, col[:, None] * inv
    cos_t = jnp.concatenate([jnp.cos(ar), jnp.cos(ar), jnp.cos(ac), jnp.cos(ac)], axis=-1)
    sin_t = jnp.concatenate([-jnp.sin(ar), jnp.sin(ar), -jnp.sin(ac), jnp.sin(ac)], axis=-1)
    return cos_t, sin_t


def kernel(x, c, ctx, c_ctx, ada_w, ada_b, norm_g, ffn1_w13, ffn1_w2, w_in, b_merge, rnn_conv_w,
           rnn_conv_b, lru_w_a, lru_b_a, lru_w_x, lru_b_x, lru_lambda, sc_conv_w, attn_sink, w_branch,
           w_out, ffn2_w13, ffn2_w2, final_norm_g):
    b, l, d = x.shape
    n_ctx = ctx.shape[1]
    depth = ada_w.shape[0]
    bw = w_branch.shape[2]
    n_heads = attn_sink.shape[1]
    kw = (n_heads // GROUP) * HEAD_DIM
    f = ffn1_w2.shape[1]
    assert bw == n_heads * HEAD_DIM and w_in.shape[2] == 6 * bw + 2 * kw + N_BRANCH * d
    assert (N_BRANCH * d) % bw == 0 and (N_BRANCH * d + 6 * bw) % kw == 0 and b + 1 <= 8
    g_blocks = N_BRANCH * d // bw
    col_rx, col_rg, col_sb, col_cg, col_sx, col_q = (g_blocks + n for n in range(6))
    col_k = (N_BRANCH * d + 6 * bw) // kw
    col_v = col_k + 1
    tf = 512 if f >= 2048 else 256

    cvec = jnp.zeros((8, d), F32).at[:b].set(c).at[b].set(c_ctx)
    mod = _ada_call(cvec, ada_w, ada_b).reshape(depth, 8, N_MOD, 1, d)
    cos_t, sin_t = _rope_tables(l)

    h = x.reshape(b * l, d)
    hc = ctx.reshape(b * n_ctx, d)
    for layer in range(depth):
        last = layer == depth - 1
        ml = mod[layer, :b]
        mc = mod[layer, b:b + 1]
        ng = norm_g[layer]
        w13a, w2a = _prep_ffn(ffn1_w13[layer], ffn1_w2[layer], tf)
        w13b, w2b = _prep_ffn(ffn2_w13[layer], ffn2_w2[layer], tf)
        w_in_p = _prep_w_in(w_in[layer], d, bw, kw)
        wa, wx = lru_w_a[layer].astype(BF16), lru_w_x[layer].astype(BF16)
        wbr, wo = w_branch[layer].astype(BF16), w_out[layer].astype(BF16)

        h = _ffn_call(h, ng[0], ml[:, 0], ml[:, 1], ml[:, 2], w13a, w2a, tf)
        hc = _ffn_call(hc, ng[0], mc[:, 0], mc[:, 1], mc[:, 2], w13a, w2a, tf)

        z = _inproj_call(h, ng[1], ml[:, 3], ml[:, 4], w_in_p).reshape(b, l, -1)
        zc = _inproj_call(hc, ng[1], mc[:, 3], mc[:, 4], w_in_p).reshape(b, n_ctx, -1)
        lru = (rnn_conv_w[layer], rnn_conv_b[layer], wa, lru_b_a[layer], wx, lru_b_x[layer], lru_lambda[layer])
        hcf, hcb, h_last = _rglru_call(zc, col_rx, *lru, jnp.zeros((b, 2, bw), F32))
        hlf, hlb, _ = _rglru_call(z, col_rx, *lru, h_last)
        att = _attn_call(attn_sink[layer], z, zc, (col_q, col_k, col_v), cos_t, sin_t, True)
        merge_cols = (0, col_rg, col_sb, col_cg, col_sx)
        h = _merge_call(h.reshape(b, l, d), ml[:, 5], hlf, hlb, z, att, merge_cols,
                        b_merge[layer], sc_conv_w[layer], wbr, wo).reshape(b * l, d)

        h = _ffn_call(h, ng[2], ml[:, 6], ml[:, 7], ml[:, 8], w13b, w2b, tf,
                      final_g=final_norm_g if last else None)
        if not last:
            attc = _attn_call(attn_sink[layer], zc, zc, (col_q, col_k, col_v), cos_t, sin_t, False)
            hc = _merge_call(hc.reshape(b, n_ctx, d), mc[:, 5], hcf, hcb, zc, attc, merge_cols,
                             b_merge[layer], sc_conv_w[layer], wbr, wo).reshape(b * n_ctx, d)
            hc = _ffn_call(hc, ng[2], mc[:, 6], mc[:, 7], mc[:, 8], w13b, w2b, tf)
    return h.reshape(b, l, d)
```

```python
import functools

import jax
import jax.numpy as jnp
from jax import lax
from jax.experimental import pallas as pl
from jax.experimental.pallas import tpu as pltpu

F32 = jnp.float32
BF16 = jnp.bfloat16

HEAD_DIM = 128
GROUP = 4
WINDOW = 128
GRID_W = 64
ROPE_BASE = 10000.0
RNN_BLOCK = 128
RNN_CONV = 4
RNN_CONV_LEFT = 2
SC_CONV = 3
SC_CONV_LEFT = 1
LRU_C = 8.0
N_BRANCH = 3
N_MOD = 9
EPS = 1e-6
NEG_INF = -1e30

LANES = 128
SUBLANES_F32 = 8
SUBLANES_BF16 = 16
HALO = SUBLANES_BF16
VMEM_LIMIT_BYTES = 56 * 1024 * 1024
FFN_CHUNK = 1024


def _cparams(semantics):
    return pltpu.CompilerParams(dimension_semantics=semantics, vmem_limit_bytes=VMEM_LIMIT_BYTES)


def _largest_tile(n, cap, quantum):
    best = None
    t = quantum
    while t <= min(n, cap):
        if n % t == 0:
            best = t
        t += quantum
    assert best is not None, (n, cap, quantum)
    return best


def _rmsnorm(x, g):
    return x * lax.rsqrt(jnp.mean(x * x, axis=-1, keepdims=True) + EPS) * g


def _norm_mod(x, g, shift, scale):
    return _rmsnorm(x, g) * (1.0 + scale) + shift


def _silu(x):
    return x * jax.nn.sigmoid(x)


def _gelu_tanh(x):
    return 0.5 * x * (1.0 + jnp.tanh(0.7978845608028654 * (x + 0.044715 * (x * x * x))))


def _softplus(x):
    return jnp.maximum(x, 0.0) + jnp.log1p(jnp.exp(-jnp.abs(x)))


def _one_minus_exp2(x, ex):
    kahan = (ex - 1.0) * x / jnp.log(ex)
    em1 = jnp.where(x < -1.0, ex - 1.0, jnp.where(ex == 1.0, x, kahan))
    return -em1 * (ex + 1.0)


def _ada_kernel(c_ref, w_ref, b_ref, o_ref):
    s = _silu(c_ref[...]).astype(BF16)
    o_ref[...] = jnp.dot(s, w_ref[...].astype(BF16), preferred_element_type=F32) + b_ref[...]


def _ada_call(cvec, ada_w, ada_b):
    depth, d, nm = ada_w.shape
    tn = _largest_tile(nm, 1024, LANES)
    return pl.pallas_call(
        _ada_kernel,
        grid=(depth, nm // tn),
        in_specs=[
            pl.BlockSpec((8, d), lambda l, j: (0, 0)),
            pl.BlockSpec((None, d, tn), lambda l, j: (l, 0, j)),
            pl.BlockSpec((None, 1, tn), lambda l, j: (l, 0, j)),
        ],
        out_specs=pl.BlockSpec((None, 8, tn), lambda l, j: (l, 0, j)),
        out_shape=jax.ShapeDtypeStruct((depth, 8, nm), F32),
        compiler_params=_cparams(("arbitrary", "arbitrary")),
        name="adaln",
    )(cvec, ada_w, ada_b.reshape(depth, 1, nm))


def _swiglu_chunk(u, wg_ref, wu_ref, w2_ref):
    gate = jnp.dot(u, wg_ref[...], preferred_element_type=F32)
    up = jnp.dot(u, wu_ref[...], preferred_element_type=F32)
    act = (_silu(gate) * up).astype(BF16)
    return jnp.dot(act, w2_ref[...], preferred_element_type=F32)


def _ffn_kernel(h_ref, g_ref, shift_ref, scale_ref, gate_ref, wg_ref, wu_ref, w2_ref, *rest, tail, final_norm):
    rest = list(rest)
    wgt_ref, wut_ref, w2t_ref = (rest.pop(0), rest.pop(0), rest.pop(0)) if tail else (None, None, None)
    fg_ref = rest.pop(0) if final_norm else None
    o_ref, u_ref = rest
    j = pl.program_id(1)

    @pl.when(j == 0)
    def _():
        u = _norm_mod(h_ref[...], g_ref[...], shift_ref[...], scale_ref[...]).astype(BF16)
        u_ref[...] = u
        o_ref[...] = _swiglu_chunk(u, wgt_ref, wut_ref, w2t_ref) if tail else jnp.zeros_like(o_ref)

    o_ref[...] += _swiglu_chunk(u_ref[...], wg_ref, wu_ref, w2_ref)

    @pl.when(j == pl.num_programs(1) - 1)
    def _():
        hn = h_ref[...] + (0.5 * gate_ref[...]) * o_ref[...]
        if final_norm:
            hn = _rmsnorm(hn, fg_ref[...])
        o_ref[...] = hn


def _ffn_call(h, g, shift, scale, gate, weights, final_g=None):
    wg, wu, w2, tails = weights
    m, d = h.shape
    nb = shift.shape[0]
    tf = min(FFN_CHUNK, wg.shape[1])
    n_chunks = wg.shape[1] // tf
    tm = _largest_tile(m // nb, 512, SUBLANES_F32)
    tpb = (m // nb) // tm
    row = lambda i, j: (i // tpb, 0, 0)
    const = lambda i, j: (0, 0)
    once = pl.Buffered(1)
    in_specs = [
        pl.BlockSpec((tm, d), lambda i, j: (i, 0)),
        pl.BlockSpec((1, d), const),
        pl.BlockSpec((None, 1, d), row),
        pl.BlockSpec((None, 1, d), row),
        pl.BlockSpec((None, 1, d), row),
        pl.BlockSpec((d, tf), lambda i, j: (0, j)),
        pl.BlockSpec((d, tf), lambda i, j: (0, j)),
        pl.BlockSpec((tf, d), lambda i, j: (j, 0)),
    ]
    args = [h, g.reshape(1, d), shift, scale, gate, wg, wu, w2]
    if tails is not None:
        for w in tails:
            in_specs.append(pl.BlockSpec(w.shape, const, pipeline_mode=once))
            args.append(w)
    if final_g is not None:
        in_specs.append(pl.BlockSpec((1, d), const))
        args.append(final_g.reshape(1, d))
    return pl.pallas_call(
        functools.partial(_ffn_kernel, tail=tails is not None, final_norm=final_g is not None),
        grid=(m // tm, n_chunks),
        in_specs=in_specs,
        out_specs=pl.BlockSpec((tm, d), lambda i, j: (i, 0)),
        out_shape=jax.ShapeDtypeStruct((m, d), F32),
        scratch_shapes=[pltpu.VMEM((tm, d), BF16)],
        compiler_params=_cparams(("parallel", "arbitrary")),
        name="ffn",
    )(*args)


def _inproj_kernel(h_ref, g_ref, shift_ref, scale_ref, w_ref, o_ref, u_ref):
    @pl.when(pl.program_id(1) == 0)
    def _():
        u_ref[...] = _norm_mod(h_ref[...], g_ref[...], shift_ref[...], scale_ref[...]).astype(BF16)

    o_ref[...] = jnp.dot(u_ref[...], w_ref[...], preferred_element_type=F32).astype(o_ref.dtype)


def _inproj_call(h, g, shift, scale, w):
    m, d = h.shape
    nb = shift.shape[0]
    nc = w.shape[1]
    tm = _largest_tile(m // nb, 1024, SUBLANES_BF16)
    tpb = (m // nb) // tm
    tn = _largest_tile(nc, 1280, LANES)
    row = lambda i, j: (i // tpb, 0, 0)
    return pl.pallas_call(
        _inproj_kernel,
        grid=(m // tm, nc // tn),
        in_specs=[
            pl.BlockSpec((tm, d), lambda i, j: (i, 0)),
            pl.BlockSpec((1, d), lambda i, j: (0, 0)),
            pl.BlockSpec((None, 1, d), row),
            pl.BlockSpec((None, 1, d), row),
            pl.BlockSpec((d, tn), lambda i, j: (0, j)),
        ],
        out_specs=pl.BlockSpec((tm, tn), lambda i, j: (i, j)),
        out_shape=jax.ShapeDtypeStruct((m, nc), BF16),
        scratch_shapes=[pltpu.VMEM((tm, d), BF16)],
        compiler_params=_cparams(("parallel", "arbitrary")),
        name="in_proj",
    )(h, g.reshape(1, d), shift, scale, w)


def _fill_padded(pad_ref, main, prev_tail, next_head, has_prev, has_next, t):
    pad_ref[0:8, :] = jnp.where(has_prev, prev_tail, 0.0)
    pad_ref[8:8 + t, :] = main
    pad_ref[8 + t:16 + t, :] = jnp.where(has_next, next_head, 0.0)


def _rglru_kernel(zf_ref, zfp_ref, zfn_ref, zb_ref, zbp_ref, zbn_ref, cw_ref, cb_ref, wa_ref, ba_ref,
                  wx_ref, bx_ref, lam_ref, h0_ref, of_ref, ob_ref, last_ref,
                  xpad_ref, xa_ref, a_ref, u_ref, hs_ref, car_ref, *, t, n_t, n_blk):
    i = pl.program_id(1)

    @pl.when(i == 0)
    def _():
        car_ref[...] = h0_ref[...]

    sp = _softplus(-lam_ref[...])

    def gates(d, z_ref, zp_ref, zn_ref, tile):
        _fill_padded(xpad_ref, z_ref[...].astype(F32), zp_ref[...].astype(F32)[8:16],
                     zn_ref[...].astype(F32)[0:8], tile > 0, tile < n_t - 1, t)
        xa = cb_ref[...]
        for k in range(RNN_CONV):
            off = 8 - RNN_CONV_LEFT + k
            xa = xa + cw_ref[k:k + 1, :] * xpad_ref[off:off + t, :]
        xa_ref[...] = xa
        for n in range(n_blk):
            sl = slice(n * RNN_BLOCK, (n + 1) * RNN_BLOCK)
            xs = xa_ref[:, sl]
            xb = xs.astype(BF16)
            rg = jax.nn.sigmoid(jnp.dot(xb, wa_ref[d, n], preferred_element_type=F32) + ba_ref[d:d + 1, sl])
            ig = jax.nn.sigmoid(jnp.dot(xb, wx_ref[d, n], preferred_element_type=F32) + bx_ref[d:d + 1, sl])
            log_a = (-LRU_C * rg) * sp[d:d + 1, sl]
            a = jnp.exp(log_a)
            a_ref[d, :, sl] = a
            u_ref[d, :, sl] = jnp.sqrt(_one_minus_exp2(log_a, a)) * (ig * xs)

    gates(0, zf_ref, zfp_ref, zfn_ref, i)
    gates(1, zb_ref, zbp_ref, zbn_ref, n_t - 1 - i)

    def step(s, carry):
        hf, hb = carry
        sb = t - 1 - s
        hf = a_ref[0, pl.ds(s, 1), :] * hf + u_ref[0, pl.ds(s, 1), :]
        hb = a_ref[1, pl.ds(sb, 1), :] * hb + u_ref[1, pl.ds(sb, 1), :]
        hs_ref[0, pl.ds(s, 1), :] = hf
        hs_ref[1, pl.ds(sb, 1), :] = hb
        return hf, hb

    hf, hb = lax.fori_loop(0, t, step, (car_ref[0:1, :], car_ref[1:2, :]), unroll=8)
    car_ref[0:1, :] = hf
    car_ref[1:2, :] = hb
    of_ref[...] = hs_ref[0].astype(of_ref.dtype)
    ob_ref[...] = hs_ref[1].astype(ob_ref.dtype)
    last_ref[0:1, :] = hf
    last_ref[1:2, :] = hb


def _rglru_call(z, col_rx, cw, cb, wa, ba, wx, bx, lam, h0):
    b, l, _ = z.shape
    c = cw.shape[1]
    n_blk = c // RNN_BLOCK
    t = _largest_tile(l, 512, HALO)
    n_t = l // t
    hb_per = t // HALO
    n_h = l // HALO

    def main(rev):
        return lambda bb, i: (bb, (n_t - 1 - i) if rev else i, col_rx)

    def prev(rev):
        return lambda bb, i: (bb, jnp.maximum(((n_t - 1 - i) if rev else i) * hb_per - 1, 0), col_rx)

    def nxt(rev):
        return lambda bb, i: (bb, jnp.minimum((((n_t - 1 - i) if rev else i) + 1) * hb_per, n_h - 1), col_rx)

    const2 = lambda bb, i: (0, 0)
    const4 = lambda bb, i: (0, 0, 0, 0)
    in_specs = []
    for rev in (False, True):
        in_specs += [pl.BlockSpec((None, t, c), main(rev)),
                     pl.BlockSpec((None, HALO, c), prev(rev)),
                     pl.BlockSpec((None, HALO, c), nxt(rev))]
    in_specs += [
        pl.BlockSpec((RNN_CONV, c), const2),
        pl.BlockSpec((1, c), const2),
        pl.BlockSpec((2, n_blk, RNN_BLOCK, RNN_BLOCK), const4),
        pl.BlockSpec((2, c), const2),
        pl.BlockSpec((2, n_blk, RNN_BLOCK, RNN_BLOCK), const4),
        pl.BlockSpec((2, c), const2),
        pl.BlockSpec((2, c), const2),
        pl.BlockSpec((None, 2, c), lambda bb, i: (bb, 0, 0)),
    ]
    return pl.pallas_call(
        functools.partial(_rglru_kernel, t=t, n_t=n_t, n_blk=n_blk),
        grid=(b, n_t),
        in_specs=in_specs,
        out_specs=[
            pl.BlockSpec((None, t, c), lambda bb, i: (bb, i, 0)),
            pl.BlockSpec((None, t, c), lambda bb, i: (bb, n_t - 1 - i, 0)),
            pl.BlockSpec((None, 2, c), lambda bb, i: (bb, 0, 0)),
        ],
        out_shape=[
            jax.ShapeDtypeStruct((b, l, c), BF16),
            jax.ShapeDtypeStruct((b, l, c), BF16),
            jax.ShapeDtypeStruct((b, 2, c), F32),
        ],
        scratch_shapes=[
            pltpu.VMEM((t + 16, c), F32),
            pltpu.VMEM((t, c), F32),
            pltpu.VMEM((2, t, c), F32),
            pltpu.VMEM((2, t, c), F32),
            pltpu.VMEM((2, t, c), F32),
            pltpu.VMEM((2, c), F32),
        ],
        compiler_params=_cparams(("parallel", "arbitrary")),
        name="rglru",
    )(z, z, z, z, z, z, cw, cb.reshape(1, c), wa, ba, wx, bx, lam, h0)


def _rope(x, cos_t, sin_t):
    lane = lax.broadcasted_iota(jnp.int32, x.shape, 1)
    partner = jnp.where((lane & 32) == 0, pltpu.roll(x, LANES - 32, 1), pltpu.roll(x, 32, 1))
    return x * cos_t + partner * sin_t


def _softmax_pv(s_parts, v_parts, sink):
    m = sink
    for s in s_parts:
        m = jnp.maximum(m, jnp.max(s, axis=-1, keepdims=True))
    den = jnp.exp(sink - m)
    acc = None
    for s, v in zip(s_parts, v_parts):
        p = jnp.exp(s - m)
        den = den + jnp.sum(p, axis=-1, keepdims=True)
        pv = jnp.dot(p.astype(BF16), v, preferred_element_type=F32)
        acc = pv if acc is None else acc + pv
    return acc / den


def _attn_kernel(sink_ref, q_ref, *rest, tq, n_q, n_kv, seq, local):
    if local:
        (km_ref, kp_ref, kn_ref, vm_ref, vp_ref, vn_ref, kc_ref, vc_ref,
         cm_ref, cp_ref, cn_ref, sm_ref, sp_ref, sn_ref, o_ref) = rest
    else:
        kc_ref, vc_ref, o_ref = rest
    i = pl.program_id(1)
    scale = HEAD_DIM ** -0.5
    nt = (((1,), (1,)), ((), ()))

    if local:
        cos_q, sin_q = cm_ref[...], sm_ref[...]
        cos_k = jnp.concatenate([cp_ref[...], cos_q, cn_ref[...]], axis=0)
        sin_k = jnp.concatenate([sp_ref[...], sin_q, sn_ref[...]], axis=0)
        span = tq + 2 * WINDOW
        qi = lax.broadcasted_iota(jnp.int32, (tq, span), 0)
        kj = lax.broadcasted_iota(jnp.int32, (tq, span), 1)
        kpos = i * tq - WINDOW + kj
        valid = (jnp.abs(qi + WINDOW - kj) <= WINDOW) & (kpos >= 0) & (kpos < seq)
        bias = jnp.where(valid, 0.0, NEG_INF)

    for kh in range(n_kv):
        ks = slice(kh * HEAD_DIM, (kh + 1) * HEAD_DIM)
        kc = kc_ref[:, ks]
        vc = vc_ref[:, ks]
        qs = []
        for g in range(GROUP):
            h = kh * GROUP + g
            qh = q_ref[:, h * HEAD_DIM:(h + 1) * HEAD_DIM]
            if local:
                qh = _rope(qh.astype(F32), cos_q, sin_q).astype(BF16)
            qs.append(qh)
        qs = jnp.concatenate(qs, axis=0)
        s_ctx = lax.dot_general(qs, kc, nt, preferred_element_type=F32) * scale
        if local:
            k_span = jnp.concatenate([kp_ref[:, ks], km_ref[:, ks], kn_ref[:, ks]], axis=0)
            k_span = _rope(k_span.astype(F32), cos_k, sin_k).astype(BF16)
            v_span = jnp.concatenate([vp_ref[:, ks], vm_ref[:, ks], vn_ref[:, ks]], axis=0)
            s_loc = lax.dot_general(qs, k_span, nt, preferred_element_type=F32) * scale
        for g in range(GROUP):
            h = kh * GROUP + g
            rows = slice(g * tq, (g + 1) * tq)
            sink = sink_ref[h]
            if local:
                o = _softmax_pv([s_loc[rows] + bias, s_ctx[rows]], [v_span, vc], sink)
            else:
                o = _softmax_pv([s_ctx[rows]], [vc], sink)
            o_ref[:, h * HEAD_DIM:(h + 1) * HEAD_DIM] = o.astype(o_ref.dtype)


def _attn_call(sink, z, zc, cols, cos_t, sin_t, local):
    b, l, _ = z.shape
    n_ctx = zc.shape[1]
    n_heads = sink.shape[0]
    n_kv = n_heads // GROUP
    qw, kw = n_heads * HEAD_DIM, n_kv * HEAD_DIM
    col_q, col_k, col_v = cols
    tq = _largest_tile(l, 256, WINDOW)
    n_q = l // tq
    wb = tq // WINDOW
    n_w = l // WINDOW
    prev = lambda bb, i: jnp.maximum(i * wb - 1, 0)
    nxt = lambda bb, i: jnp.minimum((i + 1) * wb, n_w - 1)
    in_specs = [
        pl.BlockSpec(memory_space=pltpu.SMEM),
        pl.BlockSpec((None, tq, qw), lambda bb, i: (bb, i, col_q)),
    ]
    args = [sink, z]
    if local:
        for col in (col_k, col_v):
            in_specs += [
                pl.BlockSpec((None, tq, kw), lambda bb, i, col=col: (bb, i, col)),
                pl.BlockSpec((None, WINDOW, kw), lambda bb, i, col=col: (bb, prev(bb, i), col)),
                pl.BlockSpec((None, WINDOW, kw), lambda bb, i, col=col: (bb, nxt(bb, i), col)),
            ]
            args += [z, z, z]
    in_specs += [
        pl.BlockSpec((None, n_ctx, kw), lambda bb, i: (bb, 0, col_k)),
        pl.BlockSpec((None, n_ctx, kw), lambda bb, i: (bb, 0, col_v)),
    ]
    args += [zc, zc]
    if local:
        for tab in (cos_t, sin_t):
            in_specs += [
                pl.BlockSpec((tq, HEAD_DIM), lambda bb, i: (i, 0)),
                pl.BlockSpec((WINDOW, HEAD_DIM), lambda bb, i: (prev(bb, i), 0)),
                pl.BlockSpec((WINDOW, HEAD_DIM), lambda bb, i: (nxt(bb, i), 0)),
            ]
            args += [tab, tab, tab]
    return pl.pallas_call(
        functools.partial(_attn_kernel, tq=tq, n_q=n_q, n_kv=n_kv, seq=l, local=local),
        grid=(b, n_q),
        in_specs=in_specs,
        out_specs=pl.BlockSpec((None, tq, qw), lambda bb, i: (bb, i, 0)),
        out_shape=jax.ShapeDtypeStruct((b, l, qw), BF16),
        compiler_params=_cparams(("parallel", "parallel")),
        name="attn_local" if local else "attn_ctx",
    )(*args)


def _merge_kernel(h_ref, gate_ref, hf_ref, hb_ref, rg_ref, sb_ref, cg_ref, cgp_ref, cgn_ref,
                  sx_ref, sxp_ref, sxn_ref, att_ref, g_ref, bm_ref, scw_ref, wb_ref, wo_ref,
                  o_ref, ppad_ref, *, tm, n_t, d):
    i = pl.program_id(1)
    ya = (hf_ref[...].astype(F32) + hb_ref[...].astype(F32)) * _gelu_tanh(rg_ref[...].astype(F32))

    _fill_padded(ppad_ref,
                 cg_ref[...].astype(F32) * sx_ref[...].astype(F32),
                 (cgp_ref[...].astype(F32) * sxp_ref[...].astype(F32))[8:16],
                 (cgn_ref[...].astype(F32) * sxn_ref[...].astype(F32))[0:8],
                 i > 0, i < n_t - 1, tm)
    conv = None
    for k in range(SC_CONV):
        off = 8 - SC_CONV_LEFT + k
        term = scw_ref[k:k + 1, :] * ppad_ref[off:off + tm, :]
        conv = term if conv is None else conv + term
    yb = sb_ref[...].astype(F32) * conv

    ys = (ya.astype(BF16), yb.astype(BF16), att_ref[...])
    merged = None
    for br in range(N_BRANCH):
        gates = jax.nn.sigmoid(g_ref[:, br * d:(br + 1) * d].astype(F32) + bm_ref[br:br + 1, :])
        term = gates * jnp.dot(ys[br], wb_ref[br], preferred_element_type=F32)
        merged = term if merged is None else merged + term
    y = jnp.dot(merged.astype(BF16), wo_ref[...], preferred_element_type=F32)
    o_ref[...] = h_ref[...] + gate_ref[...] * y


def _merge_call(h, gate, hf, hb, z, att, cols, b_merge, sc_w, w_branch, w_out):
    b, l, d = h.shape
    bw = hf.shape[2]
    col_g, col_rg, col_sb, col_cg, col_sx = cols
    tm = _largest_tile(l, 256, HALO)
    n_t = l // tm
    hb_per = tm // HALO
    n_h = l // HALO
    per_batch = gate.shape[0] == b
    tile = lambda col: (lambda bb, i: (bb, i, col))
    prev = lambda col: (lambda bb, i: (bb, jnp.maximum(i * hb_per - 1, 0), col))
    nxt = lambda col: (lambda bb, i: (bb, jnp.minimum((i + 1) * hb_per, n_h - 1), col))
    once = pl.Buffered(1)
    in_specs = [
        pl.BlockSpec((None, tm, d), tile(0)),
        pl.BlockSpec((None, 1, d), lambda bb, i: (bb if per_batch else 0, 0, 0)),
        pl.BlockSpec((None, tm, bw), tile(0)),
        pl.BlockSpec((None, tm, bw), tile(0)),
        pl.BlockSpec((None, tm, bw), tile(col_rg)),
        pl.BlockSpec((None, tm, bw), tile(col_sb)),
        pl.BlockSpec((None, tm, bw), tile(col_cg)),
        pl.BlockSpec((None, HALO, bw), prev(col_cg)),
        pl.BlockSpec((None, HALO, bw), nxt(col_cg)),
        pl.BlockSpec((None, tm, bw), tile(col_sx)),
        pl.BlockSpec((None, HALO, bw), prev(col_sx)),
        pl.BlockSpec((None, HALO, bw), nxt(col_sx)),
        pl.BlockSpec((None, tm, bw), tile(0)),
        pl.BlockSpec((None, tm, N_BRANCH * d), tile(col_g)),
        pl.BlockSpec((N_BRANCH, d), lambda bb, i: (0, 0)),
        pl.BlockSpec((SC_CONV, bw), lambda bb, i: (0, 0)),
        pl.BlockSpec((N_BRANCH, bw, d), lambda bb, i: (0, 0, 0), pipeline_mode=once),
        pl.BlockSpec((d, d), lambda bb, i: (0, 0), pipeline_mode=once),
    ]
    return pl.pallas_call(
        functools.partial(_merge_kernel, tm=tm, n_t=n_t, d=d),
        grid=(b, n_t),
        in_specs=in_specs,
        out_specs=pl.BlockSpec((None, tm, d), tile(0)),
        out_shape=jax.ShapeDtypeStruct((b, l, d), F32),
        scratch_shapes=[pltpu.VMEM((tm + 16, bw), F32)],
        compiler_params=_cparams(("parallel", "parallel")),
        name="merge",
    )(h, gate, hf, hb, z, z, z, z, z, z, z, z, att, z, b_merge, sc_w, w_branch, w_out)


def _prep_ffn(w13, w2):
    f = w2.shape[0]
    tf = min(FFN_CHUNK, f // LANES * LANES)
    fm = f // tf * tf
    tails = None
    if fm < f:
        tails = (w13[:, fm:f].astype(BF16), w13[:, f + fm:].astype(BF16), w2[fm:].astype(BF16))
    return w13[:, :fm].astype(BF16), w13[:, f:f + fm].astype(BF16), w2.astype(BF16), tails


def _prep_w_in(w_in, d, bw, kw):
    o = 0
    pieces = []
    for n in (bw, bw, bw, bw, bw, bw, kw, kw, N_BRANCH * d):
        pieces.append(w_in[:, o:o + n])
        o += n
    return jnp.concatenate([pieces[8]] + pieces[:8], axis=1).astype(BF16)


def _rope_tables(l):
    pos = jnp.arange(l)
    row = (pos // GRID_W).astype(F32)
    col = (pos % GRID_W).astype(F32)
    half = HEAD_DIM // 2
    inv = ROPE_BASE ** (-jnp.arange(0, half, 2, dtype=F32) / half)
    ar, ac = row[:, None] * inv, col[:, None] * inv
    cos_t = jnp.concatenate([jnp.cos(ar), jnp.cos(ar), jnp.cos(ac), jnp.cos(ac)], axis=-1)
    sin_t = jnp.concatenate([-jnp.sin(ar), jnp.sin(ar), -jnp.sin(ac), jnp.sin(ac)], axis=-1)
    return cos_t, sin_t


def kernel(x, c, ctx, c_ctx, ada_w, ada_b, norm_g, ffn1_w13, ffn1_w2, w_in, b_merge, rnn_conv_w,
           rnn_conv_b, lru_w_a, lru_b_a, lru_w_x, lru_b_x, lru_lambda, sc_conv_w, attn_sink, w_branch,
           w_out, ffn2_w13, ffn2_w2, final_norm_g):
    b, l, d = x.shape
    n_ctx = ctx.shape[1]
    depth = ada_w.shape[0]
    bw = w_branch.shape[2]
    n_heads = attn_sink.shape[1]
    kw = (n_heads // GROUP) * HEAD_DIM
    assert bw == n_heads * HEAD_DIM and w_in.shape[2] == 6 * bw + 2 * kw + N_BRANCH * d
    assert (N_BRANCH * d) % bw == 0 and (N_BRANCH * d + 6 * bw) % kw == 0 and b + 1 <= 8
    g_blocks = N_BRANCH * d // bw
    col_rx, col_rg, col_sb, col_cg, col_sx, col_q = (g_blocks + n for n in range(6))
    col_k = (N_BRANCH * d + 6 * bw) // kw
    col_v = col_k + 1

    cvec = jnp.zeros((8, d), F32).at[:b].set(c).at[b].set(c_ctx)
    mod = _ada_call(cvec, ada_w, ada_b).reshape(depth, 8, N_MOD, 1, d)
    cos_t, sin_t = _rope_tables(l)

    h = x.reshape(b * l, d)
    hc = ctx.reshape(b * n_ctx, d)
    for layer in range(depth):
        last = layer == depth - 1
        ml = mod[layer, :b]
        mc = mod[layer, b:b + 1]
        ng = norm_g[layer]
        ffn_a = _prep_ffn(ffn1_w13[layer], ffn1_w2[layer])
        ffn_b = _prep_ffn(ffn2_w13[layer], ffn2_w2[layer])
        w_in_p = _prep_w_in(w_in[layer], d, bw, kw)
        wa, wx = lru_w_a[layer].astype(BF16), lru_w_x[layer].astype(BF16)
        wbr, wo = w_branch[layer].astype(BF16), w_out[layer].astype(BF16)

        h = _ffn_call(h, ng[0], ml[:, 0], ml[:, 1], ml[:, 2], ffn_a)
        hc = _ffn_call(hc, ng[0], mc[:, 0], mc[:, 1], mc[:, 2], ffn_a)

        z = _inproj_call(h, ng[1], ml[:, 3], ml[:, 4], w_in_p).reshape(b, l, -1)
        zc = _inproj_call(hc, ng[1], mc[:, 3], mc[:, 4], w_in_p).reshape(b, n_ctx, -1)
        lru = (rnn_conv_w[layer], rnn_conv_b[layer], wa, lru_b_a[layer], wx, lru_b_x[layer], lru_lambda[layer])
        hcf, hcb, h_last = _rglru_call(zc, col_rx, *lru, jnp.zeros((b, 2, bw), F32))
        hlf, hlb, _ = _rglru_call(z, col_rx, *lru, h_last)
        att = _attn_call(attn_sink[layer], z, zc, (col_q, col_k, col_v), cos_t, sin_t, True)
        merge_cols = (0, col_rg, col_sb, col_cg, col_sx)
        h = _merge_call(h.reshape(b, l, d), ml[:, 5], hlf, hlb, z, att, merge_cols,
                        b_merge[layer], sc_conv_w[layer], wbr, wo).reshape(b * l, d)

        h = _ffn_call(h, ng[2], ml[:, 6], ml[:, 7], ml[:, 8], ffn_b,
                      final_g=final_norm_g if last else None)
        if not last:
            attc = _attn_call(attn_sink[layer], zc, zc, (col_q, col_k, col_v), cos_t, sin_t, False)
            hc = _merge_call(hc.reshape(b, n_ctx, d), mc[:, 5], hcf, hcb, zc, attc, merge_cols,
                             b_merge[layer], sc_conv_w[layer], wbr, wo).reshape(b * n_ctx, d)
            hc = _ffn_call(hc, ng[2], mc[:, 6], mc[:, 7], mc[:, 8], ffn_b)
    return h.reshape(b, l, d)
```

```python
import functools

import jax
import jax.numpy as jnp
from jax import lax
from jax.experimental import pallas as pl
from jax.experimental.pallas import tpu as pltpu

F32 = jnp.float32
BF16 = jnp.bfloat16

HEAD_DIM = 128
GROUP = 4
WINDOW = 128
GRID_W = 64
ROPE_BASE = 10000.0
RNN_BLOCK = 128
RNN_CONV = 4
RNN_CONV_LEFT = 2
SC_CONV = 3
SC_CONV_LEFT = 1
LRU_C = 8.0
N_BRANCH = 3
N_MOD = 9
EPS = 1e-6
NEG_INF = -1e30
LOG2E = 1.4426950408889634

LANES = 128
SUBLANES_F32 = 8
SUBLANES_BF16 = 16
HALO = SUBLANES_BF16
VMEM_LIMIT_BYTES = 56 * 1024 * 1024
FFN_CHUNK = 1024


def _cparams(semantics):
    return pltpu.CompilerParams(dimension_semantics=semantics, vmem_limit_bytes=VMEM_LIMIT_BYTES)


def _largest_tile(n, cap, quantum):
    best = None
    t = quantum
    while t <= min(n, cap):
        if n % t == 0:
            best = t
        t += quantum
    assert best is not None, (n, cap, quantum)
    return best


def _rmsnorm(x, g):
    return x * lax.rsqrt(jnp.mean(x * x, axis=-1, keepdims=True) + EPS) * g


def _norm_mod(x, g, shift, scale):
    return _rmsnorm(x, g) * (1.0 + scale) + shift


def _silu(x):
    return x * jax.nn.sigmoid(x)


def _gelu_tanh(x):
    return 0.5 * x * (1.0 + jnp.tanh(0.7978845608028654 * (x + 0.044715 * (x * x * x))))


def _softplus(x):
    return jnp.maximum(x, 0.0) + jnp.log1p(jnp.exp(-jnp.abs(x)))


def _one_minus_exp2(x, ex):
    kahan = (ex - 1.0) * x / jnp.log(ex)
    em1 = jnp.where(x < -1.0, ex - 1.0, jnp.where(ex == 1.0, x, kahan))
    return -em1 * (ex + 1.0)


def _ada_kernel(c_ref, w_ref, b_ref, o_ref):
    s = _silu(c_ref[...]).astype(BF16)
    o_ref[...] = jnp.dot(s, w_ref[...].astype(BF16), preferred_element_type=F32) + b_ref[...]


def _ada_call(cvec, ada_w, ada_b):
    depth, d, nm = ada_w.shape
    tn = _largest_tile(nm, 1024, LANES)
    return pl.pallas_call(
        _ada_kernel,
        grid=(depth, nm // tn),
        in_specs=[
            pl.BlockSpec((8, d), lambda l, j: (0, 0)),
            pl.BlockSpec((None, d, tn), lambda l, j: (l, 0, j)),
            pl.BlockSpec((None, 1, tn), lambda l, j: (l, 0, j)),
        ],
        out_specs=pl.BlockSpec((None, 8, tn), lambda l, j: (l, 0, j)),
        out_shape=jax.ShapeDtypeStruct((depth, 8, nm), F32),
        compiler_params=_cparams(("arbitrary", "arbitrary")),
        name="adaln",
    )(cvec, ada_w, ada_b.reshape(depth, 1, nm))


def _swiglu_chunk(u, wg_ref, wu_ref, w2_ref):
    gate = jnp.dot(u, wg_ref[...], preferred_element_type=F32)
    up = jnp.dot(u, wu_ref[...], preferred_element_type=F32)
    act = (_silu(gate) * up).astype(BF16)
    return jnp.dot(act, w2_ref[...], preferred_element_type=F32)


def _ffn_kernel(h_ref, g_ref, shift_ref, scale_ref, gate_ref, wg_ref, wu_ref, w2_ref, *rest, tail, final_norm):
    rest = list(rest)
    wgt_ref, wut_ref, w2t_ref = (rest.pop(0), rest.pop(0), rest.pop(0)) if tail else (None, None, None)
    fg_ref = rest.pop(0) if final_norm else None
    o_ref, u_ref = rest
    j = pl.program_id(1)

    @pl.when(j == 0)
    def _():
        u = _norm_mod(h_ref[...], g_ref[...], shift_ref[...], scale_ref[...]).astype(BF16)
        u_ref[...] = u
        o_ref[...] = _swiglu_chunk(u, wgt_ref, wut_ref, w2t_ref) if tail else jnp.zeros_like(o_ref)

    last = pl.num_programs(1) - 1

    @pl.when(j < last)
    def _():
        o_ref[...] += _swiglu_chunk(u_ref[...], wg_ref, wu_ref, w2_ref)

    @pl.when(j == last)
    def _():
        acc = o_ref[...] + _swiglu_chunk(u_ref[...], wg_ref, wu_ref, w2_ref)
        hn = h_ref[...] + (0.5 * gate_ref[...]) * acc
        if final_norm:
            hn = _rmsnorm(hn, fg_ref[...])
        o_ref[...] = hn


def _ffn_call(h, g, shift, scale, gate, weights, final_g=None):
    wg, wu, w2, tails = weights
    m, d = h.shape
    nb = shift.shape[0]
    tf = min(FFN_CHUNK, wg.shape[1])
    n_chunks = wg.shape[1] // tf
    tm = _largest_tile(m // nb, 512, SUBLANES_F32)
    tpb = (m // nb) // tm
    row = lambda i, j: (i // tpb, 0, 0)
    const = lambda i, j: (0, 0)
    once = pl.Buffered(1)
    in_specs = [
        pl.BlockSpec((tm, d), lambda i, j: (i, 0)),
        pl.BlockSpec((1, d), const),
        pl.BlockSpec((None, 1, d), row),
        pl.BlockSpec((None, 1, d), row),
        pl.BlockSpec((None, 1, d), row),
        pl.BlockSpec((d, tf), lambda i, j: (0, j)),
        pl.BlockSpec((d, tf), lambda i, j: (0, j)),
        pl.BlockSpec((tf, d), lambda i, j: (j, 0)),
    ]
    args = [h, g.reshape(1, d), shift, scale, gate, wg, wu, w2]
    if tails is not None:
        for w in tails:
            in_specs.append(pl.BlockSpec(w.shape, const, pipeline_mode=once))
            args.append(w)
    if final_g is not None:
        in_specs.append(pl.BlockSpec((1, d), const))
        args.append(final_g.reshape(1, d))
    return pl.pallas_call(
        functools.partial(_ffn_kernel, tail=tails is not None, final_norm=final_g is not None),
        grid=(m // tm, n_chunks),
        in_specs=in_specs,
        out_specs=pl.BlockSpec((tm, d), lambda i, j: (i, 0)),
        out_shape=jax.ShapeDtypeStruct((m, d), F32),
        scratch_shapes=[pltpu.VMEM((tm, d), BF16)],
        compiler_params=_cparams(("parallel", "arbitrary")),
        name="ffn",
    )(*args)


def _inproj_kernel(h_ref, g_ref, shift_ref, scale_ref, w_ref, o_ref, ua_ref, ub_ref, *, n_slices):
    i, j = pl.program_id(0), pl.program_id(1)
    tm = h_ref.shape[0]
    norm = lambda x: _norm_mod(x, g_ref[...], shift_ref[...], scale_ref[...]).astype(BF16)

    def step(cur_ref, nxt_ref):
        @pl.when((j == 0) & ((i == 0) | (n_slices == 0)))
        def _():
            cur_ref[...] = norm(h_ref[...])

        if n_slices:
            rows = tm // n_slices
            r0 = pl.multiple_of(jnp.clip(j - 1, 0, n_slices - 1) * rows, rows)
            nxt_ref[pl.ds(r0, rows), :] = norm(h_ref[pl.ds(r0, rows), :])
        o_ref[...] = jnp.dot(cur_ref[...], w_ref[...], preferred_element_type=F32).astype(o_ref.dtype)

    pl.when(i % 2 == 0)(lambda: step(ua_ref, ub_ref))
    pl.when(i % 2 == 1)(lambda: step(ub_ref, ua_ref))


def _inproj_call(h, g, shift, scale, w):
    m, d = h.shape
    nb = shift.shape[0]
    nc = w.shape[1]
    tm = _largest_tile(m // nb, 1024, SUBLANES_BF16)
    tpb = (m // nb) // tm
    tn = _largest_tile(nc, 1280, LANES)
    n_i, n_j = m // tm, nc // tn
    n_slices = 0
    while 2 * max(n_slices, 1) <= n_j - 1 and tm % (2 * max(n_slices, 1) * SUBLANES_BF16) == 0:
        n_slices = 2 * max(n_slices, 1)
    if n_slices:
        tile = lambda i, j: jnp.minimum(i + jnp.where(j > 0, 1, 0), n_i - 1)
    else:
        tile = lambda i, j: i
    row = lambda i, j: (tile(i, j) // tpb, 0, 0)
    return pl.pallas_call(
        functools.partial(_inproj_kernel, n_slices=n_slices),
        grid=(n_i, n_j),
        in_specs=[
            pl.BlockSpec((tm, d), lambda i, j: (tile(i, j), 0)),
            pl.BlockSpec((1, d), lambda i, j: (0, 0)),
            pl.BlockSpec((None, 1, d), row),
            pl.BlockSpec((None, 1, d), row),
            pl.BlockSpec((d, tn), lambda i, j: (0, j)),
        ],
        out_specs=pl.BlockSpec((tm, tn), lambda i, j: (i, j)),
        out_shape=jax.ShapeDtypeStruct((m, nc), BF16),
        scratch_shapes=[pltpu.VMEM((tm, d), BF16), pltpu.VMEM((tm, d), BF16)],
        compiler_params=_cparams(("arbitrary", "arbitrary")),
        name="in_proj",
    )(h, g.reshape(1, d), shift, scale, w)


def _fill_padded(pad_ref, main, prev_tail, next_head, has_prev, has_next, t):
    pad_ref[0:8, :] = jnp.where(has_prev, prev_tail, 0.0)
    pad_ref[8:8 + t, :] = main
    pad_ref[8 + t:16 + t, :] = jnp.where(has_next, next_head, 0.0)


def _rglru_kernel(zf_ref, zfp_ref, zfn_ref, zb_ref, zbp_ref, zbn_ref, cw_ref, cb_ref, wa_ref, ba_ref,
                  wx_ref, bx_ref, lam_ref, h0_ref, of_ref, ob_ref, last_ref,
                  xpad_ref, xa_ref, a_ref, u_ref, hs_ref, car_ref, *, t, n_t, n_blk):
    i = pl.program_id(1)

    @pl.when(i == 0)
    def _():
        car_ref[...] = h0_ref[...]

    sp = _softplus(-lam_ref[...])

    def gates(d, z_ref, zp_ref, zn_ref, tile):
        _fill_padded(xpad_ref, z_ref[...].astype(F32), zp_ref[...].astype(F32)[8:16],
                     zn_ref[...].astype(F32)[0:8], tile > 0, tile < n_t - 1, t)
        xa = cb_ref[...]
        for k in range(RNN_CONV):
            off = 8 - RNN_CONV_LEFT + k
            xa = xa + cw_ref[k:k + 1, :] * xpad_ref[off:off + t, :]
        xa_ref[...] = xa
        for n in range(n_blk):
            sl = slice(n * RNN_BLOCK, (n + 1) * RNN_BLOCK)
            xs = xa_ref[:, sl]
            xb = xs.astype(BF16)
            rg = jax.nn.sigmoid(jnp.dot(xb, wa_ref[d, n], preferred_element_type=F32) + ba_ref[d:d + 1, sl])
            ig = jax.nn.sigmoid(jnp.dot(xb, wx_ref[d, n], preferred_element_type=F32) + bx_ref[d:d + 1, sl])
            log_a = (-LRU_C * rg) * sp[d:d + 1, sl]
            a = jnp.exp(log_a)
            a_ref[d, :, sl] = a
            u_ref[d, :, sl] = jnp.sqrt(_one_minus_exp2(log_a, a)) * (ig * xs)

    gates(0, zf_ref, zfp_ref, zfn_ref, i)
    gates(1, zb_ref, zbp_ref, zbn_ref, n_t - 1 - i)

    def step(s, carry):
        hf, hb = carry
        sb = t - 1 - s
        hf = a_ref[0, pl.ds(s, 1), :] * hf + u_ref[0, pl.ds(s, 1), :]
        hb = a_ref[1, pl.ds(sb, 1), :] * hb + u_ref[1, pl.ds(sb, 1), :]
        hs_ref[0, pl.ds(s, 1), :] = hf
        hs_ref[1, pl.ds(sb, 1), :] = hb
        return hf, hb

    hf, hb = lax.fori_loop(0, t, step, (car_ref[0:1, :], car_ref[1:2, :]), unroll=8)
    car_ref[0:1, :] = hf
    car_ref[1:2, :] = hb
    of_ref[...] = hs_ref[0].astype(of_ref.dtype)
    ob_ref[...] = hs_ref[1].astype(ob_ref.dtype)
    last_ref[0:1, :] = hf
    last_ref[1:2, :] = hb


def _rglru_call(z, col_rx, cw, cb, wa, ba, wx, bx, lam, h0):
    b, l, _ = z.shape
    c = cw.shape[1]
    n_blk = c // RNN_BLOCK
    t = _largest_tile(l, 512, HALO)
    n_t = l // t
    hb_per = t // HALO
    n_h = l // HALO

    def main(rev):
        return lambda bb, i: (bb, (n_t - 1 - i) if rev else i, col_rx)

    def prev(rev):
        return lambda bb, i: (bb, jnp.maximum(((n_t - 1 - i) if rev else i) * hb_per - 1, 0), col_rx)

    def nxt(rev):
        return lambda bb, i: (bb, jnp.minimum((((n_t - 1 - i) if rev else i) + 1) * hb_per, n_h - 1), col_rx)

    const2 = lambda bb, i: (0, 0)
    const4 = lambda bb, i: (0, 0, 0, 0)
    in_specs = []
    for rev in (False, True):
        in_specs += [pl.BlockSpec((None, t, c), main(rev)),
                     pl.BlockSpec((None, HALO, c), prev(rev)),
                     pl.BlockSpec((None, HALO, c), nxt(rev))]
    in_specs += [
        pl.BlockSpec((RNN_CONV, c), const2),
        pl.BlockSpec((1, c), const2),
        pl.BlockSpec((2, n_blk, RNN_BLOCK, RNN_BLOCK), const4),
        pl.BlockSpec((2, c), const2),
        pl.BlockSpec((2, n_blk, RNN_BLOCK, RNN_BLOCK), const4),
        pl.BlockSpec((2, c), const2),
        pl.BlockSpec((2, c), const2),
        pl.BlockSpec((None, 2, c), lambda bb, i: (bb, 0, 0)),
    ]
    return pl.pallas_call(
        functools.partial(_rglru_kernel, t=t, n_t=n_t, n_blk=n_blk),
        grid=(b, n_t),
        in_specs=in_specs,
        out_specs=[
            pl.BlockSpec((None, t, c), lambda bb, i: (bb, i, 0)),
            pl.BlockSpec((None, t, c), lambda bb, i: (bb, n_t - 1 - i, 0)),
            pl.BlockSpec((None, 2, c), lambda bb, i: (bb, 0, 0)),
        ],
        out_shape=[
            jax.ShapeDtypeStruct((b, l, c), BF16),
            jax.ShapeDtypeStruct((b, l, c), BF16),
            jax.ShapeDtypeStruct((b, 2, c), F32),
        ],
        scratch_shapes=[
            pltpu.VMEM((t + 16, c), F32),
            pltpu.VMEM((t, c), F32),
            pltpu.VMEM((2, t, c), F32),
            pltpu.VMEM((2, t, c), F32),
            pltpu.VMEM((2, t, c), F32),
            pltpu.VMEM((2, c), F32),
        ],
        compiler_params=_cparams(("parallel", "arbitrary")),
        name="rglru",
    )(z, z, z, z, z, z, cw, cb.reshape(1, c), wa, ba, wx, bx, lam, h0)


def _rope(x, cos_t, sin_t):
    lane = lax.broadcasted_iota(jnp.int32, x.shape, 1)
    partner = jnp.where((lane & 32) == 0, pltpu.roll(x, LANES - 32, 1), pltpu.roll(x, 32, 1))
    return x * cos_t + partner * sin_t


def _attn_kernel(sink_ref, q_ref, *rest, tq, n_kv, seq, local):
    if local:
        (km_ref, kp_ref, kn_ref, vm_ref, vp_ref, vn_ref, kc_ref, vc_ref,
         cm_ref, cp_ref, cn_ref, sm_ref, sp_ref, sn_ref, o_ref) = rest
    else:
        kc_ref, vc_ref, o_ref = rest
    i = pl.program_id(1)
    q_scale = HEAD_DIM ** -0.5 * LOG2E
    nt = (((1,), (1,)), ((), ()))
    sub = WINDOW
    n_ctx = kc_ref.shape[0]

    if local:
        cos_q, sin_q = cm_ref[...], sm_ref[...]
        cos_k = jnp.concatenate([cp_ref[...], cos_q, cn_ref[...]], axis=0)
        sin_k = jnp.concatenate([sp_ref[...], sin_q, sn_ref[...]], axis=0)
        r = lax.broadcasted_iota(jnp.int32, (sub, 3 * sub + n_ctx), 0)
        c = lax.broadcasted_iota(jnp.int32, (sub, 3 * sub + n_ctx), 1)
        band = (c >= r) & (c <= r + 2 * WINDOW)
        is_ctx = c >= 3 * sub

    for kh in range(n_kv):
        ks = slice(kh * HEAD_DIM, (kh + 1) * HEAD_DIM)
        kc = kc_ref[:, ks]
        vc = vc_ref[:, ks]
        heads = [kh * GROUP + g for g in range(GROUP)]
        qh = []
        for h in heads:
            x = q_ref[:, h * HEAD_DIM:(h + 1) * HEAD_DIM].astype(F32)
            if local:
                x = _rope(x, cos_q, sin_q)
            qh.append((x * q_scale).astype(BF16))
        if local:
            k_span = jnp.concatenate([kp_ref[:, ks], km_ref[:, ks], kn_ref[:, ks]], axis=0)
            k_span = _rope(k_span.astype(F32), cos_k, sin_k).astype(BF16)
            v_span = jnp.concatenate([vp_ref[:, ks], vm_ref[:, ks], vn_ref[:, ks]], axis=0)
        sink2 = jnp.concatenate([jnp.full((sub, 1), sink_ref[h] * LOG2E, F32) for h in heads], axis=0)

        for sb in range(tq // sub):
            rows = slice(sb * sub, (sb + 1) * sub)
            qs = jnp.concatenate([x[rows] for x in qh], axis=0)
            if local:
                keys = jnp.concatenate([k_span[sb * sub:(sb + 3) * sub], kc], axis=0)
                vals = jnp.concatenate([v_span[sb * sub:(sb + 3) * sub], vc], axis=0)
                kpos = i * tq + (sb - 1) * sub + c
                valid = is_ctx | (band & (kpos >= 0) & (kpos < seq))
                bias = jnp.where(valid, 0.0, NEG_INF)
            else:
                keys, vals = kc, vc
            s = lax.dot_general(qs, keys, nt, preferred_element_type=F32)
            if local:
                s = s + jnp.concatenate([bias] * GROUP, axis=0)
            m = jnp.maximum(jnp.max(s, axis=-1, keepdims=True), sink2)
            p = jnp.exp2(s - m)
            den = jnp.sum(p, axis=-1, keepdims=True) + jnp.exp2(sink2 - m)
            o = jnp.dot(p.astype(BF16), vals, preferred_element_type=F32) / den
            for g, h in enumerate(heads):
                o_ref[rows, h * HEAD_DIM:(h + 1) * HEAD_DIM] = o[g * sub:(g + 1) * sub].astype(o_ref.dtype)


def _attn_call(sink, z, zc, cols, cos_t, sin_t, local):
    b, l, _ = z.shape
    n_ctx = zc.shape[1]
    n_heads = sink.shape[0]
    n_kv = n_heads // GROUP
    qw, kw = n_heads * HEAD_DIM, n_kv * HEAD_DIM
    col_q, col_k, col_v = cols
    tq = _largest_tile(l, 256, WINDOW)
    n_q = l // tq
    wb = tq // WINDOW
    n_w = l // WINDOW
    prev = lambda bb, i: jnp.maximum(i * wb - 1, 0)
    nxt = lambda bb, i: jnp.minimum((i + 1) * wb, n_w - 1)
    in_specs = [
        pl.BlockSpec(memory_space=pltpu.SMEM),
        pl.BlockSpec((None, tq, qw), lambda bb, i: (bb, i, col_q)),
    ]
    args = [sink, z]
    if local:
        for col in (col_k, col_v):
            in_specs += [
                pl.BlockSpec((None, tq, kw), lambda bb, i, col=col: (bb, i, col)),
                pl.BlockSpec((None, WINDOW, kw), lambda bb, i, col=col: (bb, prev(bb, i), col)),
                pl.BlockSpec((None, WINDOW, kw), lambda bb, i, col=col: (bb, nxt(bb, i), col)),
            ]
            args += [z, z, z]
    in_specs += [
        pl.BlockSpec((None, n_ctx, kw), lambda bb, i: (bb, 0, col_k)),
        pl.BlockSpec((None, n_ctx, kw), lambda bb, i: (bb, 0, col_v)),
    ]
    args += [zc, zc]
    if local:
        for tab in (cos_t, sin_t):
            in_specs += [
                pl.BlockSpec((tq, HEAD_DIM), lambda bb, i: (i, 0)),
                pl.BlockSpec((WINDOW, HEAD_DIM), lambda bb, i: (prev(bb, i), 0)),
                pl.BlockSpec((WINDOW, HEAD_DIM), lambda bb, i: (nxt(bb, i), 0)),
            ]
            args += [tab, tab, tab]
    return pl.pallas_call(
        functools.partial(_attn_kernel, tq=tq, n_kv=n_kv, seq=l, local=local),
        grid=(b, n_q),
        in_specs=in_specs,
        out_specs=pl.BlockSpec((None, tq, qw), lambda bb, i: (bb, i, 0)),
        out_shape=jax.ShapeDtypeStruct((b, l, qw), BF16),
        compiler_params=_cparams(("parallel", "parallel")),
        name="attn_local" if local else "attn_ctx",
    )(*args)


def _merge_kernel(h_ref, gate_ref, hf_ref, hb_ref, rg_ref, sb_ref, cg_ref, cgp_ref, cgn_ref,
                  sx_ref, sxp_ref, sxn_ref, att_ref, g_ref, bm_ref, scw_ref, wb_ref, wo_ref,
                  o_ref, ppad_ref, *, tm, n_t, d):
    i = pl.program_id(1)
    ya = (hf_ref[...].astype(F32) + hb_ref[...].astype(F32)) * _gelu_tanh(rg_ref[...].astype(F32))

    _fill_padded(ppad_ref,
                 cg_ref[...].astype(F32) * sx_ref[...].astype(F32),
                 (cgp_ref[...].astype(F32) * sxp_ref[...].astype(F32))[8:16],
                 (cgn_ref[...].astype(F32) * sxn_ref[...].astype(F32))[0:8],
                 i > 0, i < n_t - 1, tm)
    conv = None
    for k in range(SC_CONV):
        off = 8 - SC_CONV_LEFT + k
        term = scw_ref[k:k + 1, :] * ppad_ref[off:off + tm, :]
        conv = term if conv is None else conv + term
    yb = sb_ref[...].astype(F32) * conv

    ys = (ya.astype(BF16), yb.astype(BF16), att_ref[...])
    merged = None
    for br in range(N_BRANCH):
        gates = jax.nn.sigmoid(g_ref[:, br * d:(br + 1) * d].astype(F32) + bm_ref[br:br + 1, :])
        term = gates * jnp.dot(ys[br], wb_ref[br], preferred_element_type=F32)
        merged = term if merged is None else merged + term
    y = jnp.dot(merged.astype(BF16), wo_ref[...], preferred_element_type=F32)
    o_ref[...] = h_ref[...] + gate_ref[...] * y


def _merge_call(h, gate, hf, hb, z, att, cols, b_merge, sc_w, w_branch, w_out):
    b, l, d = h.shape
    bw = hf.shape[2]
    col_g, col_rg, col_sb, col_cg, col_sx = cols
    tm = _largest_tile(l, 256, HALO)
    n_t = l // tm
    hb_per = tm // HALO
    n_h = l // HALO
    per_batch = gate.shape[0] == b
    tile = lambda col: (lambda bb, i: (bb, i, col))
    prev = lambda col: (lambda bb, i: (bb, jnp.maximum(i * hb_per - 1, 0), col))
    nxt = lambda col: (lambda bb, i: (bb, jnp.minimum((i + 1) * hb_per, n_h - 1), col))
    once = pl.Buffered(1)
    in_specs = [
        pl.BlockSpec((None, tm, d), tile(0)),
        pl.BlockSpec((None, 1, d), lambda bb, i: (bb if per_batch else 0, 0, 0)),
        pl.BlockSpec((None, tm, bw), tile(0)),
        pl.BlockSpec((None, tm, bw), tile(0)),
        pl.BlockSpec((None, tm, bw), tile(col_rg)),
        pl.BlockSpec((None, tm, bw), tile(col_sb)),
        pl.BlockSpec((None, tm, bw), tile(col_cg)),
        pl.BlockSpec((None, HALO, bw), prev(col_cg)),
        pl.BlockSpec((None, HALO, bw), nxt(col_cg)),
        pl.BlockSpec((None, tm, bw), tile(col_sx)),
        pl.BlockSpec((None, HALO, bw), prev(col_sx)),
        pl.BlockSpec((None, HALO, bw), nxt(col_sx)),
        pl.BlockSpec((None, tm, bw), tile(0)),
        pl.BlockSpec((None, tm, N_BRANCH * d), tile(col_g)),
        pl.BlockSpec((N_BRANCH, d), lambda bb, i: (0, 0)),
        pl.BlockSpec((SC_CONV, bw), lambda bb, i: (0, 0)),
        pl.BlockSpec((N_BRANCH, bw, d), lambda bb, i: (0, 0, 0), pipeline_mode=once),
        pl.BlockSpec((d, d), lambda bb, i: (0, 0), pipeline_mode=once),
    ]
    return pl.pallas_call(
        functools.partial(_merge_kernel, tm=tm, n_t=n_t, d=d),
        grid=(b, n_t),
        in_specs=in_specs,
        out_specs=pl.BlockSpec((None, tm, d), tile(0)),
        out_shape=jax.ShapeDtypeStruct((b, l, d), F32),
        scratch_shapes=[pltpu.VMEM((tm + 16, bw), F32)],
        compiler_params=_cparams(("parallel", "parallel")),
        name="merge",
    )(h, gate, hf, hb, z, z, z, z, z, z, z, z, att, z, b_merge, sc_w, w_branch, w_out)


def _prep_ffn(w13, w2):
    f = w2.shape[0]
    tf = min(FFN_CHUNK, f // LANES * LANES)
    fm = f // tf * tf
    tails = None
    if fm < f:
        tails = (w13[:, fm:f].astype(BF16), w13[:, f + fm:].astype(BF16), w2[fm:].astype(BF16))
    return w13[:, :fm].astype(BF16), w13[:, f:f + fm].astype(BF16), w2.astype(BF16), tails


def _prep_w_in(w_in, d, bw, kw):
    o = 0
    pieces = []
    for n in (bw, bw, bw, bw, bw, bw, kw, kw, N_BRANCH * d):
        pieces.append(w_in[:, o:o + n])
        o += n
    return jnp.concatenate([pieces[8]] + pieces[:8], axis=1).astype(BF16)


def _rope_tables(l):
    pos = jnp.arange(l)
    row = (pos // GRID_W).astype(F32)
    col = (pos % GRID_W).astype(F32)
    half = HEAD_DIM // 2
    inv = ROPE_BASE ** (-jnp.arange(0, half, 2, dtype=F32) / half)
    ar, ac = row[:, None] * inv, col[:, None] * inv
    cos_t = jnp.concatenate([jnp.cos(ar), jnp.cos(ar), jnp.cos(ac), jnp.cos(ac)], axis=-1)
    sin_t = jnp.concatenate([-jnp.sin(ar), jnp.sin(ar), -jnp.sin(ac), jnp.sin(ac)], axis=-1)
    return cos_t, sin_t


def kernel(x, c, ctx, c_ctx, ada_w, ada_b, norm_g, ffn1_w13, ffn1_w2, w_in, b_merge, rnn_conv_w,
           rnn_conv_b, lru_w_a, lru_b_a, lru_w_x, lru_b_x, lru_lambda, sc_conv_w, attn_sink, w_branch,
           w_out, ffn2_w13, ffn2_w2, final_norm_g):
    b, l, d = x.shape
    n_ctx = ctx.shape[1]
    depth = ada_w.shape[0]
    bw = w_branch.shape[2]
    n_heads = attn_sink.shape[1]
    kw = (n_heads // GROUP) * HEAD_DIM
    assert bw == n_heads * HEAD_DIM and w_in.shape[2] == 6 * bw + 2 * kw + N_BRANCH * d
    assert (N_BRANCH * d) % bw == 0 and (N_BRANCH * d + 6 * bw) % kw == 0 and b + 1 <= 8
    g_blocks = N_BRANCH * d // bw
    col_rx, col_rg, col_sb, col_cg, col_sx, col_q = (g_blocks + n for n in range(6))
    col_k = (N_BRANCH * d + 6 * bw) // kw
    col_v = col_k + 1

    cvec = jnp.zeros((8, d), F32).at[:b].set(c).at[b].set(c_ctx)
    mod = _ada_call(cvec, ada_w, ada_b).reshape(depth, 8, N_MOD, 1, d)
    cos_t, sin_t = _rope_tables(l)

    h = x.reshape(b * l, d)
    hc = ctx.reshape(b * n_ctx, d)
    for layer in range(depth):
        last = layer == depth - 1
        ml = mod[layer, :b]
        mc = mod[layer, b:b + 1]
        ng = norm_g[layer]
        ffn_a = _prep_ffn(ffn1_w13[layer], ffn1_w2[layer])
        ffn_b = _prep_ffn(ffn2_w13[layer], ffn2_w2[layer])
        w_in_p = _prep_w_in(w_in[layer], d, bw, kw)
        wa, wx = lru_w_a[layer].astype(BF16), lru_w_x[layer].astype(BF16)
        wbr, wo = w_branch[layer].astype(BF16), w_out[layer].astype(BF16)

        h = _ffn_call(h, ng[0], ml[:, 0], ml[:, 1], ml[:, 2], ffn_a)
        hc = _ffn_call(hc, ng[0], mc[:, 0], mc[:, 1], mc[:, 2], ffn_a)

        z = _inproj_call(h, ng[1], ml[:, 3], ml[:, 4], w_in_p).reshape(b, l, -1)
        zc = _inproj_call(hc, ng[1], mc[:, 3], mc[:, 4], w_in_p).reshape(b, n_ctx, -1)
        lru = (rnn_conv_w[layer], rnn_conv_b[layer], wa, lru_b_a[layer], wx, lru_b_x[layer], lru_lambda[layer])
        hcf, hcb, h_last = _rglru_call(zc, col_rx, *lru, jnp.zeros((b, 2, bw), F32))
        hlf, hlb, _ = _rglru_call(z, col_rx, *lru, h_last)
        att = _attn_call(attn_sink[layer], z, zc, (col_q, col_k, col_v), cos_t, sin_t, True)
        merge_cols = (0, col_rg, col_sb, col_cg, col_sx)
        h = _merge_call(h.reshape(b, l, d), ml[:, 5], hlf, hlb, z, att, merge_cols,
                        b_merge[layer], sc_conv_w[layer], wbr, wo).reshape(b * l, d)

        h = _ffn_call(h, ng[2], ml[:, 6], ml[:, 7], ml[:, 8], ffn_b,
                      final_g=final_norm_g if last else None)
        if not last:
            attc = _attn_call(attn_sink[layer], zc, zc, (col_q, col_k, col_v), cos_t, sin_t, False)
            hc = _merge_call(hc.reshape(b, n_ctx, d), mc[:, 5], hcf, hcb, zc, attc, merge_cols,
                             b_merge[layer], sc_conv_w[layer], wbr, wo).reshape(b * n_ctx, d)
            hc = _ffn_call(hc, ng[2], mc[:, 6], mc[:, 7], mc[:, 8], ffn_b)
    return h.reshape(b, l, d)
```

```python
import functools

import jax
import jax.numpy as jnp
from jax import lax
from jax.experimental import pallas as pl
from jax.experimental.pallas import tpu as pltpu

F32 = jnp.float32
BF16 = jnp.bfloat16

HEAD_DIM = 128
GROUP = 4
WINDOW = 128
GRID_W = 64
ROPE_BASE = 10000.0
RNN_BLOCK = 128
RNN_CONV = 4
RNN_CONV_LEFT = 2
SC_CONV = 3
SC_CONV_LEFT = 1
LRU_C = 8.0
N_BRANCH = 3
N_MOD = 9
EPS = 1e-6
NEG_INF = -1e30
LOG2E = 1.4426950408889634

LANES = 128
SUBLANES_F32 = 8
SUBLANES_BF16 = 16
HALO = SUBLANES_BF16
VMEM_LIMIT_BYTES = 56 * 1024 * 1024
FFN_CHUNK = 1024


def _cparams(semantics):
    return pltpu.CompilerParams(dimension_semantics=semantics, vmem_limit_bytes=VMEM_LIMIT_BYTES)


def _largest_tile(n, cap, quantum):
    best = None
    t = quantum
    while t <= min(n, cap):
        if n % t == 0:
            best = t
        t += quantum
    assert best is not None, (n, cap, quantum)
    return best


def _rmsnorm(x, g):
    return x * lax.rsqrt(jnp.mean(x * x, axis=-1, keepdims=True) + EPS) * g


def _norm_mod(x, g, shift, scale):
    return _rmsnorm(x, g) * (1.0 + scale) + shift


def _silu(x):
    return x * jax.nn.sigmoid(x)


def _gelu_tanh(x):
    return 0.5 * x * (1.0 + jnp.tanh(0.7978845608028654 * (x + 0.044715 * (x * x * x))))


def _softplus(x):
    return jnp.maximum(x, 0.0) + jnp.log1p(jnp.exp(-jnp.abs(x)))


def _one_minus_exp2(x, ex):
    kahan = (ex - 1.0) * x / jnp.log(ex)
    em1 = jnp.where(x < -1.0, ex - 1.0, jnp.where(ex == 1.0, x, kahan))
    return -em1 * (ex + 1.0)


def _ada_kernel(c_ref, w_ref, b_ref, o_ref):
    s = _silu(c_ref[...]).astype(BF16)
    o_ref[...] = jnp.dot(s, w_ref[...].astype(BF16), preferred_element_type=F32) + b_ref[...]


def _ada_call(cvec, ada_w, ada_b):
    depth, d, nm = ada_w.shape
    tn = _largest_tile(nm, 1024, LANES)
    return pl.pallas_call(
        _ada_kernel,
        grid=(depth, nm // tn),
        in_specs=[
            pl.BlockSpec((8, d), lambda l, j: (0, 0)),
            pl.BlockSpec((None, d, tn), lambda l, j: (l, 0, j)),
            pl.BlockSpec((None, 1, tn), lambda l, j: (l, 0, j)),
        ],
        out_specs=pl.BlockSpec((None, 8, tn), lambda l, j: (l, 0, j)),
        out_shape=jax.ShapeDtypeStruct((depth, 8, nm), F32),
        compiler_params=_cparams(("arbitrary", "arbitrary")),
        name="adaln",
    )(cvec, ada_w, ada_b.reshape(depth, 1, nm))


def _swiglu_chunk(u, wg_ref, wu_ref, w2_ref):
    gate = jnp.dot(u, wg_ref[...], preferred_element_type=F32)
    up = jnp.dot(u, wu_ref[...], preferred_element_type=F32)
    act = (_silu(gate) * up).astype(BF16)
    return jnp.dot(act, w2_ref[...], preferred_element_type=F32)


def _ffn_kernel(h_ref, g_ref, shift_ref, scale_ref, gate_ref, wg_ref, wu_ref, w2_ref, *rest, tail, final_norm):
    rest = list(rest)
    wgt_ref, wut_ref, w2t_ref = (rest.pop(0), rest.pop(0), rest.pop(0)) if tail else (None, None, None)
    fg_ref = rest.pop(0) if final_norm else None
    o_ref, u_ref = rest
    j = pl.program_id(1)

    @pl.when(j == 0)
    def _():
        u = _norm_mod(h_ref[...], g_ref[...], shift_ref[...], scale_ref[...]).astype(BF16)
        u_ref[...] = u
        o_ref[...] = _swiglu_chunk(u, wgt_ref, wut_ref, w2t_ref) if tail else jnp.zeros_like(o_ref)

    last = pl.num_programs(1) - 1

    @pl.when(j < last)
    def _():
        o_ref[...] += _swiglu_chunk(u_ref[...], wg_ref, wu_ref, w2_ref)

    @pl.when(j == last)
    def _():
        acc = o_ref[...] + _swiglu_chunk(u_ref[...], wg_ref, wu_ref, w2_ref)
        hn = h_ref[...] + (0.5 * gate_ref[...]) * acc
        if final_norm:
            hn = _rmsnorm(hn, fg_ref[...])
        o_ref[...] = hn


def _ffn_call(h, g, shift, scale, gate, weights, final_g=None):
    wg, wu, w2, tails = weights
    m, d = h.shape
    nb = shift.shape[0]
    tf = min(FFN_CHUNK, wu.shape[1])
    n_chunks = wu.shape[1] // tf
    tm = _largest_tile(m // nb, 512, SUBLANES_F32)
    tpb = (m // nb) // tm
    row = lambda i, j: (i // tpb, 0, 0)
    const = lambda i, j: (0, 0)
    once = pl.Buffered(1)
    in_specs = [
        pl.BlockSpec((tm, d), lambda i, j: (i, 0)),
        pl.BlockSpec((1, d), const),
        pl.BlockSpec((None, 1, d), row),
        pl.BlockSpec((None, 1, d), row),
        pl.BlockSpec((None, 1, d), row),
        pl.BlockSpec((d, tf), lambda i, j: (0, j)),
        pl.BlockSpec((d, tf), lambda i, j: (0, j)),
        pl.BlockSpec((tf, d), lambda i, j: (j, 0)),
    ]
    args = [h, g.reshape(1, d), shift, scale, gate, wg, wu, w2]
    if tails is not None:
        for w in tails:
            in_specs.append(pl.BlockSpec(w.shape, const, pipeline_mode=once))
            args.append(w)
    if final_g is not None:
        in_specs.append(pl.BlockSpec((1, d), const))
        args.append(final_g.reshape(1, d))
    return pl.pallas_call(
        functools.partial(_ffn_kernel, tail=tails is not None, final_norm=final_g is not None),
        grid=(m // tm, n_chunks),
        in_specs=in_specs,
        out_specs=pl.BlockSpec((tm, d), lambda i, j: (i, 0)),
        out_shape=jax.ShapeDtypeStruct((m, d), F32),
        scratch_shapes=[pltpu.VMEM((tm, d), BF16)],
        compiler_params=_cparams(("parallel", "arbitrary")),
        name="ffn",
    )(*args)


def _inproj_kernel(h_ref, g_ref, shift_ref, scale_ref, w_ref, o_ref, ua_ref, ub_ref, *, n_slices):
    i, j = pl.program_id(0), pl.program_id(1)
    tm = h_ref.shape[0]
    norm = lambda x: _norm_mod(x, g_ref[...], shift_ref[...], scale_ref[...]).astype(BF16)

    def step(cur_ref, nxt_ref):
        @pl.when((j == 0) & ((i == 0) | (n_slices == 0)))
        def _():
            cur_ref[...] = norm(h_ref[...])

        if n_slices:
            rows = tm // n_slices
            r0 = pl.multiple_of(jnp.clip(j - 1, 0, n_slices - 1) * rows, rows)
            nxt_ref[pl.ds(r0, rows), :] = norm(h_ref[pl.ds(r0, rows), :])
        o_ref[...] = jnp.dot(cur_ref[...], w_ref[...], preferred_element_type=F32).astype(o_ref.dtype)

    pl.when(i % 2 == 0)(lambda: step(ua_ref, ub_ref))
    pl.when(i % 2 == 1)(lambda: step(ub_ref, ua_ref))


def _inproj_call(h, g, shift, scale, w):
    m, d = h.shape
    nb = shift.shape[0]
    nc = w.shape[1]
    tm = _largest_tile(m // nb, 1024, SUBLANES_BF16)
    tpb = (m // nb) // tm
    tn = _largest_tile(nc, 1280, LANES)
    n_i, n_j = m // tm, nc // tn
    n_slices = 0
    while 2 * max(n_slices, 1) <= n_j - 1 and tm % (2 * max(n_slices, 1) * SUBLANES_BF16) == 0:
        n_slices = 2 * max(n_slices, 1)
    if n_slices:
        tile = lambda i, j: jnp.minimum(i + jnp.where(j > 0, 1, 0), n_i - 1)
    else:
        tile = lambda i, j: i
    row = lambda i, j: (tile(i, j) // tpb, 0, 0)
    return pl.pallas_call(
        functools.partial(_inproj_kernel, n_slices=n_slices),
        grid=(n_i, n_j),
        in_specs=[
            pl.BlockSpec((tm, d), lambda i, j: (tile(i, j), 0)),
            pl.BlockSpec((1, d), lambda i, j: (0, 0)),
            pl.BlockSpec((None, 1, d), row),
            pl.BlockSpec((None, 1, d), row),
            pl.BlockSpec((d, tn), lambda i, j: (0, j)),
        ],
        out_specs=pl.BlockSpec((tm, tn), lambda i, j: (i, j)),
        out_shape=jax.ShapeDtypeStruct((m, nc), BF16),
        scratch_shapes=[pltpu.VMEM((tm, d), BF16), pltpu.VMEM((tm, d), BF16)],
        compiler_params=_cparams(("arbitrary", "arbitrary")),
        name="in_proj",
    )(h, g.reshape(1, d), shift, scale, w)


def _fill_padded(pad_ref, main, prev_tail, next_head, has_prev, has_next, t):
    pad_ref[0:8, :] = jnp.where(has_prev, prev_tail, 0.0)
    pad_ref[8:8 + t, :] = main
    pad_ref[8 + t:16 + t, :] = jnp.where(has_next, next_head, 0.0)


def _rglru_kernel(zf_ref, zfp_ref, zfn_ref, zb_ref, zbp_ref, zbn_ref, cw_ref, cb_ref, wa_ref, ba_ref,
                  wx_ref, bx_ref, lam_ref, h0_ref, of_ref, ob_ref, last_ref,
                  xpad_ref, xa_ref, a_ref, u_ref, hs_ref, car_ref, *, t, n_t, n_blk):
    i = pl.program_id(1)

    @pl.when(i == 0)
    def _():
        car_ref[...] = h0_ref[...]

    sp = _softplus(-lam_ref[...])

    def gates(d, z_ref, zp_ref, zn_ref, tile):
        _fill_padded(xpad_ref, z_ref[...].astype(F32), zp_ref[...].astype(F32)[8:16],
                     zn_ref[...].astype(F32)[0:8], tile > 0, tile < n_t - 1, t)
        xa = cb_ref[...]
        for k in range(RNN_CONV):
            off = 8 - RNN_CONV_LEFT + k
            xa = xa + cw_ref[k:k + 1, :] * xpad_ref[off:off + t, :]
        xa_ref[...] = xa
        for n in range(n_blk):
            sl = slice(n * RNN_BLOCK, (n + 1) * RNN_BLOCK)
            xs = xa_ref[:, sl]
            xb = xs.astype(BF16)
            rg = jax.nn.sigmoid(jnp.dot(xb, wa_ref[d, n], preferred_element_type=F32) + ba_ref[d:d + 1, sl])
            ig = jax.nn.sigmoid(jnp.dot(xb, wx_ref[d, n], preferred_element_type=F32) + bx_ref[d:d + 1, sl])
            log_a = (-LRU_C * rg) * sp[d:d + 1, sl]
            a = jnp.exp(log_a)
            a_ref[d, :, sl] = a
            u_ref[d, :, sl] = jnp.sqrt(_one_minus_exp2(log_a, a)) * (ig * xs)

    gates(0, zf_ref, zfp_ref, zfn_ref, i)
    gates(1, zb_ref, zbp_ref, zbn_ref, n_t - 1 - i)

    def step(s, carry):
        hf, hb = carry
        sb = t - 1 - s
        hf = a_ref[0, pl.ds(s, 1), :] * hf + u_ref[0, pl.ds(s, 1), :]
        hb = a_ref[1, pl.ds(sb, 1), :] * hb + u_ref[1, pl.ds(sb, 1), :]
        hs_ref[0, pl.ds(s, 1), :] = hf
        hs_ref[1, pl.ds(sb, 1), :] = hb
        return hf, hb

    hf, hb = lax.fori_loop(0, t, step, (car_ref[0:1, :], car_ref[1:2, :]), unroll=8)
    car_ref[0:1, :] = hf
    car_ref[1:2, :] = hb
    of_ref[...] = hs_ref[0].astype(of_ref.dtype)
    ob_ref[...] = hs_ref[1].astype(ob_ref.dtype)
    last_ref[0:1, :] = hf
    last_ref[1:2, :] = hb


def _rglru_call(z, col_rx, cw, cb, wa, ba, wx, bx, lam, h0):
    b, l, _ = z.shape
    c = cw.shape[1]
    n_blk = c // RNN_BLOCK
    t = _largest_tile(l, 512, HALO)
    n_t = l // t
    hb_per = t // HALO
    n_h = l // HALO

    def main(rev):
        return lambda bb, i: (bb, (n_t - 1 - i) if rev else i, col_rx)

    def prev(rev):
        return lambda bb, i: (bb, jnp.maximum(((n_t - 1 - i) if rev else i) * hb_per - 1, 0), col_rx)

    def nxt(rev):
        return lambda bb, i: (bb, jnp.minimum((((n_t - 1 - i) if rev else i) + 1) * hb_per, n_h - 1), col_rx)

    const2 = lambda bb, i: (0, 0)
    const4 = lambda bb, i: (0, 0, 0, 0)
    in_specs = []
    for rev in (False, True):
        in_specs += [pl.BlockSpec((None, t, c), main(rev)),
                     pl.BlockSpec((None, HALO, c), prev(rev)),
                     pl.BlockSpec((None, HALO, c), nxt(rev))]
    in_specs += [
        pl.BlockSpec((RNN_CONV, c), const2),
        pl.BlockSpec((1, c), const2),
        pl.BlockSpec((2, n_blk, RNN_BLOCK, RNN_BLOCK), const4),
        pl.BlockSpec((2, c), const2),
        pl.BlockSpec((2, n_blk, RNN_BLOCK, RNN_BLOCK), const4),
        pl.BlockSpec((2, c), const2),
        pl.BlockSpec((2, c), const2),
        pl.BlockSpec((None, 2, c), lambda bb, i: (bb, 0, 0)),
    ]
    return pl.pallas_call(
        functools.partial(_rglru_kernel, t=t, n_t=n_t, n_blk=n_blk),
        grid=(b, n_t),
        in_specs=in_specs,
        out_specs=[
            pl.BlockSpec((None, t, c), lambda bb, i: (bb, i, 0)),
            pl.BlockSpec((None, t, c), lambda bb, i: (bb, n_t - 1 - i, 0)),
            pl.BlockSpec((None, 2, c), lambda bb, i: (bb, 0, 0)),
        ],
        out_shape=[
            jax.ShapeDtypeStruct((b, l, c), BF16),
            jax.ShapeDtypeStruct((b, l, c), BF16),
            jax.ShapeDtypeStruct((b, 2, c), F32),
        ],
        scratch_shapes=[
            pltpu.VMEM((t + 16, c), F32),
            pltpu.VMEM((t, c), F32),
            pltpu.VMEM((2, t, c), F32),
            pltpu.VMEM((2, t, c), F32),
            pltpu.VMEM((2, t, c), F32),
            pltpu.VMEM((2, c), F32),
        ],
        compiler_params=_cparams(("parallel", "arbitrary")),
        name="rglru",
    )(z, z, z, z, z, z, cw, cb.reshape(1, c), wa, ba, wx, bx, lam, h0)


def _rope(x, cos_t, sin_t):
    lane = lax.broadcasted_iota(jnp.int32, x.shape, 1)
    partner = jnp.where((lane & 32) == 0, pltpu.roll(x, LANES - 32, 1), pltpu.roll(x, 32, 1))
    return x * cos_t + partner * sin_t


def _attn_kernel(sink_ref, q_ref, *rest, tq, n_kv, seq, local):
    if local:
        (km_ref, kp_ref, kn_ref, vm_ref, vp_ref, vn_ref, kc_ref, vc_ref,
         cm_ref, cp_ref, cn_ref, sm_ref, sp_ref, sn_ref, o_ref) = rest
    else:
        kc_ref, vc_ref, o_ref = rest
    i = pl.program_id(1)
    q_scale = HEAD_DIM ** -0.5 * LOG2E
    nt = (((1,), (1,)), ((), ()))
    sub = WINDOW
    n_ctx = kc_ref.shape[0]

    if local:
        cos_q, sin_q = cm_ref[...], sm_ref[...]
        cos_k = jnp.concatenate([cp_ref[...], cos_q, cn_ref[...]], axis=0)
        sin_k = jnp.concatenate([sp_ref[...], sin_q, sn_ref[...]], axis=0)
        r = lax.broadcasted_iota(jnp.int32, (sub, 3 * sub + n_ctx), 0)
        c = lax.broadcasted_iota(jnp.int32, (sub, 3 * sub + n_ctx), 1)
        band = (c >= r) & (c <= r + 2 * WINDOW)
        is_ctx = c >= 3 * sub

    for kh in range(n_kv):
        ks = slice(kh * HEAD_DIM, (kh + 1) * HEAD_DIM)
        kc = kc_ref[:, ks]
        vc = vc_ref[:, ks]
        heads = [kh * GROUP + g for g in range(GROUP)]
        qh = []
        for h in heads:
            x = q_ref[:, h * HEAD_DIM:(h + 1) * HEAD_DIM].astype(F32)
            if local:
                x = _rope(x, cos_q, sin_q)
            qh.append((x * q_scale).astype(BF16))
        if local:
            k_span = jnp.concatenate([kp_ref[:, ks], km_ref[:, ks], kn_ref[:, ks]], axis=0)
            k_span = _rope(k_span.astype(F32), cos_k, sin_k).astype(BF16)
            v_span = jnp.concatenate([vp_ref[:, ks], vm_ref[:, ks], vn_ref[:, ks]], axis=0)
        sink2 = jnp.concatenate([jnp.full((sub, 1), sink_ref[h] * LOG2E, F32) for h in heads], axis=0)

        for sb in range(tq // sub):
            rows = slice(sb * sub, (sb + 1) * sub)
            qs = jnp.concatenate([x[rows] for x in qh], axis=0)
            if local:
                keys = jnp.concatenate([k_span[sb * sub:(sb + 3) * sub], kc], axis=0)
                vals = jnp.concatenate([v_span[sb * sub:(sb + 3) * sub], vc], axis=0)
                kpos = i * tq + (sb - 1) * sub + c
                valid = is_ctx | (band & (kpos >= 0) & (kpos < seq))
                bias = jnp.where(valid, 0.0, NEG_INF)
            else:
                keys, vals = kc, vc
            s = lax.dot_general(qs, keys, nt, preferred_element_type=F32)
            if local:
                s = s + jnp.concatenate([bias] * GROUP, axis=0)
            m = jnp.maximum(jnp.max(s, axis=-1, keepdims=True), sink2)
            p = jnp.exp2(s - m)
            den = jnp.sum(p, axis=-1, keepdims=True) + jnp.exp2(sink2 - m)
            o = jnp.dot(p.astype(BF16), vals, preferred_element_type=F32) / den
            for g, h in enumerate(heads):
                o_ref[rows, h * HEAD_DIM:(h + 1) * HEAD_DIM] = o[g * sub:(g + 1) * sub].astype(o_ref.dtype)


def _attn_operands(sink, z, zc, cols, cos_t, sin_t, local, tq):
    l = z.shape[1]
    n_ctx = zc.shape[1]
    n_heads = sink.shape[0]
    n_kv = n_heads // GROUP
    qw, kw = n_heads * HEAD_DIM, n_kv * HEAD_DIM
    col_q, col_k, col_v = cols
    assert tq % WINDOW == 0
    wb = tq // WINDOW
    n_w = l // WINDOW
    prev = lambda bb, i: jnp.maximum(i * wb - 1, 0)
    nxt = lambda bb, i: jnp.minimum((i + 1) * wb, n_w - 1)
    in_specs = [
        pl.BlockSpec(memory_space=pltpu.SMEM),
        pl.BlockSpec((None, tq, qw), lambda bb, i: (bb, i, col_q)),
    ]
    args = [sink, z]
    if local:
        for col in (col_k, col_v):
            in_specs += [
                pl.BlockSpec((None, tq, kw), lambda bb, i, col=col: (bb, i, col)),
                pl.BlockSpec((None, WINDOW, kw), lambda bb, i, col=col: (bb, prev(bb, i), col)),
                pl.BlockSpec((None, WINDOW, kw), lambda bb, i, col=col: (bb, nxt(bb, i), col)),
            ]
            args += [z, z, z]
    in_specs += [
        pl.BlockSpec((None, n_ctx, kw), lambda bb, i: (bb, 0, col_k)),
        pl.BlockSpec((None, n_ctx, kw), lambda bb, i: (bb, 0, col_v)),
    ]
    args += [zc, zc]
    if local:
        for tab in (cos_t, sin_t):
            in_specs += [
                pl.BlockSpec((tq, HEAD_DIM), lambda bb, i: (i, 0)),
                pl.BlockSpec((WINDOW, HEAD_DIM), lambda bb, i: (prev(bb, i), 0)),
                pl.BlockSpec((WINDOW, HEAD_DIM), lambda bb, i: (nxt(bb, i), 0)),
            ]
            args += [tab, tab, tab]
    return in_specs, args


def _merge_kernel(*refs, n_attn, tm, n_t, d, n_kv, seq, local):
    attn_refs = refs[:n_attn]
    (h_ref, gate_ref, hf_ref, hb_ref, rg_ref, sb_ref, cg_ref, cgp_ref, cgn_ref, sx_ref, sxp_ref, sxn_ref,
     g_ref, bm_ref, scw_ref, wb_ref, wo_ref, o_ref, ppad_ref, att_ref) = refs[n_attn:]
    i = pl.program_id(1)
    _attn_kernel(*attn_refs, att_ref, tq=tm, n_kv=n_kv, seq=seq, local=local)
    ya = (hf_ref[...].astype(F32) + hb_ref[...].astype(F32)) * _gelu_tanh(rg_ref[...].astype(F32))

    _fill_padded(ppad_ref,
                 cg_ref[...].astype(F32) * sx_ref[...].astype(F32),
                 (cgp_ref[...].astype(F32) * sxp_ref[...].astype(F32))[8:16],
                 (cgn_ref[...].astype(F32) * sxn_ref[...].astype(F32))[0:8],
                 i > 0, i < n_t - 1, tm)
    conv = None
    for k in range(SC_CONV):
        off = 8 - SC_CONV_LEFT + k
        term = scw_ref[k:k + 1, :] * ppad_ref[off:off + tm, :]
        conv = term if conv is None else conv + term
    yb = sb_ref[...].astype(F32) * conv

    def lift(br, y):
        gates = jax.nn.sigmoid(g_ref[:, br * d:(br + 1) * d].astype(F32) + bm_ref[br:br + 1, :])
        return gates * jnp.dot(y, wb_ref[br], preferred_element_type=F32)

    merged = lift(0, ya.astype(BF16)) + lift(1, yb.astype(BF16)) + lift(2, att_ref[...])
    y = jnp.dot(merged.astype(BF16), wo_ref[...], preferred_element_type=F32)
    o_ref[...] = h_ref[...] + gate_ref[...] * y


def _merge_call(h, gate, hf, hb, z, zc, cols, attn_cols, sink, cos_t, sin_t, local, b_merge, sc_w,
                w_branch, w_out):
    b, l, d = h.shape
    bw = hf.shape[2]
    col_g, col_rg, col_sb, col_cg, col_sx = cols
    tm = _largest_tile(l, 256, WINDOW)
    n_t = l // tm
    attn_specs, attn_args = _attn_operands(sink, z, zc, attn_cols, cos_t, sin_t, local, tm)
    hb_per = tm // HALO
    n_h = l // HALO
    per_batch = gate.shape[0] == b
    tile = lambda col: (lambda bb, i: (bb, i, col))
    prev = lambda col: (lambda bb, i: (bb, jnp.maximum(i * hb_per - 1, 0), col))
    nxt = lambda col: (lambda bb, i: (bb, jnp.minimum((i + 1) * hb_per, n_h - 1), col))
    once = pl.Buffered(1)
    in_specs = [
        pl.BlockSpec((None, tm, d), tile(0)),
        pl.BlockSpec((None, 1, d), lambda bb, i: (bb if per_batch else 0, 0, 0)),
        pl.BlockSpec((None, tm, bw), tile(0)),
        pl.BlockSpec((None, tm, bw), tile(0)),
        pl.BlockSpec((None, tm, bw), tile(col_rg)),
        pl.BlockSpec((None, tm, bw), tile(col_sb)),
        pl.BlockSpec((None, tm, bw), tile(col_cg)),
        pl.BlockSpec((None, HALO, bw), prev(col_cg)),
        pl.BlockSpec((None, HALO, bw), nxt(col_cg)),
        pl.BlockSpec((None, tm, bw), tile(col_sx)),
        pl.BlockSpec((None, HALO, bw), prev(col_sx)),
        pl.BlockSpec((None, HALO, bw), nxt(col_sx)),
        pl.BlockSpec((None, tm, N_BRANCH * d), tile(col_g)),
        pl.BlockSpec((N_BRANCH, d), lambda bb, i: (0, 0)),
        pl.BlockSpec((SC_CONV, bw), lambda bb, i: (0, 0)),
        pl.BlockSpec((N_BRANCH, bw, d), lambda bb, i: (0, 0, 0), pipeline_mode=once),
        pl.BlockSpec((d, d), lambda bb, i: (0, 0), pipeline_mode=once),
    ]
    return pl.pallas_call(
        functools.partial(_merge_kernel, n_attn=len(attn_specs), tm=tm, n_t=n_t, d=d,
                          n_kv=sink.shape[0] // GROUP, seq=l, local=local),
        grid=(b, n_t),
        in_specs=attn_specs + in_specs,
        out_specs=pl.BlockSpec((None, tm, d), tile(0)),
        out_shape=jax.ShapeDtypeStruct((b, l, d), F32),
        scratch_shapes=[pltpu.VMEM((tm + 16, bw), F32), pltpu.VMEM((tm, bw), BF16)],
        compiler_params=_cparams(("parallel", "parallel")),
        name="merge_local" if local else "merge_ctx",
    )(*attn_args, h, gate, hf, hb, z, z, z, z, z, z, z, z, z, b_merge, sc_w, w_branch, w_out)


def _prep_ffn(w13, w2):
    f = w2.shape[0]
    tf = min(FFN_CHUNK, f // LANES * LANES)
    fm = f // tf * tf
    tails = None
    if fm < f:
        tails = (w13[:, fm:f].astype(BF16), w13[:, f + fm:].astype(BF16), w2[fm:].astype(BF16))
    return w13.astype(BF16), w13[:, f:f + fm].astype(BF16), w2.astype(BF16), tails


def _prep_w_in(w_in, d, bw, kw):
    o = 0
    pieces = []
    for n in (bw, bw, bw, bw, bw, bw, kw, kw, N_BRANCH * d):
        pieces.append(w_in[:, o:o + n])
        o += n
    return jnp.concatenate([pieces[8]] + pieces[:8], axis=1).astype(BF16)


def _rope_tables(l):
    pos = jnp.arange(l)
    row = (pos // GRID_W).astype(F32)
    col = (pos % GRID_W).astype(F32)
    half = HEAD_DIM // 2
    inv = ROPE_BASE ** (-jnp.arange(0, half, 2, dtype=F32) / half)
    ar, ac = row[:, None] * inv, col[:, None] * inv
    cos_t = jnp.concatenate([jnp.cos(ar), jnp.cos(ar), jnp.cos(ac), jnp.cos(ac)], axis=-1)
    sin_t = jnp.concatenate([-jnp.sin(ar), jnp.sin(ar), -jnp.sin(ac), jnp.sin(ac)], axis=-1)
    return cos_t, sin_t


def kernel(x, c, ctx, c_ctx, ada_w, ada_b, norm_g, ffn1_w13, ffn1_w2, w_in, b_merge, rnn_conv_w,
           rnn_conv_b, lru_w_a, lru_b_a, lru_w_x, lru_b_x, lru_lambda, sc_conv_w, attn_sink, w_branch,
           w_out, ffn2_w13, ffn2_w2, final_norm_g):
    b, l, d = x.shape
    n_ctx = ctx.shape[1]
    depth = ada_w.shape[0]
    bw = w_branch.shape[2]
    n_heads = attn_sink.shape[1]
    kw = (n_heads // GROUP) * HEAD_DIM
    assert bw == n_heads * HEAD_DIM and w_in.shape[2] == 6 * bw + 2 * kw + N_BRANCH * d
    assert (N_BRANCH * d) % bw == 0 and (N_BRANCH * d + 6 * bw) % kw == 0 and b + 1 <= 8
    g_blocks = N_BRANCH * d // bw
    col_rx, col_rg, col_sb, col_cg, col_sx, col_q = (g_blocks + n for n in range(6))
    col_k = (N_BRANCH * d + 6 * bw) // kw
    col_v = col_k + 1

    cvec = jnp.zeros((8, d), F32).at[:b].set(c).at[b].set(c_ctx)
    mod = _ada_call(cvec, ada_w, ada_b).reshape(depth, 8, N_MOD, 1, d)
    cos_t, sin_t = _rope_tables(l)

    h = x.reshape(b * l, d)
    hc = ctx.reshape(b * n_ctx, d)
    for layer in range(depth):
        last = layer == depth - 1
        ml = mod[layer, :b]
        mc = mod[layer, b:b + 1]
        ng = norm_g[layer]
        ffn_a = _prep_ffn(ffn1_w13[layer], ffn1_w2[layer])
        ffn_b = _prep_ffn(ffn2_w13[layer], ffn2_w2[layer])
        w_in_p = _prep_w_in(w_in[layer], d, bw, kw)
        wa, wx = lru_w_a[layer].astype(BF16), lru_w_x[layer].astype(BF16)
        wbr, wo = w_branch[layer].astype(BF16), w_out[layer].astype(BF16)

        h = _ffn_call(h, ng[0], ml[:, 0], ml[:, 1], ml[:, 2], ffn_a)
        hc = _ffn_call(hc, ng[0], mc[:, 0], mc[:, 1], mc[:, 2], ffn_a)

        z = _inproj_call(h, ng[1], ml[:, 3], ml[:, 4], w_in_p).reshape(b, l, -1)
        zc = _inproj_call(hc, ng[1], mc[:, 3], mc[:, 4], w_in_p).reshape(b, n_ctx, -1)
        lru = (rnn_conv_w[layer], rnn_conv_b[layer], wa, lru_b_a[layer], wx, lru_b_x[layer], lru_lambda[layer])
        hcf, hcb, h_last = _rglru_call(zc, col_rx, *lru, jnp.zeros((b, 2, bw), F32))
        hlf, hlb, _ = _rglru_call(z, col_rx, *lru, h_last)
        merge_cols = (0, col_rg, col_sb, col_cg, col_sx)
        attn_cols = (col_q, col_k, col_v)
        mix = (attn_sink[layer], cos_t, sin_t)
        mix_w = (b_merge[layer], sc_conv_w[layer], wbr, wo)
        h = _merge_call(h.reshape(b, l, d), ml[:, 5], hlf, hlb, z, zc, merge_cols, attn_cols, *mix, True,
                        *mix_w).reshape(b * l, d)

        h = _ffn_call(h, ng[2], ml[:, 6], ml[:, 7], ml[:, 8], ffn_b,
                      final_g=final_norm_g if last else None)
        if not last:
            hc = _merge_call(hc.reshape(b, n_ctx, d), mc[:, 5], hcf, hcb, zc, zc, merge_cols, attn_cols, *mix,
                             False, *mix_w).reshape(b * n_ctx, d)
            hc = _ffn_call(hc, ng[2], mc[:, 6], mc[:, 7], mc[:, 8], ffn_b)
    return h.reshape(b, l, d)
```

```python
import functools
import math

import jax
import jax.numpy as jnp
from jax import lax
from jax.experimental import pallas as pl
from jax.experimental.pallas import tpu as pltpu

F32 = jnp.float32
BF16 = jnp.bfloat16

HEAD_DIM = 128
GROUP = 4
WINDOW = 128
GRID_W = 64
ROPE_BASE = 10000.0
RNN_BLOCK = 128
RNN_CONV = 4
RNN_CONV_LEFT = 2
SC_CONV = 3
SC_CONV_LEFT = 1
LRU_C = 8.0
N_BRANCH = 3
N_MOD = 9
EPS = 1e-6
NEG_INF = -1e30
LOG2E = 1.4426950408889634

LANES = 128
SUBLANES_F32 = 8
SUBLANES_BF16 = 16
HALO = SUBLANES_BF16
VMEM_LIMIT_BYTES = 56 * 1024 * 1024
FFN_CHUNK = 1024


def _cparams(semantics):
    return pltpu.CompilerParams(dimension_semantics=semantics, vmem_limit_bytes=VMEM_LIMIT_BYTES)


def _pow2_slices(n_steps, tm):
    n = 0
    while 2 * max(n, 1) <= n_steps and tm % (2 * max(n, 1) * SUBLANES_BF16) == 0:
        n = 2 * max(n, 1)
    return n


def _largest_tile(n, cap, quantum):
    best = None
    t = quantum
    while t <= min(n, cap):
        if n % t == 0:
            best = t
        t += quantum
    assert best is not None, (n, cap, quantum)
    return best


def _rmsnorm(x, g):
    return x * lax.rsqrt(jnp.mean(x * x, axis=-1, keepdims=True) + EPS) * g


def _norm_mod(x, g, shift, scale):
    return _rmsnorm(x, g) * (1.0 + scale) + shift


def _silu(x):
    return x * jax.nn.sigmoid(x)


def _gelu_tanh(x):
    return 0.5 * x * (1.0 + jnp.tanh(0.7978845608028654 * (x + 0.044715 * (x * x * x))))


def _softplus(x):
    return jnp.maximum(x, 0.0) + jnp.log1p(jnp.exp(-jnp.abs(x)))


def _one_minus_exp2(x, ex):
    kahan = (ex - 1.0) * x / jnp.log(ex)
    em1 = jnp.where(x < -1.0, ex - 1.0, jnp.where(ex == 1.0, x, kahan))
    return -em1 * (ex + 1.0)


def _ada_kernel(c_ref, w_ref, b_ref, o_ref):
    s = _silu(c_ref[...]).astype(BF16)
    o_ref[...] = jnp.dot(s, w_ref[...].astype(BF16), preferred_element_type=F32) + b_ref[...]


def _ada_call(cvec, ada_w, ada_b):
    depth, d, nm = ada_w.shape
    tn = _largest_tile(nm, 1024, LANES)
    return pl.pallas_call(
        _ada_kernel,
        grid=(depth, nm // tn),
        in_specs=[
            pl.BlockSpec((8, d), lambda l, j: (0, 0)),
            pl.BlockSpec((None, d, tn), lambda l, j: (l, 0, j)),
            pl.BlockSpec((None, 1, tn), lambda l, j: (l, 0, j)),
        ],
        out_specs=pl.BlockSpec((None, 8, tn), lambda l, j: (l, 0, j)),
        out_shape=jax.ShapeDtypeStruct((depth, 8, nm), F32),
        compiler_params=_cparams(("arbitrary", "arbitrary")),
        name="adaln",
    )(cvec, ada_w, ada_b.reshape(depth, 1, nm))


def _swiglu_chunk(u, wg_ref, wu_ref, w2_ref):
    gate = jnp.dot(u, wg_ref[...], preferred_element_type=F32)
    up = jnp.dot(u, wu_ref[...], preferred_element_type=F32)
    act = (_silu(gate) * up).astype(BF16)
    return jnp.dot(act, w2_ref[...], preferred_element_type=F32)


def _ffn_kernel(h_ref, g_ref, shift_ref, scale_ref, gate_ref, wg_ref, wu_ref, w2_ref, *rest, tail, final_norm):
    rest = list(rest)
    wgt_ref, wut_ref, w2t_ref = (rest.pop(0), rest.pop(0), rest.pop(0)) if tail else (None, None, None)
    fg_ref = rest.pop(0) if final_norm else None
    o_ref, u_ref = rest
    j = pl.program_id(1)

    @pl.when(j == 0)
    def _():
        u = _norm_mod(h_ref[...], g_ref[...], shift_ref[...], scale_ref[...]).astype(BF16)
        u_ref[...] = u
        o_ref[...] = _swiglu_chunk(u, wgt_ref, wut_ref, w2t_ref) if tail else jnp.zeros_like(o_ref)

    last = pl.num_programs(1) - 1

    @pl.when(j < last)
    def _():
        o_ref[...] += _swiglu_chunk(u_ref[...], wg_ref, wu_ref, w2_ref)

    @pl.when(j == last)
    def _():
        acc = o_ref[...] + _swiglu_chunk(u_ref[...], wg_ref, wu_ref, w2_ref)
        hn = h_ref[...] + (0.5 * gate_ref[...]) * acc
        if final_norm:
            hn = _rmsnorm(hn, fg_ref[...])
        o_ref[...] = hn


def _ffn_call(h, g, shift, scale, gate, weights, layer, final_g=None):
    wg, wu, w2, tails = weights
    m, d = h.shape
    nb = shift.shape[0]
    tf = min(FFN_CHUNK, wu.shape[2])
    n_chunks = wu.shape[2] // tf
    tm = _largest_tile(m // nb, 512, SUBLANES_F32)
    tpb = (m // nb) // tm
    row = lambda i, j: (i // tpb, 0, 0)
    const = lambda i, j: (0, 0)
    once = pl.Buffered(1)
    in_specs = [
        pl.BlockSpec((tm, d), lambda i, j: (i, 0)),
        pl.BlockSpec((1, d), const),
        pl.BlockSpec((None, 1, d), row),
        pl.BlockSpec((None, 1, d), row),
        pl.BlockSpec((None, 1, d), row),
        pl.BlockSpec((None, d, tf), lambda i, j: (layer, 0, j)),
        pl.BlockSpec((None, d, tf), lambda i, j: (layer, 0, j)),
        pl.BlockSpec((None, tf, d), lambda i, j: (layer, j, 0)),
    ]
    args = [h, g.reshape(1, d), shift, scale, gate, wg, wu, w2]
    if tails is not None:
        for w in tails:
            in_specs.append(pl.BlockSpec((None,) + w.shape[1:], lambda i, j: (layer, 0, 0), pipeline_mode=once))
            args.append(w)
    if final_g is not None:
        in_specs.append(pl.BlockSpec((1, d), const))
        args.append(final_g.reshape(1, d))
    return pl.pallas_call(
        functools.partial(_ffn_kernel, tail=tails is not None, final_norm=final_g is not None),
        grid=(m // tm, n_chunks),
        in_specs=in_specs,
        out_specs=pl.BlockSpec((tm, d), lambda i, j: (i, 0)),
        out_shape=jax.ShapeDtypeStruct((m, d), F32),
        scratch_shapes=[pltpu.VMEM((tm, d), BF16)],
        compiler_params=_cparams(("parallel", "arbitrary")),
        name="ffn",
    )(*args)


def _inproj_kernel(h_ref, g_ref, shift_ref, scale_ref, w_ref, o_ref, ua_ref, ub_ref, *, n_slices):
    i, j = pl.program_id(0), pl.program_id(1)
    tm = h_ref.shape[0]
    norm = lambda x: _norm_mod(x, g_ref[...], shift_ref[...], scale_ref[...]).astype(BF16)

    def step(cur_ref, nxt_ref):
        @pl.when((j == 0) & ((i == 0) | (n_slices == 0)))
        def _():
            cur_ref[...] = norm(h_ref[...])

        if n_slices:
            rows = tm // n_slices
            r0 = pl.multiple_of(jnp.clip(j - 1, 0, n_slices - 1) * rows, rows)
            nxt_ref[pl.ds(r0, rows), :] = norm(h_ref[pl.ds(r0, rows), :])
        o_ref[...] = jnp.dot(cur_ref[...], w_ref[...], preferred_element_type=F32).astype(o_ref.dtype)

    pl.when(i % 2 == 0)(lambda: step(ua_ref, ub_ref))
    pl.when(i % 2 == 1)(lambda: step(ub_ref, ua_ref))


def _inproj_call(h, g, shift, scale, w, layer):
    m, d = h.shape
    nb = shift.shape[0]
    nc = w.shape[2]
    tm = _largest_tile(m // nb, 1024, SUBLANES_BF16)
    tpb = (m // nb) // tm
    tn = _largest_tile(nc, 1280, LANES)
    n_i, n_j = m // tm, nc // tn
    n_slices = _pow2_slices(n_j - 1, tm)
    if n_slices:
        tile = lambda i, j: jnp.minimum(i + jnp.where(j > 0, 1, 0), n_i - 1)
    else:
        tile = lambda i, j: i
    row = lambda i, j: (tile(i, j) // tpb, 0, 0)
    return pl.pallas_call(
        functools.partial(_inproj_kernel, n_slices=n_slices),
        grid=(n_i, n_j),
        in_specs=[
            pl.BlockSpec((tm, d), lambda i, j: (tile(i, j), 0)),
            pl.BlockSpec((1, d), lambda i, j: (0, 0)),
            pl.BlockSpec((None, 1, d), row),
            pl.BlockSpec((None, 1, d), row),
            pl.BlockSpec((None, d, tn), lambda i, j: (layer, 0, j)),
        ],
        out_specs=pl.BlockSpec((tm, tn), lambda i, j: (i, j)),
        out_shape=jax.ShapeDtypeStruct((m, nc), BF16),
        scratch_shapes=[pltpu.VMEM((tm, d), BF16), pltpu.VMEM((tm, d), BF16)],
        compiler_params=_cparams(("arbitrary", "arbitrary")),
        name="in_proj",
    )(h, g.reshape(1, d), shift, scale, w)


def _fill_padded(pad_ref, main, prev_tail, next_head, has_prev, has_next, t):
    pad_ref[0:8, :] = jnp.where(has_prev, prev_tail, 0.0)
    pad_ref[8:8 + t, :] = main
    pad_ref[8 + t:16 + t, :] = jnp.where(has_next, next_head, 0.0)


def _rglru_kernel(zf_ref, zfp_ref, zfn_ref, zb_ref, zbp_ref, zbn_ref, cw_ref, cb_ref, wa_ref, ba_ref,
                  wx_ref, bx_ref, lam_ref, h0_ref, of_ref, ob_ref, last_ref,
                  xpad_ref, xa_ref, a_ref, u_ref, hs_ref, car_ref, *, t, n_t, n_blk):
    i = pl.program_id(1)

    @pl.when(i == 0)
    def _():
        car_ref[...] = h0_ref[...]

    sp = _softplus(-lam_ref[...])

    def gates(d, z_ref, zp_ref, zn_ref, tile):
        _fill_padded(xpad_ref, z_ref[...].astype(F32), zp_ref[...].astype(F32)[8:16],
                     zn_ref[...].astype(F32)[0:8], tile > 0, tile < n_t - 1, t)
        xa = cb_ref[...]
        for k in range(RNN_CONV):
            off = 8 - RNN_CONV_LEFT + k
            xa = xa + cw_ref[k:k + 1, :] * xpad_ref[off:off + t, :]
        xa_ref[...] = xa
        for n in range(n_blk):
            sl = slice(n * RNN_BLOCK, (n + 1) * RNN_BLOCK)
            xs = xa_ref[:, sl]
            xb = xs.astype(BF16)
            rg = jax.nn.sigmoid(jnp.dot(xb, wa_ref[d, n], preferred_element_type=F32) + ba_ref[d:d + 1, sl])
            ig = jax.nn.sigmoid(jnp.dot(xb, wx_ref[d, n], preferred_element_type=F32) + bx_ref[d:d + 1, sl])
            log_a = (-LRU_C * rg) * sp[d:d + 1, sl]
            a = jnp.exp(log_a)
            a_ref[d, :, sl] = a
            u_ref[d, :, sl] = jnp.sqrt(_one_minus_exp2(log_a, a)) * (ig * xs)

    gates(0, zf_ref, zfp_ref, zfn_ref, i)
    gates(1, zb_ref, zbp_ref, zbn_ref, n_t - 1 - i)

    def step(s, carry):
        hf, hb = carry
        sb = t - 1 - s
        hf = a_ref[0, pl.ds(s, 1), :] * hf + u_ref[0, pl.ds(s, 1), :]
        hb = a_ref[1, pl.ds(sb, 1), :] * hb + u_ref[1, pl.ds(sb, 1), :]
        hs_ref[0, pl.ds(s, 1), :] = hf
        hs_ref[1, pl.ds(sb, 1), :] = hb
        return hf, hb

    hf, hb = lax.fori_loop(0, t, step, (car_ref[0:1, :], car_ref[1:2, :]), unroll=8)
    car_ref[0:1, :] = hf
    car_ref[1:2, :] = hb
    of_ref[...] = hs_ref[0].astype(of_ref.dtype)
    ob_ref[...] = hs_ref[1].astype(ob_ref.dtype)
    last_ref[0:1, :] = hf
    last_ref[1:2, :] = hb


def _rglru_call(z, col_rx, cw, cb, wa, ba, wx, bx, lam, h0):
    b, l, _ = z.shape
    c = cw.shape[1]
    n_blk = c // RNN_BLOCK
    t = _largest_tile(l, 512, HALO)
    n_t = l // t
    hb_per = t // HALO
    n_h = l // HALO

    def main(rev):
        return lambda bb, i: (bb, (n_t - 1 - i) if rev else i, col_rx)

    def prev(rev):
        return lambda bb, i: (bb, jnp.maximum(((n_t - 1 - i) if rev else i) * hb_per - 1, 0), col_rx)

    def nxt(rev):
        return lambda bb, i: (bb, jnp.minimum((((n_t - 1 - i) if rev else i) + 1) * hb_per, n_h - 1), col_rx)

    const2 = lambda bb, i: (0, 0)
    const4 = lambda bb, i: (0, 0, 0, 0)
    in_specs = []
    for rev in (False, True):
        in_specs += [pl.BlockSpec((None, t, c), main(rev)),
                     pl.BlockSpec((None, HALO, c), prev(rev)),
                     pl.BlockSpec((None, HALO, c), nxt(rev))]
    in_specs += [
        pl.BlockSpec((RNN_CONV, c), const2),
        pl.BlockSpec((1, c), const2),
        pl.BlockSpec((2, n_blk, RNN_BLOCK, RNN_BLOCK), const4),
        pl.BlockSpec((2, c), const2),
        pl.BlockSpec((2, n_blk, RNN_BLOCK, RNN_BLOCK), const4),
        pl.BlockSpec((2, c), const2),
        pl.BlockSpec((2, c), const2),
        pl.BlockSpec((None, 2, c), lambda bb, i: (bb, 0, 0)),
    ]
    return pl.pallas_call(
        functools.partial(_rglru_kernel, t=t, n_t=n_t, n_blk=n_blk),
        grid=(b, n_t),
        in_specs=in_specs,
        out_specs=[
            pl.BlockSpec((None, t, c), lambda bb, i: (bb, i, 0)),
            pl.BlockSpec((None, t, c), lambda bb, i: (bb, n_t - 1 - i, 0)),
            pl.BlockSpec((None, 2, c), lambda bb, i: (bb, 0, 0)),
        ],
        out_shape=[
            jax.ShapeDtypeStruct((b, l, c), BF16),
            jax.ShapeDtypeStruct((b, l, c), BF16),
            jax.ShapeDtypeStruct((b, 2, c), F32),
        ],
        scratch_shapes=[
            pltpu.VMEM((t + 16, c), F32),
            pltpu.VMEM((t, c), F32),
            pltpu.VMEM((2, t, c), F32),
            pltpu.VMEM((2, t, c), F32),
            pltpu.VMEM((2, t, c), F32),
            pltpu.VMEM((2, c), F32),
        ],
        compiler_params=_cparams(("parallel", "arbitrary")),
        name="rglru",
    )(z, z, z, z, z, z, cw, cb.reshape(1, c), wa, ba, wx, bx, lam, h0)


def _rope(x, cos_t, sin_t):
    lane = lax.broadcasted_iota(jnp.int32, x.shape, 1)
    partner = jnp.where((lane & 32) == 0, pltpu.roll(x, LANES - 32, 1), pltpu.roll(x, 32, 1))
    return x * cos_t + partner * sin_t


def _attn_kernel(sink_ref, q_ref, *rest, tq, n_kv, seq, local):
    if local:
        (km_ref, kp_ref, kn_ref, vm_ref, vp_ref, vn_ref, kc_ref, vc_ref,
         cm_ref, cp_ref, cn_ref, sm_ref, sp_ref, sn_ref, o_ref) = rest
    else:
        kc_ref, vc_ref, o_ref = rest
    i = pl.program_id(1)
    q_scale = HEAD_DIM ** -0.5 * LOG2E
    nt = (((1,), (1,)), ((), ()))
    sub = WINDOW
    n_ctx = kc_ref.shape[0]

    if local:
        cos_q, sin_q = cm_ref[...], sm_ref[...]
        cos_k = jnp.concatenate([cp_ref[...], cos_q, cn_ref[...]], axis=0)
        sin_k = jnp.concatenate([sp_ref[...], sin_q, sn_ref[...]], axis=0)
        r = lax.broadcasted_iota(jnp.int32, (sub, 3 * sub + n_ctx), 0)
        c = lax.broadcasted_iota(jnp.int32, (sub, 3 * sub + n_ctx), 1)
        band = (c >= r) & (c <= r + 2 * WINDOW)
        is_ctx = c >= 3 * sub

    for kh in range(n_kv):
        ks = slice(kh * HEAD_DIM, (kh + 1) * HEAD_DIM)
        kc = kc_ref[:, ks]
        vc = vc_ref[:, ks]
        heads = [kh * GROUP + g for g in range(GROUP)]
        qh = []
        for h in heads:
            x = q_ref[:, h * HEAD_DIM:(h + 1) * HEAD_DIM].astype(F32)
            if local:
                x = _rope(x, cos_q, sin_q)
            qh.append((x * q_scale).astype(BF16))
        if local:
            k_span = jnp.concatenate([kp_ref[:, ks], km_ref[:, ks], kn_ref[:, ks]], axis=0)
            k_span = _rope(k_span.astype(F32), cos_k, sin_k).astype(BF16)
            v_span = jnp.concatenate([vp_ref[:, ks], vm_ref[:, ks], vn_ref[:, ks]], axis=0)
        sink2 = jnp.concatenate([jnp.full((sub, 1), sink_ref[h] * LOG2E, F32) for h in heads], axis=0)

        for sb in range(tq // sub):
            rows = slice(sb * sub, (sb + 1) * sub)
            qs = jnp.concatenate([x[rows] for x in qh], axis=0)
            if local:
                keys = jnp.concatenate([k_span[sb * sub:(sb + 3) * sub], kc], axis=0)
                vals = jnp.concatenate([v_span[sb * sub:(sb + 3) * sub], vc], axis=0)
                kpos = i * tq + (sb - 1) * sub + c
                valid = is_ctx | (band & (kpos >= 0) & (kpos < seq))
                bias = jnp.where(valid, 0.0, NEG_INF)
            else:
                keys, vals = kc, vc
            s = lax.dot_general(qs, keys, nt, preferred_element_type=F32)
            if local:
                s = s + jnp.concatenate([bias] * GROUP, axis=0)
            m = jnp.maximum(jnp.max(s, axis=-1, keepdims=True), sink2)
            p = jnp.exp2(s - m)
            den = jnp.sum(p, axis=-1, keepdims=True) + jnp.exp2(sink2 - m)
            o = jnp.dot(p.astype(BF16), vals, preferred_element_type=F32) / den
            for g, h in enumerate(heads):
                o_ref[rows, h * HEAD_DIM:(h + 1) * HEAD_DIM] = o[g * sub:(g + 1) * sub].astype(o_ref.dtype)


def _attn_operands(sink, z, zc, cols, cos_t, sin_t, local, tq):
    l = z.shape[1]
    n_ctx = zc.shape[1]
    n_heads = sink.shape[0]
    n_kv = n_heads // GROUP
    qw, kw = n_heads * HEAD_DIM, n_kv * HEAD_DIM
    col_q, col_k, col_v = cols
    assert tq % WINDOW == 0
    wb = tq // WINDOW
    n_w = l // WINDOW
    prev = lambda bb, i: jnp.maximum(i * wb - 1, 0)
    nxt = lambda bb, i: jnp.minimum((i + 1) * wb, n_w - 1)
    in_specs = [
        pl.BlockSpec(memory_space=pltpu.SMEM),
        pl.BlockSpec((None, tq, qw), lambda bb, i: (bb, i, col_q)),
    ]
    args = [sink, z]
    if local:
        for col in (col_k, col_v):
            in_specs += [
                pl.BlockSpec((None, tq, kw), lambda bb, i, col=col: (bb, i, col)),
                pl.BlockSpec((None, WINDOW, kw), lambda bb, i, col=col: (bb, prev(bb, i), col)),
                pl.BlockSpec((None, WINDOW, kw), lambda bb, i, col=col: (bb, nxt(bb, i), col)),
            ]
            args += [z, z, z]
    in_specs += [
        pl.BlockSpec((None, n_ctx, kw), lambda bb, i: (bb, 0, col_k)),
        pl.BlockSpec((None, n_ctx, kw), lambda bb, i: (bb, 0, col_v)),
    ]
    args += [zc, zc]
    if local:
        for tab in (cos_t, sin_t):
            in_specs += [
                pl.BlockSpec((tq, HEAD_DIM), lambda bb, i: (i, 0)),
                pl.BlockSpec((WINDOW, HEAD_DIM), lambda bb, i: (prev(bb, i), 0)),
                pl.BlockSpec((WINDOW, HEAD_DIM), lambda bb, i: (nxt(bb, i), 0)),
            ]
            args += [tab, tab, tab]
    return in_specs, args


def _merge_kernel(*refs, n_attn, tm, n_t, d, n_kv, seq, local):
    attn_refs = refs[:n_attn]
    (h_ref, gate_ref, hf_ref, hb_ref, rg_ref, sb_ref, cg_ref, cgp_ref, cgn_ref, sx_ref, sxp_ref, sxn_ref,
     g_ref, bm_ref, scw_ref, wb_ref, wo_ref, o_ref, ppad_ref, att_ref) = refs[n_attn:]
    i = pl.program_id(1)
    _attn_kernel(*attn_refs, att_ref, tq=tm, n_kv=n_kv, seq=seq, local=local)
    ya = (hf_ref[...].astype(F32) + hb_ref[...].astype(F32)) * _gelu_tanh(rg_ref[...].astype(F32))

    _fill_padded(ppad_ref,
                 cg_ref[...].astype(F32) * sx_ref[...].astype(F32),
                 (cgp_ref[...].astype(F32) * sxp_ref[...].astype(F32))[8:16],
                 (cgn_ref[...].astype(F32) * sxn_ref[...].astype(F32))[0:8],
                 i > 0, i < n_t - 1, tm)
    conv = None
    for k in range(SC_CONV):
        off = 8 - SC_CONV_LEFT + k
        term = scw_ref[k:k + 1, :] * ppad_ref[off:off + tm, :]
        conv = term if conv is None else conv + term
    yb = sb_ref[...].astype(F32) * conv

    def lift(br, y):
        gates = jax.nn.sigmoid(g_ref[:, br * d:(br + 1) * d].astype(F32) + bm_ref[br:br + 1, :])
        return gates * jnp.dot(y, wb_ref[br], preferred_element_type=F32)

    merged = lift(0, ya.astype(BF16)) + lift(1, yb.astype(BF16)) + lift(2, att_ref[...])
    y = jnp.dot(merged.astype(BF16), wo_ref[...], preferred_element_type=F32)
    o_ref[...] = h_ref[...] + gate_ref[...] * y


def _merge_call(h, gate, hf, hb, z, zc, cols, attn_cols, sink, cos_t, sin_t, local, b_merge, sc_w,
                w_branch, w_out, layer):
    b, l, d = h.shape
    bw = hf.shape[2]
    col_g, col_rg, col_sb, col_cg, col_sx = cols
    tm = _largest_tile(l, 256, WINDOW)
    n_t = l // tm
    attn_specs, attn_args = _attn_operands(sink, z, zc, attn_cols, cos_t, sin_t, local, tm)
    hb_per = tm // HALO
    n_h = l // HALO
    per_batch = gate.shape[0] == b
    tile = lambda col: (lambda bb, i: (bb, i, col))
    prev = lambda col: (lambda bb, i: (bb, jnp.maximum(i * hb_per - 1, 0), col))
    nxt = lambda col: (lambda bb, i: (bb, jnp.minimum((i + 1) * hb_per, n_h - 1), col))
    once = pl.Buffered(1)
    in_specs = [
        pl.BlockSpec((None, tm, d), tile(0)),
        pl.BlockSpec((None, 1, d), lambda bb, i: (bb if per_batch else 0, 0, 0)),
        pl.BlockSpec((None, tm, bw), tile(0)),
        pl.BlockSpec((None, tm, bw), tile(0)),
        pl.BlockSpec((None, tm, bw), tile(col_rg)),
        pl.BlockSpec((None, tm, bw), tile(col_sb)),
        pl.BlockSpec((None, tm, bw), tile(col_cg)),
        pl.BlockSpec((None, HALO, bw), prev(col_cg)),
        pl.BlockSpec((None, HALO, bw), nxt(col_cg)),
        pl.BlockSpec((None, tm, bw), tile(col_sx)),
        pl.BlockSpec((None, HALO, bw), prev(col_sx)),
        pl.BlockSpec((None, HALO, bw), nxt(col_sx)),
        pl.BlockSpec((None, tm, N_BRANCH * d), tile(col_g)),
        pl.BlockSpec((N_BRANCH, d), lambda bb, i: (0, 0)),
        pl.BlockSpec((SC_CONV, bw), lambda bb, i: (0, 0)),
        pl.BlockSpec((None, N_BRANCH, bw, d), lambda bb, i: (layer, 0, 0, 0), pipeline_mode=once),
        pl.BlockSpec((None, d, d), lambda bb, i: (layer, 0, 0), pipeline_mode=once),
    ]
    return pl.pallas_call(
        functools.partial(_merge_kernel, n_attn=len(attn_specs), tm=tm, n_t=n_t, d=d,
                          n_kv=sink.shape[0] // GROUP, seq=l, local=local),
        grid=(b, n_t),
        in_specs=attn_specs + in_specs,
        out_specs=pl.BlockSpec((None, tm, d), tile(0)),
        out_shape=jax.ShapeDtypeStruct((b, l, d), F32),
        scratch_shapes=[pltpu.VMEM((tm + 16, bw), F32), pltpu.VMEM((tm, bw), BF16)],
        compiler_params=_cparams(("parallel", "parallel")),
        name="merge_local" if local else "merge_ctx",
    )(*attn_args, h, gate, hf, hb, z, z, z, z, z, z, z, z, z, b_merge, sc_w, w_branch, w_out)


def _prep_ffn(w13, w2):
    f = w2.shape[1]
    tf = min(FFN_CHUNK, f // LANES * LANES)
    fm = f // tf * tf
    tails = None
    if fm < f:
        tails = (w13[:, :, fm:f].astype(BF16), w13[:, :, f + fm:].astype(BF16), w2[:, fm:].astype(BF16))
    return w13.astype(BF16), w13[:, :, f:f + fm].astype(BF16), w2.astype(BF16), tails


def _cast_kernel(w_ref, o_ref):
    o_ref[...] = w_ref[...].astype(o_ref.dtype)


def _prep_w_in(w_in, d):
    depth, _, nc = w_in.shape
    n_g = N_BRANCH * d
    tn = _largest_tile(math.gcd(n_g, nc - n_g), 512, LANES)
    n_blk, g_blk = nc // tn, n_g // tn
    return pl.pallas_call(
        _cast_kernel,
        grid=(depth, n_blk),
        in_specs=[pl.BlockSpec((None, d, tn), lambda l, j: (l, 0, (j + n_blk - g_blk) % n_blk))],
        out_specs=pl.BlockSpec((None, d, tn), lambda l, j: (l, 0, j)),
        out_shape=jax.ShapeDtypeStruct(w_in.shape, BF16),
        compiler_params=_cparams(("parallel", "parallel")),
        name="cast_w_in",
    )(w_in)


def _rope_tables(l):
    pos = jnp.arange(l)
    row = (pos // GRID_W).astype(F32)
    col = (pos % GRID_W).astype(F32)
    half = HEAD_DIM // 2
    inv = ROPE_BASE ** (-jnp.arange(0, half, 2, dtype=F32) / half)
    ar, ac = row[:, None] * inv, col[:, None] * inv
    cos_t = jnp.concatenate([jnp.cos(ar), jnp.cos(ar), jnp.cos(ac), jnp.cos(ac)], axis=-1)
    sin_t = jnp.concatenate([-jnp.sin(ar), jnp.sin(ar), -jnp.sin(ac), jnp.sin(ac)], axis=-1)
    return cos_t, sin_t


def kernel(x, c, ctx, c_ctx, ada_w, ada_b, norm_g, ffn1_w13, ffn1_w2, w_in, b_merge, rnn_conv_w,
           rnn_conv_b, lru_w_a, lru_b_a, lru_w_x, lru_b_x, lru_lambda, sc_conv_w, attn_sink, w_branch,
           w_out, ffn2_w13, ffn2_w2, final_norm_g):
    b, l, d = x.shape
    n_ctx = ctx.shape[1]
    depth = ada_w.shape[0]
    bw = w_branch.shape[2]
    n_heads = attn_sink.shape[1]
    kw = (n_heads // GROUP) * HEAD_DIM
    assert bw == n_heads * HEAD_DIM and w_in.shape[2] == 6 * bw + 2 * kw + N_BRANCH * d
    assert (N_BRANCH * d) % bw == 0 and (N_BRANCH * d + 6 * bw) % kw == 0 and b + 1 <= 8
    g_blocks = N_BRANCH * d // bw
    col_rx, col_rg, col_sb, col_cg, col_sx, col_q = (g_blocks + n for n in range(6))
    col_k = (N_BRANCH * d + 6 * bw) // kw
    col_v = col_k + 1

    cvec = jnp.zeros((8, d), F32).at[:b].set(c).at[b].set(c_ctx)
    mod = _ada_call(cvec, ada_w, ada_b).reshape(depth, 8, N_MOD, 1, d)
    cos_t, sin_t = _rope_tables(l)
    w_in_p = _prep_w_in(w_in, d)
    ffn_a = _prep_ffn(ffn1_w13, ffn1_w2)
    ffn_b = _prep_ffn(ffn2_w13, ffn2_w2)
    wbr, wo = w_branch.astype(BF16), w_out.astype(BF16)

    h = x.reshape(b * l, d)
    hc = ctx.reshape(b * n_ctx, d)
    for layer in range(depth):
        last = layer == depth - 1
        ml = mod[layer, :b]
        mc = mod[layer, b:b + 1]
        ng = norm_g[layer]
        wa, wx = lru_w_a[layer].astype(BF16), lru_w_x[layer].astype(BF16)

        h = _ffn_call(h, ng[0], ml[:, 0], ml[:, 1], ml[:, 2], ffn_a, layer)
        hc = _ffn_call(hc, ng[0], mc[:, 0], mc[:, 1], mc[:, 2], ffn_a, layer)

        z = _inproj_call(h, ng[1], ml[:, 3], ml[:, 4], w_in_p, layer).reshape(b, l, -1)
        zc = _inproj_call(hc, ng[1], mc[:, 3], mc[:, 4], w_in_p, layer).reshape(b, n_ctx, -1)
        lru = (rnn_conv_w[layer], rnn_conv_b[layer], wa, lru_b_a[layer], wx, lru_b_x[layer], lru_lambda[layer])
        hcf, hcb, h_last = _rglru_call(zc, col_rx, *lru, jnp.zeros((b, 2, bw), F32))
        hlf, hlb, _ = _rglru_call(z, col_rx, *lru, h_last)
        merge_cols = (0, col_rg, col_sb, col_cg, col_sx)
        attn_cols = (col_q, col_k, col_v)
        mix = (attn_sink[layer], cos_t, sin_t)
        mix_w = (b_merge[layer], sc_conv_w[layer], wbr, wo, layer)
        h = _merge_call(h.reshape(b, l, d), ml[:, 5], hlf, hlb, z, zc, merge_cols, attn_cols, *mix, True,
                        *mix_w).reshape(b * l, d)

        h = _ffn_call(h, ng[2], ml[:, 6], ml[:, 7], ml[:, 8], ffn_b, layer,
                      final_g=final_norm_g if last else None)
        if not last:
            hc = _merge_call(hc.reshape(b, n_ctx, d), mc[:, 5], hcf, hcb, zc, zc, merge_cols, attn_cols, *mix,
                             False, *mix_w).reshape(b * n_ctx, d)
            hc = _ffn_call(hc, ng[2], mc[:, 6], mc[:, 7], mc[:, 8], ffn_b, layer)
    return h.reshape(b, l, d)
```

```python
import functools
import math

import jax
import jax.numpy as jnp
from jax import lax
from jax.experimental import pallas as pl
from jax.experimental.pallas import tpu as pltpu

F32 = jnp.float32
BF16 = jnp.bfloat16

HEAD_DIM = 128
GROUP = 4
WINDOW = 128
GRID_W = 64
ROPE_BASE = 10000.0
RNN_BLOCK = 128
RNN_CONV = 4
RNN_CONV_LEFT = 2
SC_CONV = 3
SC_CONV_LEFT = 1
LRU_C = 8.0
N_BRANCH = 3
N_MOD = 9
EPS = 1e-6
NEG_INF = -1e30
LOG2E = 1.4426950408889634

LANES = 128
SUBLANES_F32 = 8
SUBLANES_BF16 = 16
HALO = SUBLANES_BF16
VMEM_LIMIT_BYTES = 56 * 1024 * 1024
FFN_CHUNK = 1024


def _cparams(semantics):
    return pltpu.CompilerParams(dimension_semantics=semantics, vmem_limit_bytes=VMEM_LIMIT_BYTES)


def _pow2_slices(n_steps, tm):
    n = 0
    while 2 * max(n, 1) <= n_steps and tm % (2 * max(n, 1) * SUBLANES_BF16) == 0:
        n = 2 * max(n, 1)
    return n


def _largest_tile(n, cap, quantum):
    best = None
    t = quantum
    while t <= min(n, cap):
        if n % t == 0:
            best = t
        t += quantum
    assert best is not None, (n, cap, quantum)
    return best


def _rmsnorm(x, g):
    return x * lax.rsqrt(jnp.mean(x * x, axis=-1, keepdims=True) + EPS) * g


def _norm_mod(x, g, shift, scale):
    return _rmsnorm(x, g) * (1.0 + scale) + shift


def _silu(x):
    return x * jax.nn.sigmoid(x)


def _gelu_tanh(x):
    return 0.5 * x * (1.0 + jnp.tanh(0.7978845608028654 * (x + 0.044715 * (x * x * x))))


def _softplus(x):
    return jnp.maximum(x, 0.0) + jnp.log1p(jnp.exp(-jnp.abs(x)))


def _one_minus_exp2(x, ex):
    kahan = (ex - 1.0) * x / jnp.log(ex)
    em1 = jnp.where(x < -1.0, ex - 1.0, jnp.where(ex == 1.0, x, kahan))
    return -em1 * (ex + 1.0)


def _ada_kernel(c_ref, w_ref, b_ref, o_ref):
    s = _silu(c_ref[...]).astype(BF16)
    o_ref[...] = jnp.dot(s, w_ref[...].astype(BF16), preferred_element_type=F32) + b_ref[...]


def _ada_call(cvec, ada_w, ada_b):
    depth, d, nm = ada_w.shape
    tn = _largest_tile(nm, 1024, LANES)
    return pl.pallas_call(
        _ada_kernel,
        grid=(depth, nm // tn),
        in_specs=[
            pl.BlockSpec((8, d), lambda l, j: (0, 0)),
            pl.BlockSpec((None, d, tn), lambda l, j: (l, 0, j)),
            pl.BlockSpec((None, 1, tn), lambda l, j: (l, 0, j)),
        ],
        out_specs=pl.BlockSpec((None, 8, tn), lambda l, j: (l, 0, j)),
        out_shape=jax.ShapeDtypeStruct((depth, 8, nm), F32),
        compiler_params=_cparams(("arbitrary", "arbitrary")),
        name="adaln",
    )(cvec, ada_w, ada_b.reshape(depth, 1, nm))


def _swiglu_chunk(u, wg_ref, wu_ref, w2_ref):
    gate = jnp.dot(u, wg_ref[...], preferred_element_type=F32)
    up = jnp.dot(u, wu_ref[...], preferred_element_type=F32)
    act = (_silu(gate) * up).astype(BF16)
    return jnp.dot(act, w2_ref[...], preferred_element_type=F32)


def _ffn_kernel(h_ref, g_ref, shift_ref, scale_ref, gate_ref, wg_ref, wu_ref, w2_ref, *rest, tail, final_norm):
    rest = list(rest)
    wgt_ref, wut_ref, w2t_ref = (rest.pop(0), rest.pop(0), rest.pop(0)) if tail else (None, None, None)
    fg_ref = rest.pop(0) if final_norm else None
    o_ref, u_ref = rest
    j = pl.program_id(1)

    @pl.when(j == 0)
    def _():
        u = _norm_mod(h_ref[...], g_ref[...], shift_ref[...], scale_ref[...]).astype(BF16)
        u_ref[...] = u
        o_ref[...] = _swiglu_chunk(u, wgt_ref, wut_ref, w2t_ref) if tail else jnp.zeros_like(o_ref)

    last = pl.num_programs(1) - 1

    @pl.when(j < last)
    def _():
        o_ref[...] += _swiglu_chunk(u_ref[...], wg_ref, wu_ref, w2_ref)

    @pl.when(j == last)
    def _():
        acc = o_ref[...] + _swiglu_chunk(u_ref[...], wg_ref, wu_ref, w2_ref)
        hn = h_ref[...] + (0.5 * gate_ref[...]) * acc
        if final_norm:
            hn = _rmsnorm(hn, fg_ref[...])
        o_ref[...] = hn


def _ffn_call(h, g, shift, scale, gate, weights, layer, final_g=None):
    wg, wu, w2, tails = weights
    m, d = h.shape
    nb = shift.shape[0]
    tf = min(FFN_CHUNK, wu.shape[2])
    n_chunks = wu.shape[2] // tf
    tm = _largest_tile(m // nb, 512, SUBLANES_F32)
    tpb = (m // nb) // tm
    row = lambda i, j: (i // tpb, 0, 0)
    const = lambda i, j: (0, 0)
    once = pl.Buffered(1)
    in_specs = [
        pl.BlockSpec((tm, d), lambda i, j: (i, 0)),
        pl.BlockSpec((1, d), const),
        pl.BlockSpec((None, 1, d), row),
        pl.BlockSpec((None, 1, d), row),
        pl.BlockSpec((None, 1, d), row),
        pl.BlockSpec((None, d, tf), lambda i, j: (layer, 0, j)),
        pl.BlockSpec((None, d, tf), lambda i, j: (layer, 0, j)),
        pl.BlockSpec((None, tf, d), lambda i, j: (layer, j, 0)),
    ]
    args = [h, g.reshape(1, d), shift, scale, gate, wg, wu, w2]
    if tails is not None:
        for w in tails:
            in_specs.append(pl.BlockSpec((None,) + w.shape[1:], lambda i, j: (layer, 0, 0), pipeline_mode=once))
            args.append(w)
    if final_g is not None:
        in_specs.append(pl.BlockSpec((1, d), const))
        args.append(final_g.reshape(1, d))
    return pl.pallas_call(
        functools.partial(_ffn_kernel, tail=tails is not None, final_norm=final_g is not None),
        grid=(m // tm, n_chunks),
        in_specs=in_specs,
        out_specs=pl.BlockSpec((tm, d), lambda i, j: (i, 0)),
        out_shape=jax.ShapeDtypeStruct((m, d), F32),
        scratch_shapes=[pltpu.VMEM((tm, d), BF16)],
        compiler_params=_cparams(("parallel", "arbitrary")),
        name="ffn",
    )(*args)


def _inproj_kernel(h_ref, g_ref, shift_ref, scale_ref, w_ref, o_ref, ua_ref, ub_ref, *, n_slices):
    i, j = pl.program_id(0), pl.program_id(1)
    tm = h_ref.shape[0]
    norm = lambda x: _norm_mod(x, g_ref[...], shift_ref[...], scale_ref[...]).astype(BF16)

    def step(cur_ref, nxt_ref):
        @pl.when((j == 0) & ((i == 0) | (n_slices == 0)))
        def _():
            cur_ref[...] = norm(h_ref[...])

        if n_slices:
            rows = tm // n_slices
            r0 = pl.multiple_of(jnp.clip(j - 1, 0, n_slices - 1) * rows, rows)
            nxt_ref[pl.ds(r0, rows), :] = norm(h_ref[pl.ds(r0, rows), :])
        o_ref[...] = jnp.dot(cur_ref[...], w_ref[...], preferred_element_type=F32).astype(o_ref.dtype)

    pl.when(i % 2 == 0)(lambda: step(ua_ref, ub_ref))
    pl.when(i % 2 == 1)(lambda: step(ub_ref, ua_ref))


def _inproj_call(h, g, shift, scale, w, layer):
    m, d = h.shape
    nb = shift.shape[0]
    nc = w.shape[2]
    tm = _largest_tile(m // nb, 1024, SUBLANES_BF16)
    tpb = (m // nb) // tm
    tn = _largest_tile(nc, 1280, LANES)
    n_i, n_j = m // tm, nc // tn
    n_slices = _pow2_slices(n_j - 1, tm)
    if n_slices:
        tile = lambda i, j: jnp.minimum(i + jnp.where(j > 0, 1, 0), n_i - 1)
    else:
        tile = lambda i, j: i
    row = lambda i, j: (tile(i, j) // tpb, 0, 0)
    return pl.pallas_call(
        functools.partial(_inproj_kernel, n_slices=n_slices),
        grid=(n_i, n_j),
        in_specs=[
            pl.BlockSpec((tm, d), lambda i, j: (tile(i, j), 0)),
            pl.BlockSpec((1, d), lambda i, j: (0, 0)),
            pl.BlockSpec((None, 1, d), row),
            pl.BlockSpec((None, 1, d), row),
            pl.BlockSpec((None, d, tn), lambda i, j: (layer, 0, j)),
        ],
        out_specs=pl.BlockSpec((tm, tn), lambda i, j: (i, j)),
        out_shape=jax.ShapeDtypeStruct((m, nc), BF16),
        scratch_shapes=[pltpu.VMEM((tm, d), BF16), pltpu.VMEM((tm, d), BF16)],
        compiler_params=_cparams(("arbitrary", "arbitrary")),
        name="in_proj",
    )(h, g.reshape(1, d), shift, scale, w)


def _fill_padded(pad_ref, main, prev_tail, next_head, has_prev, has_next, t):
    pad_ref[0:8, :] = jnp.where(has_prev, prev_tail, 0.0)
    pad_ref[8:8 + t, :] = main
    pad_ref[8 + t:16 + t, :] = jnp.where(has_next, next_head, 0.0)


def _rglru_kernel(zf_ref, zfp_ref, zfn_ref, zb_ref, zbp_ref, zbn_ref, cw_ref, cb_ref, wa_ref, ba_ref,
                  wx_ref, bx_ref, lam_ref, h0_ref, of_ref, ob_ref, last_ref,
                  xpad_ref, xa_ref, a_ref, u_ref, hs_ref, car_ref, *, t, n_t, n_blk):
    i = pl.program_id(1)

    @pl.when(i == 0)
    def _():
        car_ref[...] = h0_ref[...]

    sp = _softplus(-lam_ref[...])

    def gates(d, z_ref, zp_ref, zn_ref, tile):
        _fill_padded(xpad_ref, z_ref[...].astype(F32), zp_ref[...].astype(F32)[8:16],
                     zn_ref[...].astype(F32)[0:8], tile > 0, tile < n_t - 1, t)
        xa = cb_ref[...]
        for k in range(RNN_CONV):
            off = 8 - RNN_CONV_LEFT + k
            xa = xa + cw_ref[k:k + 1, :] * xpad_ref[off:off + t, :]
        xa_ref[...] = xa
        for n in range(n_blk):
            sl = slice(n * RNN_BLOCK, (n + 1) * RNN_BLOCK)
            xs = xa_ref[:, sl]
            xb = xs.astype(BF16)
            rg = jax.nn.sigmoid(jnp.dot(xb, wa_ref[d, n], preferred_element_type=F32) + ba_ref[d:d + 1, sl])
            ig = jax.nn.sigmoid(jnp.dot(xb, wx_ref[d, n], preferred_element_type=F32) + bx_ref[d:d + 1, sl])
            log_a = (-LRU_C * rg) * sp[d:d + 1, sl]
            a = jnp.exp(log_a)
            a_ref[d, :, sl] = a
            u_ref[d, :, sl] = jnp.sqrt(_one_minus_exp2(log_a, a)) * (ig * xs)

    gates(0, zf_ref, zfp_ref, zfn_ref, i)
    gates(1, zb_ref, zbp_ref, zbn_ref, n_t - 1 - i)

    def step(s, carry):
        hf, hb = carry
        sb = t - 1 - s
        hf = a_ref[0, pl.ds(s, 1), :] * hf + u_ref[0, pl.ds(s, 1), :]
        hb = a_ref[1, pl.ds(sb, 1), :] * hb + u_ref[1, pl.ds(sb, 1), :]
        hs_ref[0, pl.ds(s, 1), :] = hf
        hs_ref[1, pl.ds(sb, 1), :] = hb
        return hf, hb

    hf, hb = lax.fori_loop(0, t, step, (car_ref[0:1, :], car_ref[1:2, :]), unroll=8)
    car_ref[0:1, :] = hf
    car_ref[1:2, :] = hb
    of_ref[...] = hs_ref[0].astype(of_ref.dtype)
    ob_ref[...] = hs_ref[1].astype(ob_ref.dtype)
    last_ref[0:1, :] = hf
    last_ref[1:2, :] = hb


def _rglru_call(z, col_rx, cw, cb, wa, ba, wx, bx, lam, h0):
    b, l, _ = z.shape
    c = cw.shape[1]
    n_blk = c // RNN_BLOCK
    t = _largest_tile(l, 512, HALO)
    n_t = l // t
    hb_per = t // HALO
    n_h = l // HALO

    def main(rev):
        return lambda bb, i: (bb, (n_t - 1 - i) if rev else i, col_rx)

    def prev(rev):
        return lambda bb, i: (bb, jnp.maximum(((n_t - 1 - i) if rev else i) * hb_per - 1, 0), col_rx)

    def nxt(rev):
        return lambda bb, i: (bb, jnp.minimum((((n_t - 1 - i) if rev else i) + 1) * hb_per, n_h - 1), col_rx)

    const2 = lambda bb, i: (0, 0)
    const4 = lambda bb, i: (0, 0, 0, 0)
    in_specs = []
    for rev in (False, True):
        in_specs += [pl.BlockSpec((None, t, c), main(rev)),
                     pl.BlockSpec((None, HALO, c), prev(rev)),
                     pl.BlockSpec((None, HALO, c), nxt(rev))]
    in_specs += [
        pl.BlockSpec((RNN_CONV, c), const2),
        pl.BlockSpec((1, c), const2),
        pl.BlockSpec((2, n_blk, RNN_BLOCK, RNN_BLOCK), const4),
        pl.BlockSpec((2, c), const2),
        pl.BlockSpec((2, n_blk, RNN_BLOCK, RNN_BLOCK), const4),
        pl.BlockSpec((2, c), const2),
        pl.BlockSpec((2, c), const2),
        pl.BlockSpec((None, 2, c), lambda bb, i: (bb, 0, 0)),
    ]
    return pl.pallas_call(
        functools.partial(_rglru_kernel, t=t, n_t=n_t, n_blk=n_blk),
        grid=(b, n_t),
        in_specs=in_specs,
        out_specs=[
            pl.BlockSpec((None, t, c), lambda bb, i: (bb, i, 0)),
            pl.BlockSpec((None, t, c), lambda bb, i: (bb, n_t - 1 - i, 0)),
            pl.BlockSpec((None, 2, c), lambda bb, i: (bb, 0, 0)),
        ],
        out_shape=[
            jax.ShapeDtypeStruct((b, l, c), BF16),
            jax.ShapeDtypeStruct((b, l, c), BF16),
            jax.ShapeDtypeStruct((b, 2, c), F32),
        ],
        scratch_shapes=[
            pltpu.VMEM((t + 16, c), F32),
            pltpu.VMEM((t, c), F32),
            pltpu.VMEM((2, t, c), F32),
            pltpu.VMEM((2, t, c), F32),
            pltpu.VMEM((2, t, c), F32),
            pltpu.VMEM((2, c), F32),
        ],
        compiler_params=_cparams(("parallel", "arbitrary")),
        name="rglru",
    )(z, z, z, z, z, z, cw, cb.reshape(1, c), wa, ba, wx, bx, lam, h0)


def _rope(x, cos_t, sin_t):
    lane = lax.broadcasted_iota(jnp.int32, x.shape, 1)
    partner = jnp.where((lane & 32) == 0, pltpu.roll(x, LANES - 32, 1), pltpu.roll(x, 32, 1))
    return x * cos_t + partner * sin_t


def _attn_kernel(sink_ref, q_ref, *rest, tq, n_kv, seq, local, between=(lambda: None, lambda: None)):
    if local:
        (km_ref, kp_ref, kn_ref, vm_ref, vp_ref, vn_ref, kc_ref, vc_ref,
         cm_ref, cp_ref, cn_ref, sm_ref, sp_ref, sn_ref, o_ref) = rest
    else:
        kc_ref, vc_ref, o_ref = rest
    i = pl.program_id(1)
    q_scale = HEAD_DIM ** -0.5 * LOG2E
    nt = (((1,), (1,)), ((), ()))
    sub = WINDOW
    n_ctx = kc_ref.shape[0]

    if local:
        cos_q, sin_q = cm_ref[...], sm_ref[...]
        cos_k = jnp.concatenate([cp_ref[...], cos_q, cn_ref[...]], axis=0)
        sin_k = jnp.concatenate([sp_ref[...], sin_q, sn_ref[...]], axis=0)
        r = lax.broadcasted_iota(jnp.int32, (sub, 3 * sub + n_ctx), 0)
        c = lax.broadcasted_iota(jnp.int32, (sub, 3 * sub + n_ctx), 1)
        band = (c >= r) & (c <= r + 2 * WINDOW)
        is_ctx = c >= 3 * sub

    units = []
    for kh in range(n_kv):
        ks = slice(kh * HEAD_DIM, (kh + 1) * HEAD_DIM)
        kc = kc_ref[:, ks]
        vc = vc_ref[:, ks]
        heads = [kh * GROUP + g for g in range(GROUP)]
        qh = []
        for h in heads:
            x = q_ref[:, h * HEAD_DIM:(h + 1) * HEAD_DIM].astype(F32)
            if local:
                x = _rope(x, cos_q, sin_q)
            qh.append((x * q_scale).astype(BF16))
        if local:
            k_span = jnp.concatenate([kp_ref[:, ks], km_ref[:, ks], kn_ref[:, ks]], axis=0)
            k_span = _rope(k_span.astype(F32), cos_k, sin_k).astype(BF16)
            v_span = jnp.concatenate([vp_ref[:, ks], vm_ref[:, ks], vn_ref[:, ks]], axis=0)
        sink2 = jnp.concatenate([jnp.full((sub, 1), sink_ref[h] * LOG2E, F32) for h in heads], axis=0)

        for sb in range(tq // sub):
            rows = slice(sb * sub, (sb + 1) * sub)
            qs = jnp.concatenate([x[rows] for x in qh], axis=0)
            if local:
                keys = jnp.concatenate([k_span[sb * sub:(sb + 3) * sub], kc], axis=0)
                vals = jnp.concatenate([v_span[sb * sub:(sb + 3) * sub], vc], axis=0)
                kpos = i * tq + (sb - 1) * sub + c
                valid = is_ctx | (band & (kpos >= 0) & (kpos < seq))
                bias = jnp.concatenate([jnp.where(valid, 0.0, NEG_INF)] * GROUP, axis=0)
            else:
                keys, vals, bias = kc, vc, None
            units.append(dict(rows=rows, heads=heads, qs=qs, keys=keys, vals=vals, bias=bias, sink2=sink2))

    for u in units:
        s = lax.dot_general(u["qs"], u["keys"], nt, preferred_element_type=F32)
        u["s"] = s if u["bias"] is None else s + u["bias"]
    between[0]()
    for u in units:
        m = jnp.maximum(jnp.max(u["s"], axis=-1, keepdims=True), u["sink2"])
        p = jnp.exp2(u["s"] - m)
        u["den"] = jnp.sum(p, axis=-1, keepdims=True) + jnp.exp2(u["sink2"] - m)
        u["p"] = p.astype(BF16)
    between[1]()
    for u in units:
        o = jnp.dot(u["p"], u["vals"], preferred_element_type=F32) / u["den"]
        for g, h in enumerate(u["heads"]):
            o_ref[u["rows"], h * HEAD_DIM:(h + 1) * HEAD_DIM] = o[g * sub:(g + 1) * sub].astype(o_ref.dtype)


def _attn_operands(sink, z, zc, cols, cos_t, sin_t, local, tq):
    l = z.shape[1]
    n_ctx = zc.shape[1]
    n_heads = sink.shape[0]
    n_kv = n_heads // GROUP
    qw, kw = n_heads * HEAD_DIM, n_kv * HEAD_DIM
    col_q, col_k, col_v = cols
    assert tq % WINDOW == 0
    wb = tq // WINDOW
    n_w = l // WINDOW
    prev = lambda bb, i: jnp.maximum(i * wb - 1, 0)
    nxt = lambda bb, i: jnp.minimum((i + 1) * wb, n_w - 1)
    in_specs = [
        pl.BlockSpec(memory_space=pltpu.SMEM),
        pl.BlockSpec((None, tq, qw), lambda bb, i: (bb, i, col_q)),
    ]
    args = [sink, z]
    if local:
        for col in (col_k, col_v):
            in_specs += [
                pl.BlockSpec((None, tq, kw), lambda bb, i, col=col: (bb, i, col)),
                pl.BlockSpec((None, WINDOW, kw), lambda bb, i, col=col: (bb, prev(bb, i), col)),
                pl.BlockSpec((None, WINDOW, kw), lambda bb, i, col=col: (bb, nxt(bb, i), col)),
            ]
            args += [z, z, z]
    in_specs += [
        pl.BlockSpec((None, n_ctx, kw), lambda bb, i: (bb, 0, col_k)),
        pl.BlockSpec((None, n_ctx, kw), lambda bb, i: (bb, 0, col_v)),
    ]
    args += [zc, zc]
    if local:
        for tab in (cos_t, sin_t):
            in_specs += [
                pl.BlockSpec((tq, HEAD_DIM), lambda bb, i: (i, 0)),
                pl.BlockSpec((WINDOW, HEAD_DIM), lambda bb, i: (prev(bb, i), 0)),
                pl.BlockSpec((WINDOW, HEAD_DIM), lambda bb, i: (nxt(bb, i), 0)),
            ]
            args += [tab, tab, tab]
    return in_specs, args


def _merge_kernel(*refs, n_attn, tm, n_t, d, n_kv, seq, local):
    attn_refs = refs[:n_attn]
    (h_ref, gate_ref, hf_ref, hb_ref, rg_ref, sb_ref, cg_ref, cgp_ref, cgn_ref, sx_ref, sxp_ref, sxn_ref,
     g_ref, bm_ref, scw_ref, wb_ref, wo_ref, o_ref, ppad_ref, att_ref) = refs[n_attn:]
    i = pl.program_id(1)
    ya = (hf_ref[...].astype(F32) + hb_ref[...].astype(F32)) * _gelu_tanh(rg_ref[...].astype(F32))

    _fill_padded(ppad_ref,
                 cg_ref[...].astype(F32) * sx_ref[...].astype(F32),
                 (cgp_ref[...].astype(F32) * sxp_ref[...].astype(F32))[8:16],
                 (cgn_ref[...].astype(F32) * sxn_ref[...].astype(F32))[0:8],
                 i > 0, i < n_t - 1, tm)
    conv = None
    for k in range(SC_CONV):
        off = 8 - SC_CONV_LEFT + k
        term = scw_ref[k:k + 1, :] * ppad_ref[off:off + tm, :]
        conv = term if conv is None else conv + term
    yb = sb_ref[...].astype(F32) * conv

    def lift(br, y):
        gates = jax.nn.sigmoid(g_ref[:, br * d:(br + 1) * d].astype(F32) + bm_ref[br:br + 1, :])
        return gates * jnp.dot(y, wb_ref[br], preferred_element_type=F32)

    lifted = []
    _attn_kernel(*attn_refs, att_ref, tq=tm, n_kv=n_kv, seq=seq, local=local,
                 between=(lambda: lifted.append(lift(0, ya.astype(BF16))),
                          lambda: lifted.append(lift(1, yb.astype(BF16)))))
    merged = lifted[0] + lifted[1] + lift(2, att_ref[...])
    y = jnp.dot(merged.astype(BF16), wo_ref[...], preferred_element_type=F32)
    o_ref[...] = h_ref[...] + gate_ref[...] * y


def _merge_call(h, gate, hf, hb, z, zc, cols, attn_cols, sink, cos_t, sin_t, local, b_merge, sc_w,
                w_branch, w_out, layer):
    b, l, d = h.shape
    bw = hf.shape[2]
    col_g, col_rg, col_sb, col_cg, col_sx = cols
    tm = _largest_tile(l, 256, WINDOW)
    n_t = l // tm
    attn_specs, attn_args = _attn_operands(sink, z, zc, attn_cols, cos_t, sin_t, local, tm)
    hb_per = tm // HALO
    n_h = l // HALO
    per_batch = gate.shape[0] == b
    tile = lambda col: (lambda bb, i: (bb, i, col))
    prev = lambda col: (lambda bb, i: (bb, jnp.maximum(i * hb_per - 1, 0), col))
    nxt = lambda col: (lambda bb, i: (bb, jnp.minimum((i + 1) * hb_per, n_h - 1), col))
    once = pl.Buffered(1)
    in_specs = [
        pl.BlockSpec((None, tm, d), tile(0)),
        pl.BlockSpec((None, 1, d), lambda bb, i: (bb if per_batch else 0, 0, 0)),
        pl.BlockSpec((None, tm, bw), tile(0)),
        pl.BlockSpec((None, tm, bw), tile(0)),
        pl.BlockSpec((None, tm, bw), tile(col_rg)),
        pl.BlockSpec((None, tm, bw), tile(col_sb)),
        pl.BlockSpec((None, tm, bw), tile(col_cg)),
        pl.BlockSpec((None, HALO, bw), prev(col_cg)),
        pl.BlockSpec((None, HALO, bw), nxt(col_cg)),
        pl.BlockSpec((None, tm, bw), tile(col_sx)),
        pl.BlockSpec((None, HALO, bw), prev(col_sx)),
        pl.BlockSpec((None, HALO, bw), nxt(col_sx)),
        pl.BlockSpec((None, tm, N_BRANCH * d), tile(col_g)),
        pl.BlockSpec((N_BRANCH, d), lambda bb, i: (0, 0)),
        pl.BlockSpec((SC_CONV, bw), lambda bb, i: (0, 0)),
        pl.BlockSpec((None, N_BRANCH, bw, d), lambda bb, i: (layer, 0, 0, 0), pipeline_mode=once),
        pl.BlockSpec((None, d, d), lambda bb, i: (layer, 0, 0), pipeline_mode=once),
    ]
    return pl.pallas_call(
        functools.partial(_merge_kernel, n_attn=len(attn_specs), tm=tm, n_t=n_t, d=d,
                          n_kv=sink.shape[0] // GROUP, seq=l, local=local),
        grid=(b, n_t),
        in_specs=attn_specs + in_specs,
        out_specs=pl.BlockSpec((None, tm, d), tile(0)),
        out_shape=jax.ShapeDtypeStruct((b, l, d), F32),
        scratch_shapes=[pltpu.VMEM((tm + 16, bw), F32), pltpu.VMEM((tm, bw), BF16)],
        compiler_params=_cparams(("parallel", "parallel")),
        name="merge_local" if local else "merge_ctx",
    )(*attn_args, h, gate, hf, hb, z, z, z, z, z, z, z, z, z, b_merge, sc_w, w_branch, w_out)


def _prep_ffn(w13, w2):
    f = w2.shape[1]
    tf = min(FFN_CHUNK, f // LANES * LANES)
    fm = f // tf * tf
    tails = None
    if fm < f:
        tails = (w13[:, :, fm:f].astype(BF16), w13[:, :, f + fm:].astype(BF16), w2[:, fm:].astype(BF16))
    return w13.astype(BF16), w13[:, :, f:f + fm].astype(BF16), w2.astype(BF16), tails


def _cast_kernel(w_ref, o_ref):
    o_ref[...] = w_ref[...].astype(o_ref.dtype)


def _prep_w_in(w_in, d):
    depth, _, nc = w_in.shape
    n_g = N_BRANCH * d
    tn = _largest_tile(math.gcd(n_g, nc - n_g), 512, LANES)
    n_blk, g_blk = nc // tn, n_g // tn
    return pl.pallas_call(
        _cast_kernel,
        grid=(depth, n_blk),
        in_specs=[pl.BlockSpec((None, d, tn), lambda l, j: (l, 0, (j + n_blk - g_blk) % n_blk))],
        out_specs=pl.BlockSpec((None, d, tn), lambda l, j: (l, 0, j)),
        out_shape=jax.ShapeDtypeStruct(w_in.shape, BF16),
        compiler_params=_cparams(("parallel", "parallel")),
        name="cast_w_in",
    )(w_in)


def _rope_tables(l):
    pos = jnp.arange(l)
    row = (pos // GRID_W).astype(F32)
    col = (pos % GRID_W).astype(F32)
    half = HEAD_DIM // 2
    inv = ROPE_BASE ** (-jnp.arange(0, half, 2, dtype=F32) / half)
    ar, ac = row[:, None] * inv, col[:, None] * inv
    cos_t = jnp.concatenate([jnp.cos(ar), jnp.cos(ar), jnp.cos(ac), jnp.cos(ac)], axis=-1)
    sin_t = jnp.concatenate([-jnp.sin(ar), jnp.sin(ar), -jnp.sin(ac), jnp.sin(ac)], axis=-1)
    return cos_t, sin_t


def kernel(x, c, ctx, c_ctx, ada_w, ada_b, norm_g, ffn1_w13, ffn1_w2, w_in, b_merge, rnn_conv_w,
           rnn_conv_b, lru_w_a, lru_b_a, lru_w_x, lru_b_x, lru_lambda, sc_conv_w, attn_sink, w_branch,
           w_out, ffn2_w13, ffn2_w2, final_norm_g):
    b, l, d = x.shape
    n_ctx = ctx.shape[1]
    depth = ada_w.shape[0]
    bw = w_branch.shape[2]
    n_heads = attn_sink.shape[1]
    kw = (n_heads // GROUP) * HEAD_DIM
    assert bw == n_heads * HEAD_DIM and w_in.shape[2] == 6 * bw + 2 * kw + N_BRANCH * d
    assert (N_BRANCH * d) % bw == 0 and (N_BRANCH * d + 6 * bw) % kw == 0 and b + 1 <= 8
    g_blocks = N_BRANCH * d // bw
    col_rx, col_rg, col_sb, col_cg, col_sx, col_q = (g_blocks + n for n in range(6))
    col_k = (N_BRANCH * d + 6 * bw) // kw
    col_v = col_k + 1

    cvec = jnp.zeros((8, d), F32).at[:b].set(c).at[b].set(c_ctx)
    mod = _ada_call(cvec, ada_w, ada_b).reshape(depth, 8, N_MOD, 1, d)
    cos_t, sin_t = _rope_tables(l)
    w_in_p = _prep_w_in(w_in, d)
    ffn_a = _prep_ffn(ffn1_w13, ffn1_w2)
    ffn_b = _prep_ffn(ffn2_w13, ffn2_w2)
    wbr, wo = w_branch.astype(BF16), w_out.astype(BF16)

    h = x.reshape(b * l, d)
    hc = ctx.reshape(b * n_ctx, d)
    for layer in range(depth):
        last = layer == depth - 1
        ml = mod[layer, :b]
        mc = mod[layer, b:b + 1]
        ng = norm_g[layer]
        wa, wx = lru_w_a[layer].astype(BF16), lru_w_x[layer].astype(BF16)

        h = _ffn_call(h, ng[0], ml[:, 0], ml[:, 1], ml[:, 2], ffn_a, layer)
        hc = _ffn_call(hc, ng[0], mc[:, 0], mc[:, 1], mc[:, 2], ffn_a, layer)

        z = _inproj_call(h, ng[1], ml[:, 3], ml[:, 4], w_in_p, layer).reshape(b, l, -1)
        zc = _inproj_call(hc, ng[1], mc[:, 3], mc[:, 4], w_in_p, layer).reshape(b, n_ctx, -1)
        lru = (rnn_conv_w[layer], rnn_conv_b[layer], wa, lru_b_a[layer], wx, lru_b_x[layer], lru_lambda[layer])
        hcf, hcb, h_last = _rglru_call(zc, col_rx, *lru, jnp.zeros((b, 2, bw), F32))
        hlf, hlb, _ = _rglru_call(z, col_rx, *lru, h_last)
        merge_cols = (0, col_rg, col_sb, col_cg, col_sx)
        attn_cols = (col_q, col_k, col_v)
        mix = (attn_sink[layer], cos_t, sin_t)
        mix_w = (b_merge[layer], sc_conv_w[layer], wbr, wo, layer)
        h = _merge_call(h.reshape(b, l, d), ml[:, 5], hlf, hlb, z, zc, merge_cols, attn_cols, *mix, True,
                        *mix_w).reshape(b * l, d)

        h = _ffn_call(h, ng[2], ml[:, 6], ml[:, 7], ml[:, 8], ffn_b, layer,
                      final_g=final_norm_g if last else None)
        if not last:
            hc = _merge_call(hc.reshape(b, n_ctx, d), mc[:, 5], hcf, hcb, zc, zc, merge_cols, attn_cols, *mix,
                             False, *mix_w).reshape(b * n_ctx, d)
            hc = _ffn_call(hc, ng[2], mc[:, 6], mc[:, 7], mc[:, 8], ffn_b, layer)
    return h.reshape(b, l, d)
```

```python
import functools
import math

import jax
import jax.numpy as jnp
from jax import lax
from jax.experimental import pallas as pl
from jax.experimental.pallas import tpu as pltpu

F32 = jnp.float32
BF16 = jnp.bfloat16

HEAD_DIM = 128
GROUP = 4
WINDOW = 128
GRID_W = 64
ROPE_BASE = 10000.0
RNN_BLOCK = 128
RNN_CONV = 4
RNN_CONV_LEFT = 2
SC_CONV = 3
SC_CONV_LEFT = 1
LRU_C = 8.0
N_BRANCH = 3
N_MOD = 9
EPS = 1e-6
NEG_INF = -1e30
LOG2E = 1.4426950408889634

LANES = 128
SUBLANES_F32 = 8
SUBLANES_BF16 = 16
HALO = SUBLANES_BF16
VMEM_LIMIT_BYTES = 56 * 1024 * 1024
FFN_CHUNK = 1024


def _cparams(semantics):
    return pltpu.CompilerParams(dimension_semantics=semantics, vmem_limit_bytes=VMEM_LIMIT_BYTES)


def _pow2_slices(n_steps, tm):
    n = 0
    while 2 * max(n, 1) <= n_steps and tm % (2 * max(n, 1) * SUBLANES_BF16) == 0:
        n = 2 * max(n, 1)
    return n


def _largest_tile(n, cap, quantum):
    best = None
    t = quantum
    while t <= min(n, cap):
        if n % t == 0:
            best = t
        t += quantum
    assert best is not None, (n, cap, quantum)
    return best


def _rmsnorm(x, g):
    return x * lax.rsqrt(jnp.mean(x * x, axis=-1, keepdims=True) + EPS) * g


def _norm_mod(x, g, shift, scale):
    return _rmsnorm(x, g) * (1.0 + scale) + shift


def _silu(x):
    return x * jax.nn.sigmoid(x)


def _gelu_tanh(x):
    return 0.5 * x * (1.0 + jnp.tanh(0.7978845608028654 * (x + 0.044715 * (x * x * x))))


def _softplus(x):
    return jnp.maximum(x, 0.0) + jnp.log1p(jnp.exp(-jnp.abs(x)))


def _one_minus_exp2(x, ex):
    kahan = (ex - 1.0) * x / jnp.log(ex)
    em1 = jnp.where(x < -1.0, ex - 1.0, jnp.where(ex == 1.0, x, kahan))
    return -em1 * (ex + 1.0)


def _ada_kernel(c_ref, w_ref, b_ref, o_ref):
    s = _silu(c_ref[...]).astype(BF16)
    o_ref[...] = jnp.dot(s, w_ref[...].astype(BF16), preferred_element_type=F32) + b_ref[...]


def _ada_call(cvec, ada_w, ada_b):
    depth, d, nm = ada_w.shape
    tn = _largest_tile(nm, 1024, LANES)
    return pl.pallas_call(
        _ada_kernel,
        grid=(depth, nm // tn),
        in_specs=[
            pl.BlockSpec((8, d), lambda l, j: (0, 0)),
            pl.BlockSpec((None, d, tn), lambda l, j: (l, 0, j)),
            pl.BlockSpec((None, 1, tn), lambda l, j: (l, 0, j)),
        ],
        out_specs=pl.BlockSpec((None, 8, tn), lambda l, j: (l, 0, j)),
        out_shape=jax.ShapeDtypeStruct((depth, 8, nm), F32),
        compiler_params=_cparams(("arbitrary", "arbitrary")),
        name="adaln",
    )(cvec, ada_w, ada_b.reshape(depth, 1, nm))


def _swiglu_chunk(u, wg_ref, wu_ref, w2_ref):
    gate = jnp.dot(u, wg_ref[...], preferred_element_type=F32)
    up = jnp.dot(u, wu_ref[...], preferred_element_type=F32)
    act = (_silu(gate) * up).astype(BF16)
    return jnp.dot(act, w2_ref[...], preferred_element_type=F32)


def _ffn_kernel(h_ref, g_ref, shift_ref, scale_ref, gate_ref, wg_ref, wu_ref, w2_ref, *rest, tail, final_norm):
    rest = list(rest)
    wgt_ref, wut_ref, w2t_ref = (rest.pop(0), rest.pop(0), rest.pop(0)) if tail else (None, None, None)
    fg_ref = rest.pop(0) if final_norm else None
    o_ref, u_ref = rest
    j = pl.program_id(1)

    @pl.when(j == 0)
    def _():
        u = _norm_mod(h_ref[...], g_ref[...], shift_ref[...], scale_ref[...]).astype(BF16)
        u_ref[...] = u
        o_ref[...] = _swiglu_chunk(u, wgt_ref, wut_ref, w2t_ref) if tail else jnp.zeros_like(o_ref)

    last = pl.num_programs(1) - 1

    @pl.when(j < last)
    def _():
        o_ref[...] += _swiglu_chunk(u_ref[...], wg_ref, wu_ref, w2_ref)

    @pl.when(j == last)
    def _():
        acc = o_ref[...] + _swiglu_chunk(u_ref[...], wg_ref, wu_ref, w2_ref)
        hn = h_ref[...] + (0.5 * gate_ref[...]) * acc
        if final_norm:
            hn = _rmsnorm(hn, fg_ref[...])
        o_ref[...] = hn


def _ffn_call(h, g, shift, scale, gate, weights, layer, final_g=None):
    wg, wu, w2, tails = weights
    m, d = h.shape
    nb = shift.shape[0]
    tf = min(FFN_CHUNK, wu.shape[2])
    n_chunks = wu.shape[2] // tf
    tm = _largest_tile(m // nb, 512, SUBLANES_F32)
    tpb = (m // nb) // tm
    row = lambda i, j: (i // tpb, 0, 0)
    const = lambda i, j: (0, 0)
    once = pl.Buffered(1)
    in_specs = [
        pl.BlockSpec((tm, d), lambda i, j: (i, 0)),
        pl.BlockSpec((1, d), const),
        pl.BlockSpec((None, 1, d), row),
        pl.BlockSpec((None, 1, d), row),
        pl.BlockSpec((None, 1, d), row),
        pl.BlockSpec((None, d, tf), lambda i, j: (layer, 0, j)),
        pl.BlockSpec((None, d, tf), lambda i, j: (layer, 0, j)),
        pl.BlockSpec((None, tf, d), lambda i, j: (layer, j, 0)),
    ]
    args = [h, g.reshape(1, d), shift, scale, gate, wg, wu, w2]
    if tails is not None:
        for w in tails:
            in_specs.append(pl.BlockSpec((None,) + w.shape[1:], lambda i, j: (layer, 0, 0), pipeline_mode=once))
            args.append(w)
    if final_g is not None:
        in_specs.append(pl.BlockSpec((1, d), const))
        args.append(final_g.reshape(1, d))
    return pl.pallas_call(
        functools.partial(_ffn_kernel, tail=tails is not None, final_norm=final_g is not None),
        grid=(m // tm, n_chunks),
        in_specs=in_specs,
        out_specs=pl.BlockSpec((tm, d), lambda i, j: (i, 0)),
        out_shape=jax.ShapeDtypeStruct((m, d), F32),
        scratch_shapes=[pltpu.VMEM((tm, d), BF16)],
        compiler_params=_cparams(("parallel", "arbitrary")),
        name="ffn",
    )(*args)


def _inproj_kernel(h_ref, g_ref, shift_ref, scale_ref, w_ref, o_ref, ua_ref, ub_ref, *, n_slices):
    i, j = pl.program_id(0), pl.program_id(1)
    tm = h_ref.shape[0]
    norm = lambda x: _norm_mod(x, g_ref[...], shift_ref[...], scale_ref[...]).astype(BF16)

    def step(cur_ref, nxt_ref):
        @pl.when((j == 0) & ((i == 0) | (n_slices == 0)))
        def _():
            cur_ref[...] = norm(h_ref[...])

        if n_slices:
            rows = tm // n_slices
            r0 = pl.multiple_of(jnp.clip(j - 1, 0, n_slices - 1) * rows, rows)
            nxt_ref[pl.ds(r0, rows), :] = norm(h_ref[pl.ds(r0, rows), :])
        o_ref[...] = jnp.dot(cur_ref[...], w_ref[...], preferred_element_type=F32).astype(o_ref.dtype)

    pl.when(i % 2 == 0)(lambda: step(ua_ref, ub_ref))
    pl.when(i % 2 == 1)(lambda: step(ub_ref, ua_ref))


def _inproj_call(h, g, shift, scale, w, layer):
    m, d = h.shape
    nb = shift.shape[0]
    nc = w.shape[2]
    tm = _largest_tile(m // nb, 1024, SUBLANES_BF16)
    tpb = (m // nb) // tm
    tn = _largest_tile(nc, 1280, LANES)
    n_i, n_j = m // tm, nc // tn
    n_slices = _pow2_slices(n_j - 1, tm)
    if n_slices:
        tile = lambda i, j: jnp.minimum(i + jnp.where(j > 0, 1, 0), n_i - 1)
    else:
        tile = lambda i, j: i
    row = lambda i, j: (tile(i, j) // tpb, 0, 0)
    return pl.pallas_call(
        functools.partial(_inproj_kernel, n_slices=n_slices),
        grid=(n_i, n_j),
        in_specs=[
            pl.BlockSpec((tm, d), lambda i, j: (tile(i, j), 0)),
            pl.BlockSpec((1, d), lambda i, j: (0, 0)),
            pl.BlockSpec((None, 1, d), row),
            pl.BlockSpec((None, 1, d), row),
            pl.BlockSpec((None, d, tn), lambda i, j: (layer, 0, j)),
        ],
        out_specs=pl.BlockSpec((tm, tn), lambda i, j: (i, j)),
        out_shape=jax.ShapeDtypeStruct((m, nc), BF16),
        scratch_shapes=[pltpu.VMEM((tm, d), BF16), pltpu.VMEM((tm, d), BF16)],
        compiler_params=_cparams(("arbitrary", "arbitrary")),
        name="in_proj",
    )(h, g.reshape(1, d), shift, scale, w)


def _fill_padded(pad_ref, main, prev_tail, next_head, has_prev, has_next, t):
    pad_ref[0:8, :] = jnp.where(has_prev, prev_tail, 0.0)
    pad_ref[8:8 + t, :] = main
    pad_ref[8 + t:16 + t, :] = jnp.where(has_next, next_head, 0.0)


def _segment_perm(t):
    r = lax.broadcasted_iota(jnp.int32, (t, t), 0)
    c = lax.broadcasted_iota(jnp.int32, (t, t), 1)
    return (c == (r % 8) * (t // 8) + r // 8).astype(BF16)


def _rglru_kernel(zf_ref, zfp_ref, zfn_ref, zb_ref, zbp_ref, zbn_ref, perm_ref, permt_ref, cw_ref, cb_ref,
                  wa_ref, ba_ref, wx_ref, bx_ref, lam_ref, h0_ref, of_ref, ob_ref, last_ref,
                  xp_ref, xa_ref, a_ref, u_ref, hs_ref, pc_ref, c8_ref, car_ref, *, t, n_t, n_blk):
    i = pl.program_id(1)
    seg = t // 8

    @pl.when(i == 0)
    def _():
        car_ref[...] = h0_ref[...]

    sp = _softplus(-lam_ref[...])
    sub = lax.broadcasted_iota(jnp.int32, (8, 1), 0)

    def gates(d, z_ref, zp_ref, zn_ref, tile):
        xq = jnp.dot(perm_ref[...], z_ref[...], preferred_element_type=F32)
        prev = jnp.where(tile > 0, zp_ref[...].astype(F32), 0.0)
        nxt = jnp.where(tile < n_t - 1, zn_ref[...].astype(F32), 0.0)
        xp_ref[0:8, :] = jnp.where(sub == 0, prev[14:15], pltpu.roll(xq[(seg - 2) * 8:(seg - 1) * 8], 1, 0))
        xp_ref[8:16, :] = jnp.where(sub == 0, prev[15:16], pltpu.roll(xq[(seg - 1) * 8:seg * 8], 1, 0))
        xp_ref[16:16 + t, :] = xq
        xp_ref[16 + t:24 + t, :] = jnp.where(sub == 7, nxt[0:1], pltpu.roll(xq[0:8], 7, 0))
        xa = cb_ref[...]
        for k in range(RNN_CONV):
            off = 8 * (2 - RNN_CONV_LEFT + k)
            xa = xa + cw_ref[k:k + 1, :] * xp_ref[off:off + t, :]
        xa_ref[...] = xa
        for n in range(n_blk):
            sl = slice(n * RNN_BLOCK, (n + 1) * RNN_BLOCK)
            xs = xa_ref[:, sl]
            xb = xs.astype(BF16)
            rg = jax.nn.sigmoid(jnp.dot(xb, wa_ref[d, n], preferred_element_type=F32) + ba_ref[d:d + 1, sl])
            ig = jax.nn.sigmoid(jnp.dot(xb, wx_ref[d, n], preferred_element_type=F32) + bx_ref[d:d + 1, sl])
            log_a = (-LRU_C * rg) * sp[d:d + 1, sl]
            a = jnp.exp(log_a)
            a_ref[d, :, sl] = a
            u_ref[d, :, sl] = jnp.sqrt(_one_minus_exp2(log_a, a)) * (ig * xs)

    gates(0, zf_ref, zfp_ref, zfn_ref, i)
    gates(1, zb_ref, zbp_ref, zbn_ref, n_t - 1 - i)

    def step(j, carry):
        hf, pf, hb, pb = carry
        rf = pl.multiple_of(j * 8, 8)
        rb = pl.multiple_of((seg - 1 - j) * 8, 8)
        af, ab = a_ref[0, pl.ds(rf, 8), :], a_ref[1, pl.ds(rb, 8), :]
        hf = af * hf + u_ref[0, pl.ds(rf, 8), :]
        hb = ab * hb + u_ref[1, pl.ds(rb, 8), :]
        pf, pb = pf * af, pb * ab
        hs_ref[0, pl.ds(rf, 8), :] = hf
        hs_ref[1, pl.ds(rb, 8), :] = hb
        pc_ref[0, pl.ds(rf, 8), :] = pf
        pc_ref[1, pl.ds(rb, 8), :] = pb
        return hf, pf, hb, pb

    zeros, ones = jnp.zeros((8, a_ref.shape[2]), F32), jnp.ones((8, a_ref.shape[2]), F32)
    lax.fori_loop(0, seg, step, (zeros, ones, zeros, ones), unroll=4)

    lf, qf = hs_ref[0, (seg - 1) * 8:seg * 8, :], pc_ref[0, (seg - 1) * 8:seg * 8, :]
    lb, qb = hs_ref[1, 0:8, :], pc_ref[1, 0:8, :]
    cf, cb = car_ref[0:1, :], car_ref[1:2, :]
    for s in range(8):
        c8_ref[0, s:s + 1, :] = cf
        cf = lf[s:s + 1] + qf[s:s + 1] * cf
        c8_ref[1, 7 - s:8 - s, :] = cb
        cb = lb[7 - s:8 - s] + qb[7 - s:8 - s] * cb
    car_ref[0:1, :] = cf
    car_ref[1:2, :] = cb
    last_ref[0:1, :] = cf
    last_ref[1:2, :] = cb

    for d, o_ref in ((0, of_ref), (1, ob_ref)):
        h = hs_ref[d] + pc_ref[d] * jnp.tile(c8_ref[d], (seg, 1))
        o_ref[...] = jnp.dot(permt_ref[...], h.astype(BF16), preferred_element_type=F32).astype(o_ref.dtype)


def _rglru_call(z, col_rx, cw, cb, wa, ba, wx, bx, lam, h0):
    b, l, _ = z.shape
    c = cw.shape[1]
    n_blk = c // RNN_BLOCK
    t = _largest_tile(l, 512, HALO)
    n_t = l // t
    hb_per = t // HALO
    n_h = l // HALO

    def main(rev):
        return lambda bb, i: (bb, (n_t - 1 - i) if rev else i, col_rx)

    def prev(rev):
        return lambda bb, i: (bb, jnp.maximum(((n_t - 1 - i) if rev else i) * hb_per - 1, 0), col_rx)

    def nxt(rev):
        return lambda bb, i: (bb, jnp.minimum((((n_t - 1 - i) if rev else i) + 1) * hb_per, n_h - 1), col_rx)

    const2 = lambda bb, i: (0, 0)
    const4 = lambda bb, i: (0, 0, 0, 0)
    in_specs = []
    for rev in (False, True):
        in_specs += [pl.BlockSpec((None, t, c), main(rev)),
                     pl.BlockSpec((None, HALO, c), prev(rev)),
                     pl.BlockSpec((None, HALO, c), nxt(rev))]
    perm = _segment_perm(t)
    in_specs += [
        pl.BlockSpec((t, t), const2, pipeline_mode=pl.Buffered(1)),
        pl.BlockSpec((t, t), const2, pipeline_mode=pl.Buffered(1)),
        pl.BlockSpec((RNN_CONV, c), const2),
        pl.BlockSpec((1, c), const2),
        pl.BlockSpec((2, n_blk, RNN_BLOCK, RNN_BLOCK), const4),
        pl.BlockSpec((2, c), const2),
        pl.BlockSpec((2, n_blk, RNN_BLOCK, RNN_BLOCK), const4),
        pl.BlockSpec((2, c), const2),
        pl.BlockSpec((2, c), const2),
        pl.BlockSpec((None, 2, c), lambda bb, i: (bb, 0, 0)),
    ]
    return pl.pallas_call(
        functools.partial(_rglru_kernel, t=t, n_t=n_t, n_blk=n_blk),
        grid=(b, n_t),
        in_specs=in_specs,
        out_specs=[
            pl.BlockSpec((None, t, c), lambda bb, i: (bb, i, 0)),
            pl.BlockSpec((None, t, c), lambda bb, i: (bb, n_t - 1 - i, 0)),
            pl.BlockSpec((None, 2, c), lambda bb, i: (bb, 0, 0)),
        ],
        out_shape=[
            jax.ShapeDtypeStruct((b, l, c), BF16),
            jax.ShapeDtypeStruct((b, l, c), BF16),
            jax.ShapeDtypeStruct((b, 2, c), F32),
        ],
        scratch_shapes=[
            pltpu.VMEM((t + 24, c), F32),
            pltpu.VMEM((t, c), F32),
            pltpu.VMEM((2, t, c), F32),
            pltpu.VMEM((2, t, c), F32),
            pltpu.VMEM((2, t, c), F32),
            pltpu.VMEM((2, t, c), F32),
            pltpu.VMEM((2, 8, c), F32),
            pltpu.VMEM((2, c), F32),
        ],
        compiler_params=_cparams(("parallel", "arbitrary")),
        name="rglru",
    )(z, z, z, z, z, z, perm, perm.T, cw, cb.reshape(1, c), wa, ba, wx, bx, lam, h0)


def _rope(x, cos_t, sin_t):
    lane = lax.broadcasted_iota(jnp.int32, x.shape, 1)
    partner = jnp.where((lane & 32) == 0, pltpu.roll(x, LANES - 32, 1), pltpu.roll(x, 32, 1))
    return x * cos_t + partner * sin_t


def _attn_kernel(sink_ref, q_ref, *rest, tq, n_kv, seq, local, between=(lambda: None, lambda: None)):
    if local:
        (km_ref, kp_ref, kn_ref, vm_ref, vp_ref, vn_ref, kc_ref, vc_ref,
         cm_ref, cp_ref, cn_ref, sm_ref, sp_ref, sn_ref, o_ref) = rest
    else:
        kc_ref, vc_ref, o_ref = rest
    i = pl.program_id(1)
    q_scale = HEAD_DIM ** -0.5 * LOG2E
    nt = (((1,), (1,)), ((), ()))
    sub = WINDOW
    n_ctx = kc_ref.shape[0]

    if local:
        cos_q, sin_q = cm_ref[...], sm_ref[...]
        cos_k = jnp.concatenate([cp_ref[...], cos_q, cn_ref[...]], axis=0)
        sin_k = jnp.concatenate([sp_ref[...], sin_q, sn_ref[...]], axis=0)
        r = lax.broadcasted_iota(jnp.int32, (sub, 3 * sub + n_ctx), 0)
        c = lax.broadcasted_iota(jnp.int32, (sub, 3 * sub + n_ctx), 1)
        band = (c >= r) & (c <= r + 2 * WINDOW)
        is_ctx = c >= 3 * sub

    units = []
    for kh in range(n_kv):
        ks = slice(kh * HEAD_DIM, (kh + 1) * HEAD_DIM)
        kc = kc_ref[:, ks]
        vc = vc_ref[:, ks]
        heads = [kh * GROUP + g for g in range(GROUP)]
        qh = []
        for h in heads:
            x = q_ref[:, h * HEAD_DIM:(h + 1) * HEAD_DIM].astype(F32)
            if local:
                x = _rope(x, cos_q, sin_q)
            qh.append((x * q_scale).astype(BF16))
        if local:
            k_span = jnp.concatenate([kp_ref[:, ks], km_ref[:, ks], kn_ref[:, ks]], axis=0)
            k_span = _rope(k_span.astype(F32), cos_k, sin_k).astype(BF16)
            v_span = jnp.concatenate([vp_ref[:, ks], vm_ref[:, ks], vn_ref[:, ks]], axis=0)
        sink2 = jnp.concatenate([jnp.full((sub, 1), sink_ref[h] * LOG2E, F32) for h in heads], axis=0)

        for sb in range(tq // sub):
            rows = slice(sb * sub, (sb + 1) * sub)
            qs = jnp.concatenate([x[rows] for x in qh], axis=0)
            if local:
                keys = jnp.concatenate([k_span[sb * sub:(sb + 3) * sub], kc], axis=0)
                vals = jnp.concatenate([v_span[sb * sub:(sb + 3) * sub], vc], axis=0)
                kpos = i * tq + (sb - 1) * sub + c
                valid = is_ctx | (band & (kpos >= 0) & (kpos < seq))
                bias = jnp.concatenate([jnp.where(valid, 0.0, NEG_INF)] * GROUP, axis=0)
            else:
                keys, vals, bias = kc, vc, None
            units.append(dict(rows=rows, heads=heads, qs=qs, keys=keys, vals=vals, bias=bias, sink2=sink2))

    for u in units:
        s = lax.dot_general(u["qs"], u["keys"], nt, preferred_element_type=F32)
        u["s"] = s if u["bias"] is None else s + u["bias"]
    between[0]()
    for u in units:
        m = jnp.maximum(jnp.max(u["s"], axis=-1, keepdims=True), u["sink2"])
        p = jnp.exp2(u["s"] - m)
        u["den"] = jnp.sum(p, axis=-1, keepdims=True) + jnp.exp2(u["sink2"] - m)
        u["p"] = p.astype(BF16)
    between[1]()
    for u in units:
        o = jnp.dot(u["p"], u["vals"], preferred_element_type=F32) / u["den"]
        for g, h in enumerate(u["heads"]):
            o_ref[u["rows"], h * HEAD_DIM:(h + 1) * HEAD_DIM] = o[g * sub:(g + 1) * sub].astype(o_ref.dtype)


def _attn_operands(sink, z, zc, cols, cos_t, sin_t, local, tq):
    l = z.shape[1]
    n_ctx = zc.shape[1]
    n_heads = sink.shape[0]
    n_kv = n_heads // GROUP
    qw, kw = n_heads * HEAD_DIM, n_kv * HEAD_DIM
    col_q, col_k, col_v = cols
    assert tq % WINDOW == 0
    wb = tq // WINDOW
    n_w = l // WINDOW
    prev = lambda bb, i: jnp.maximum(i * wb - 1, 0)
    nxt = lambda bb, i: jnp.minimum((i + 1) * wb, n_w - 1)
    in_specs = [
        pl.BlockSpec(memory_space=pltpu.SMEM),
        pl.BlockSpec((None, tq, qw), lambda bb, i: (bb, i, col_q)),
    ]
    args = [sink, z]
    if local:
        for col in (col_k, col_v):
            in_specs += [
                pl.BlockSpec((None, tq, kw), lambda bb, i, col=col: (bb, i, col)),
                pl.BlockSpec((None, WINDOW, kw), lambda bb, i, col=col: (bb, prev(bb, i), col)),
                pl.BlockSpec((None, WINDOW, kw), lambda bb, i, col=col: (bb, nxt(bb, i), col)),
            ]
            args += [z, z, z]
    in_specs += [
        pl.BlockSpec((None, n_ctx, kw), lambda bb, i: (bb, 0, col_k)),
        pl.BlockSpec((None, n_ctx, kw), lambda bb, i: (bb, 0, col_v)),
    ]
    args += [zc, zc]
    if local:
        for tab in (cos_t, sin_t):
            in_specs += [
                pl.BlockSpec((tq, HEAD_DIM), lambda bb, i: (i, 0)),
                pl.BlockSpec((WINDOW, HEAD_DIM), lambda bb, i: (prev(bb, i), 0)),
                pl.BlockSpec((WINDOW, HEAD_DIM), lambda bb, i: (nxt(bb, i), 0)),
            ]
            args += [tab, tab, tab]
    return in_specs, args


def _merge_kernel(*refs, n_attn, tm, n_t, d, n_kv, seq, local):
    attn_refs = refs[:n_attn]
    (h_ref, gate_ref, hf_ref, hb_ref, rg_ref, sb_ref, cg_ref, cgp_ref, cgn_ref, sx_ref, sxp_ref, sxn_ref,
     g_ref, bm_ref, scw_ref, wb_ref, wo_ref, o_ref, ppad_ref, att_ref) = refs[n_attn:]
    i = pl.program_id(1)
    ya = (hf_ref[...].astype(F32) + hb_ref[...].astype(F32)) * _gelu_tanh(rg_ref[...].astype(F32))

    _fill_padded(ppad_ref,
                 cg_ref[...].astype(F32) * sx_ref[...].astype(F32),
                 (cgp_ref[...].astype(F32) * sxp_ref[...].astype(F32))[8:16],
                 (cgn_ref[...].astype(F32) * sxn_ref[...].astype(F32))[0:8],
                 i > 0, i < n_t - 1, tm)
    conv = None
    for k in range(SC_CONV):
        off = 8 - SC_CONV_LEFT + k
        term = scw_ref[k:k + 1, :] * ppad_ref[off:off + tm, :]
        conv = term if conv is None else conv + term
    yb = sb_ref[...].astype(F32) * conv

    def lift(br, y):
        gates = jax.nn.sigmoid(g_ref[:, br * d:(br + 1) * d].astype(F32) + bm_ref[br:br + 1, :])
        return gates * jnp.dot(y, wb_ref[br], preferred_element_type=F32)

    lifted = []
    _attn_kernel(*attn_refs, att_ref, tq=tm, n_kv=n_kv, seq=seq, local=local,
                 between=(lambda: lifted.append(lift(0, ya.astype(BF16))),
                          lambda: lifted.append(lift(1, yb.astype(BF16)))))
    merged = lifted[0] + lifted[1] + lift(2, att_ref[...])
    y = jnp.dot(merged.astype(BF16), wo_ref[...], preferred_element_type=F32)
    o_ref[...] = h_ref[...] + gate_ref[...] * y


def _merge_call(h, gate, hf, hb, z, zc, cols, attn_cols, sink, cos_t, sin_t, local, b_merge, sc_w,
                w_branch, w_out, layer):
    b, l, d = h.shape
    bw = hf.shape[2]
    col_g, col_rg, col_sb, col_cg, col_sx = cols
    tm = _largest_tile(l, 256, WINDOW)
    n_t = l // tm
    attn_specs, attn_args = _attn_operands(sink, z, zc, attn_cols, cos_t, sin_t, local, tm)
    hb_per = tm // HALO
    n_h = l // HALO
    per_batch = gate.shape[0] == b
    tile = lambda col: (lambda bb, i: (bb, i, col))
    prev = lambda col: (lambda bb, i: (bb, jnp.maximum(i * hb_per - 1, 0), col))
    nxt = lambda col: (lambda bb, i: (bb, jnp.minimum((i + 1) * hb_per, n_h - 1), col))
    once = pl.Buffered(1)
    in_specs = [
        pl.BlockSpec((None, tm, d), tile(0)),
        pl.BlockSpec((None, 1, d), lambda bb, i: (bb if per_batch else 0, 0, 0)),
        pl.BlockSpec((None, tm, bw), tile(0)),
        pl.BlockSpec((None, tm, bw), tile(0)),
        pl.BlockSpec((None, tm, bw), tile(col_rg)),
        pl.BlockSpec((None, tm, bw), tile(col_sb)),
        pl.BlockSpec((None, tm, bw), tile(col_cg)),
        pl.BlockSpec((None, HALO, bw), prev(col_cg)),
        pl.BlockSpec((None, HALO, bw), nxt(col_cg)),
        pl.BlockSpec((None, tm, bw), tile(col_sx)),
        pl.BlockSpec((None, HALO, bw), prev(col_sx)),
        pl.BlockSpec((None, HALO, bw), nxt(col_sx)),
        pl.BlockSpec((None, tm, N_BRANCH * d), tile(col_g)),
        pl.BlockSpec((N_BRANCH, d), lambda bb, i: (0, 0)),
        pl.BlockSpec((SC_CONV, bw), lambda bb, i: (0, 0)),
        pl.BlockSpec((None, N_BRANCH, bw, d), lambda bb, i: (layer, 0, 0, 0), pipeline_mode=once),
        pl.BlockSpec((None, d, d), lambda bb, i: (layer, 0, 0), pipeline_mode=once),
    ]
    return pl.pallas_call(
        functools.partial(_merge_kernel, n_attn=len(attn_specs), tm=tm, n_t=n_t, d=d,
                          n_kv=sink.shape[0] // GROUP, seq=l, local=local),
        grid=(b, n_t),
        in_specs=attn_specs + in_specs,
        out_specs=pl.BlockSpec((None, tm, d), tile(0)),
        out_shape=jax.ShapeDtypeStruct((b, l, d), F32),
        scratch_shapes=[pltpu.VMEM((tm + 16, bw), F32), pltpu.VMEM((tm, bw), BF16)],
        compiler_params=_cparams(("parallel", "parallel")),
        name="merge_local" if local else "merge_ctx",
    )(*attn_args, h, gate, hf, hb, z, z, z, z, z, z, z, z, z, b_merge, sc_w, w_branch, w_out)


def _prep_ffn(w13, w2):
    f = w2.shape[1]
    tf = min(FFN_CHUNK, f // LANES * LANES)
    fm = f // tf * tf
    tails = None
    if fm < f:
        tails = (w13[:, :, fm:f].astype(BF16), w13[:, :, f + fm:].astype(BF16), w2[:, fm:].astype(BF16))
    return w13.astype(BF16), w13[:, :, f:f + fm].astype(BF16), w2.astype(BF16), tails


def _cast_kernel(w_ref, o_ref):
    o_ref[...] = w_ref[...].astype(o_ref.dtype)


def _prep_w_in(w_in, d):
    depth, _, nc = w_in.shape
    n_g = N_BRANCH * d
    tn = _largest_tile(math.gcd(n_g, nc - n_g), 512, LANES)
    n_blk, g_blk = nc // tn, n_g // tn
    return pl.pallas_call(
        _cast_kernel,
        grid=(depth, n_blk),
        in_specs=[pl.BlockSpec((None, d, tn), lambda l, j: (l, 0, (j + n_blk - g_blk) % n_blk))],
        out_specs=pl.BlockSpec((None, d, tn), lambda l, j: (l, 0, j)),
        out_shape=jax.ShapeDtypeStruct(w_in.shape, BF16),
        compiler_params=_cparams(("parallel", "parallel")),
        name="cast_w_in",
    )(w_in)


def _rope_tables(l):
    pos = jnp.arange(l)
    row = (pos // GRID_W).astype(F32)
    col = (pos % GRID_W).astype(F32)
    half = HEAD_DIM // 2
    inv = ROPE_BASE ** (-jnp.arange(0, half, 2, dtype=F32) / half)
    ar, ac = row[:, None] * inv, col[:, None] * inv
    cos_t = jnp.concatenate([jnp.cos(ar), jnp.cos(ar), jnp.cos(ac), jnp.cos(ac)], axis=-1)
    sin_t = jnp.concatenate([-jnp.sin(ar), jnp.sin(ar), -jnp.sin(ac), jnp.sin(ac)], axis=-1)
    return cos_t, sin_t


def kernel(x, c, ctx, c_ctx, ada_w, ada_b, norm_g, ffn1_w13, ffn1_w2, w_in, b_merge, rnn_conv_w,
           rnn_conv_b, lru_w_a, lru_b_a, lru_w_x, lru_b_x, lru_lambda, sc_conv_w, attn_sink, w_branch,
           w_out, ffn2_w13, ffn2_w2, final_norm_g):
    b, l, d = x.shape
    n_ctx = ctx.shape[1]
    depth = ada_w.shape[0]
    bw = w_branch.shape[2]
    n_heads = attn_sink.shape[1]
    kw = (n_heads // GROUP) * HEAD_DIM
    assert bw == n_heads * HEAD_DIM and w_in.shape[2] == 6 * bw + 2 * kw + N_BRANCH * d
    assert (N_BRANCH * d) % bw == 0 and (N_BRANCH * d + 6 * bw) % kw == 0 and b + 1 <= 8
    g_blocks = N_BRANCH * d // bw
    col_rx, col_rg, col_sb, col_cg, col_sx, col_q = (g_blocks + n for n in range(6))
    col_k = (N_BRANCH * d + 6 * bw) // kw
    col_v = col_k + 1

    cvec = jnp.zeros((8, d), F32).at[:b].set(c).at[b].set(c_ctx)
    mod = _ada_call(cvec, ada_w, ada_b).reshape(depth, 8, N_MOD, 1, d)
    cos_t, sin_t = _rope_tables(l)
    w_in_p = _prep_w_in(w_in, d)
    ffn_a = _prep_ffn(ffn1_w13, ffn1_w2)
    ffn_b = _prep_ffn(ffn2_w13, ffn2_w2)
    wbr, wo = w_branch.astype(BF16), w_out.astype(BF16)

    h = x.reshape(b * l, d)
    hc = ctx.reshape(b * n_ctx, d)
    for layer in range(depth):
        last = layer == depth - 1
        ml = mod[layer, :b]
        mc = mod[layer, b:b + 1]
        ng = norm_g[layer]
        wa, wx = lru_w_a[layer].astype(BF16), lru_w_x[layer].astype(BF16)

        h = _ffn_call(h, ng[0], ml[:, 0], ml[:, 1], ml[:, 2], ffn_a, layer)
        hc = _ffn_call(hc, ng[0], mc[:, 0], mc[:, 1], mc[:, 2], ffn_a, layer)

        z = _inproj_call(h, ng[1], ml[:, 3], ml[:, 4], w_in_p, layer).reshape(b, l, -1)
        zc = _inproj_call(hc, ng[1], mc[:, 3], mc[:, 4], w_in_p, layer).reshape(b, n_ctx, -1)
        lru = (rnn_conv_w[layer], rnn_conv_b[layer], wa, lru_b_a[layer], wx, lru_b_x[layer], lru_lambda[layer])
        hcf, hcb, h_last = _rglru_call(zc, col_rx, *lru, jnp.zeros((b, 2, bw), F32))
        hlf, hlb, _ = _rglru_call(z, col_rx, *lru, h_last)
        merge_cols = (0, col_rg, col_sb, col_cg, col_sx)
        attn_cols = (col_q, col_k, col_v)
        mix = (attn_sink[layer], cos_t, sin_t)
        mix_w = (b_merge[layer], sc_conv_w[layer], wbr, wo, layer)
        h = _merge_call(h.reshape(b, l, d), ml[:, 5], hlf, hlb, z, zc, merge_cols, attn_cols, *mix, True,
                        *mix_w).reshape(b * l, d)

        h = _ffn_call(h, ng[2], ml[:, 6], ml[:, 7], ml[:, 8], ffn_b, layer,
                      final_g=final_norm_g if last else None)
        if not last:
            hc = _merge_call(hc.reshape(b, n_ctx, d), mc[:, 5], hcf, hcb, zc, zc, merge_cols, attn_cols, *mix,
                             False, *mix_w).reshape(b * n_ctx, d)
            hc = _ffn_call(hc, ng[2], mc[:, 6], mc[:, 7], mc[:, 8], ffn_b, layer)
    return h.reshape(b, l, d)
```

```python
import functools
import math

import jax
import jax.numpy as jnp
from jax import lax
from jax.experimental import pallas as pl
from jax.experimental.pallas import tpu as pltpu

F32 = jnp.float32
BF16 = jnp.bfloat16

HEAD_DIM = 128
GROUP = 4
WINDOW = 128
GRID_W = 64
ROPE_BASE = 10000.0
RNN_BLOCK = 128
RNN_CONV = 4
RNN_CONV_LEFT = 2
SC_CONV = 3
SC_CONV_LEFT = 1
LRU_C = 8.0
N_BRANCH = 3
N_MOD = 9
EPS = 1e-6
NEG_INF = -1e30
LOG2E = 1.4426950408889634

LANES = 128
SUBLANES_F32 = 8
SUBLANES_BF16 = 16
HALO = SUBLANES_BF16
VMEM_LIMIT_BYTES = 56 * 1024 * 1024
FFN_CHUNK = 512
FFN_TOKENS = 1024
FFN_VMEM_LIMIT_BYTES = 62 * 1024 * 1024


def _cparams(semantics, vmem_limit_bytes=VMEM_LIMIT_BYTES):
    return pltpu.CompilerParams(dimension_semantics=semantics, vmem_limit_bytes=vmem_limit_bytes)


def _pow2_slices(n_steps, tm):
    n = 0
    while 2 * max(n, 1) <= n_steps and tm % (2 * max(n, 1) * SUBLANES_BF16) == 0:
        n = 2 * max(n, 1)
    return n


def _largest_tile(n, cap, quantum):
    best = None
    t = quantum
    while t <= min(n, cap):
        if n % t == 0:
            best = t
        t += quantum
    assert best is not None, (n, cap, quantum)
    return best


def _rmsnorm(x, g):
    return x * lax.rsqrt(jnp.mean(x * x, axis=-1, keepdims=True) + EPS) * g


def _norm_mod(x, g, shift, scale):
    return _rmsnorm(x, g) * (1.0 + scale) + shift


def _silu(x):
    return x * jax.nn.sigmoid(x)


def _gelu_tanh(x):
    return 0.5 * x * (1.0 + jnp.tanh(0.7978845608028654 * (x + 0.044715 * (x * x * x))))


def _softplus(x):
    return jnp.maximum(x, 0.0) + jnp.log1p(jnp.exp(-jnp.abs(x)))


def _one_minus_exp2(x, ex):
    kahan = (ex - 1.0) * x / jnp.log(ex)
    em1 = jnp.where(x < -1.0, ex - 1.0, jnp.where(ex == 1.0, x, kahan))
    return -em1 * (ex + 1.0)


def _ada_kernel(c_ref, w_ref, b_ref, o_ref):
    s = _silu(c_ref[...]).astype(BF16)
    o_ref[...] = jnp.dot(s, w_ref[...].astype(BF16), preferred_element_type=F32) + b_ref[...]


def _ada_call(cvec, ada_w, ada_b):
    depth, d, nm = ada_w.shape
    tn = _largest_tile(nm, 1024, LANES)
    return pl.pallas_call(
        _ada_kernel,
        grid=(depth, nm // tn),
        in_specs=[
            pl.BlockSpec((8, d), lambda l, j: (0, 0)),
            pl.BlockSpec((None, d, tn), lambda l, j: (l, 0, j)),
            pl.BlockSpec((None, 1, tn), lambda l, j: (l, 0, j)),
        ],
        out_specs=pl.BlockSpec((None, 8, tn), lambda l, j: (l, 0, j)),
        out_shape=jax.ShapeDtypeStruct((depth, 8, nm), F32),
        compiler_params=_cparams(("arbitrary", "arbitrary")),
        name="adaln",
    )(cvec, ada_w, ada_b.reshape(depth, 1, nm))


def _swiglu_chunk(u, wg_ref, wu_ref, w2_ref):
    gate = jnp.dot(u, wg_ref[...], preferred_element_type=F32)
    up = jnp.dot(u, wu_ref[...], preferred_element_type=F32)
    act = (_silu(gate) * up).astype(BF16)
    return jnp.dot(act, w2_ref[...], preferred_element_type=F32)


def _ffn_kernel(h_ref, g_ref, shift_ref, scale_ref, gate_ref, wg_ref, wu_ref, w2_ref, *rest, tail, final_norm):
    rest = list(rest)
    wgt_ref, wut_ref, w2t_ref = (rest.pop(0), rest.pop(0), rest.pop(0)) if tail else (None, None, None)
    fg_ref = rest.pop(0) if final_norm else None
    o_ref, u_ref = rest
    j = pl.program_id(1)

    @pl.when(j == 0)
    def _():
        u = _norm_mod(h_ref[...], g_ref[...], shift_ref[...], scale_ref[...]).astype(BF16)
        u_ref[...] = u
        o_ref[...] = _swiglu_chunk(u, wgt_ref, wut_ref, w2t_ref) if tail else jnp.zeros_like(o_ref)

    last = pl.num_programs(1) - 1

    @pl.when(j < last)
    def _():
        o_ref[...] += _swiglu_chunk(u_ref[...], wg_ref, wu_ref, w2_ref)

    @pl.when(j == last)
    def _():
        acc = o_ref[...] + _swiglu_chunk(u_ref[...], wg_ref, wu_ref, w2_ref)
        hn = h_ref[...] + (0.5 * gate_ref[...]) * acc
        if final_norm:
            hn = _rmsnorm(hn, fg_ref[...])
        o_ref[...] = hn


def _ffn_call(h, g, shift, scale, gate, weights, layer, final_g=None):
    wg, wu, w2, tails = weights
    m, d = h.shape
    nb = shift.shape[0]
    tf = min(FFN_CHUNK, wu.shape[2])
    n_chunks = wu.shape[2] // tf
    tm = _largest_tile(m // nb, FFN_TOKENS, SUBLANES_F32)
    tpb = (m // nb) // tm
    row = lambda i, j: (i // tpb, 0, 0)
    const = lambda i, j: (0, 0)
    once = pl.Buffered(1)
    in_specs = [
        pl.BlockSpec((tm, d), lambda i, j: (i, 0)),
        pl.BlockSpec((1, d), const),
        pl.BlockSpec((None, 1, d), row),
        pl.BlockSpec((None, 1, d), row),
        pl.BlockSpec((None, 1, d), row),
        pl.BlockSpec((None, d, tf), lambda i, j: (layer, 0, j)),
        pl.BlockSpec((None, d, tf), lambda i, j: (layer, 0, j)),
        pl.BlockSpec((None, tf, d), lambda i, j: (layer, j, 0)),
    ]
    args = [h, g.reshape(1, d), shift, scale, gate, wg, wu, w2]
    if tails is not None:
        for w in tails:
            in_specs.append(pl.BlockSpec((None,) + w.shape[1:], lambda i, j: (layer, 0, 0), pipeline_mode=once))
            args.append(w)
    if final_g is not None:
        in_specs.append(pl.BlockSpec((1, d), const))
        args.append(final_g.reshape(1, d))
    return pl.pallas_call(
        functools.partial(_ffn_kernel, tail=tails is not None, final_norm=final_g is not None),
        grid=(m // tm, n_chunks),
        in_specs=in_specs,
        out_specs=pl.BlockSpec((tm, d), lambda i, j: (i, 0)),
        out_shape=jax.ShapeDtypeStruct((m, d), F32),
        scratch_shapes=[pltpu.VMEM((tm, d), BF16)],
        compiler_params=_cparams(("parallel", "arbitrary"), FFN_VMEM_LIMIT_BYTES),
        name="ffn",
    )(*args)


def _inproj_kernel(h_ref, g_ref, shift_ref, scale_ref, w_ref, o_ref, ua_ref, ub_ref, *, n_slices):
    i, j = pl.program_id(0), pl.program_id(1)
    tm = h_ref.shape[0]
    norm = lambda x: _norm_mod(x, g_ref[...], shift_ref[...], scale_ref[...]).astype(BF16)

    def step(cur_ref, nxt_ref):
        @pl.when((j == 0) & ((i == 0) | (n_slices == 0)))
        def _():
            cur_ref[...] = norm(h_ref[...])

        if n_slices:
            rows = tm // n_slices
            r0 = pl.multiple_of(jnp.clip(j - 1, 0, n_slices - 1) * rows, rows)
            nxt_ref[pl.ds(r0, rows), :] = norm(h_ref[pl.ds(r0, rows), :])
        o_ref[...] = jnp.dot(cur_ref[...], w_ref[...], preferred_element_type=F32).astype(o_ref.dtype)

    pl.when(i % 2 == 0)(lambda: step(ua_ref, ub_ref))
    pl.when(i % 2 == 1)(lambda: step(ub_ref, ua_ref))


def _inproj_call(h, g, shift, scale, w, layer):
    m, d = h.shape
    nb = shift.shape[0]
    nc = w.shape[2]
    tm = _largest_tile(m // nb, 1024, SUBLANES_BF16)
    tpb = (m // nb) // tm
    tn = _largest_tile(nc, 1280, LANES)
    n_i, n_j = m // tm, nc // tn
    n_slices = _pow2_slices(n_j - 1, tm)
    if n_slices:
        tile = lambda i, j: jnp.minimum(i + jnp.where(j > 0, 1, 0), n_i - 1)
    else:
        tile = lambda i, j: i
    row = lambda i, j: (tile(i, j) // tpb, 0, 0)
    return pl.pallas_call(
        functools.partial(_inproj_kernel, n_slices=n_slices),
        grid=(n_i, n_j),
        in_specs=[
            pl.BlockSpec((tm, d), lambda i, j: (tile(i, j), 0)),
            pl.BlockSpec((1, d), lambda i, j: (0, 0)),
            pl.BlockSpec((None, 1, d), row),
            pl.BlockSpec((None, 1, d), row),
            pl.BlockSpec((None, d, tn), lambda i, j: (layer, 0, j)),
        ],
        out_specs=pl.BlockSpec((tm, tn), lambda i, j: (i, j)),
        out_shape=jax.ShapeDtypeStruct((m, nc), BF16),
        scratch_shapes=[pltpu.VMEM((tm, d), BF16), pltpu.VMEM((tm, d), BF16)],
        compiler_params=_cparams(("arbitrary", "arbitrary")),
        name="in_proj",
    )(h, g.reshape(1, d), shift, scale, w)


def _fill_padded(pad_ref, main, prev_tail, next_head, has_prev, has_next, t):
    pad_ref[0:8, :] = jnp.where(has_prev, prev_tail, 0.0)
    pad_ref[8:8 + t, :] = main
    pad_ref[8 + t:16 + t, :] = jnp.where(has_next, next_head, 0.0)


def _segment_perm(t):
    r = lax.broadcasted_iota(jnp.int32, (t, t), 0)
    c = lax.broadcasted_iota(jnp.int32, (t, t), 1)
    return (c == (r % 8) * (t // 8) + r // 8).astype(BF16)


def _rglru_kernel(zf_ref, zfp_ref, zfn_ref, zb_ref, zbp_ref, zbn_ref, perm_ref, permt_ref, cw_ref, cb_ref,
                  wa_ref, ba_ref, wx_ref, bx_ref, lam_ref, h0_ref, of_ref, ob_ref, last_ref,
                  xp_ref, xa_ref, a_ref, u_ref, hs_ref, pc_ref, c8_ref, car_ref, *, t, n_t, n_blk):
    i = pl.program_id(1)
    seg = t // 8

    @pl.when(i == 0)
    def _():
        car_ref[...] = h0_ref[...]

    sp = _softplus(-lam_ref[...])
    sub = lax.broadcasted_iota(jnp.int32, (8, 1), 0)

    def gates(d, z_ref, zp_ref, zn_ref, tile):
        xq = jnp.dot(perm_ref[...], z_ref[...], preferred_element_type=F32)
        prev = jnp.where(tile > 0, zp_ref[...].astype(F32), 0.0)
        nxt = jnp.where(tile < n_t - 1, zn_ref[...].astype(F32), 0.0)
        xp_ref[0:8, :] = jnp.where(sub == 0, prev[14:15], pltpu.roll(xq[(seg - 2) * 8:(seg - 1) * 8], 1, 0))
        xp_ref[8:16, :] = jnp.where(sub == 0, prev[15:16], pltpu.roll(xq[(seg - 1) * 8:seg * 8], 1, 0))
        xp_ref[16:16 + t, :] = xq
        xp_ref[16 + t:24 + t, :] = jnp.where(sub == 7, nxt[0:1], pltpu.roll(xq[0:8], 7, 0))
        xa = cb_ref[...]
        for k in range(RNN_CONV):
            off = 8 * (2 - RNN_CONV_LEFT + k)
            xa = xa + cw_ref[k:k + 1, :] * xp_ref[off:off + t, :]
        xa_ref[...] = xa
        for n in range(n_blk):
            sl = slice(n * RNN_BLOCK, (n + 1) * RNN_BLOCK)
            xs = xa_ref[:, sl]
            xb = xs.astype(BF16)
            rg = jax.nn.sigmoid(jnp.dot(xb, wa_ref[d, n], preferred_element_type=F32) + ba_ref[d:d + 1, sl])
            ig = jax.nn.sigmoid(jnp.dot(xb, wx_ref[d, n], preferred_element_type=F32) + bx_ref[d:d + 1, sl])
            log_a = (-LRU_C * rg) * sp[d:d + 1, sl]
            a = jnp.exp(log_a)
            a_ref[d, :, sl] = a
            u_ref[d, :, sl] = jnp.sqrt(_one_minus_exp2(log_a, a)) * (ig * xs)

    gates(0, zf_ref, zfp_ref, zfn_ref, i)
    gates(1, zb_ref, zbp_ref, zbn_ref, n_t - 1 - i)

    def step(j, carry):
        hf, pf, hb, pb = carry
        rf = pl.multiple_of(j * 8, 8)
        rb = pl.multiple_of((seg - 1 - j) * 8, 8)
        af, ab = a_ref[0, pl.ds(rf, 8), :], a_ref[1, pl.ds(rb, 8), :]
        hf = af * hf + u_ref[0, pl.ds(rf, 8), :]
        hb = ab * hb + u_ref[1, pl.ds(rb, 8), :]
        pf, pb = pf * af, pb * ab
        hs_ref[0, pl.ds(rf, 8), :] = hf
        hs_ref[1, pl.ds(rb, 8), :] = hb
        pc_ref[0, pl.ds(rf, 8), :] = pf
        pc_ref[1, pl.ds(rb, 8), :] = pb
        return hf, pf, hb, pb

    zeros, ones = jnp.zeros((8, a_ref.shape[2]), F32), jnp.ones((8, a_ref.shape[2]), F32)
    lax.fori_loop(0, seg, step, (zeros, ones, zeros, ones), unroll=4)

    lf, qf = hs_ref[0, (seg - 1) * 8:seg * 8, :], pc_ref[0, (seg - 1) * 8:seg * 8, :]
    lb, qb = hs_ref[1, 0:8, :], pc_ref[1, 0:8, :]
    cf, cb = car_ref[0:1, :], car_ref[1:2, :]
    for s in range(8):
        c8_ref[0, s:s + 1, :] = cf
        cf = lf[s:s + 1] + qf[s:s + 1] * cf
        c8_ref[1, 7 - s:8 - s, :] = cb
        cb = lb[7 - s:8 - s] + qb[7 - s:8 - s] * cb
    car_ref[0:1, :] = cf
    car_ref[1:2, :] = cb
    last_ref[0:1, :] = cf
    last_ref[1:2, :] = cb

    for d, o_ref in ((0, of_ref), (1, ob_ref)):
        h = hs_ref[d] + pc_ref[d] * jnp.tile(c8_ref[d], (seg, 1))
        o_ref[...] = jnp.dot(permt_ref[...], h.astype(BF16), preferred_element_type=F32).astype(o_ref.dtype)


def _rglru_call(z, col_rx, cw, cb, wa, ba, wx, bx, lam, h0):
    b, l, _ = z.shape
    c = cw.shape[1]
    n_blk = c // RNN_BLOCK
    t = _largest_tile(l, 512, HALO)
    n_t = l // t
    hb_per = t // HALO
    n_h = l // HALO

    def main(rev):
        return lambda bb, i: (bb, (n_t - 1 - i) if rev else i, col_rx)

    def prev(rev):
        return lambda bb, i: (bb, jnp.maximum(((n_t - 1 - i) if rev else i) * hb_per - 1, 0), col_rx)

    def nxt(rev):
        return lambda bb, i: (bb, jnp.minimum((((n_t - 1 - i) if rev else i) + 1) * hb_per, n_h - 1), col_rx)

    const2 = lambda bb, i: (0, 0)
    const4 = lambda bb, i: (0, 0, 0, 0)
    in_specs = []
    for rev in (False, True):
        in_specs += [pl.BlockSpec((None, t, c), main(rev)),
                     pl.BlockSpec((None, HALO, c), prev(rev)),
                     pl.BlockSpec((None, HALO, c), nxt(rev))]
    perm = _segment_perm(t)
    in_specs += [
        pl.BlockSpec((t, t), const2, pipeline_mode=pl.Buffered(1)),
        pl.BlockSpec((t, t), const2, pipeline_mode=pl.Buffered(1)),
        pl.BlockSpec((RNN_CONV, c), const2),
        pl.BlockSpec((1, c), const2),
        pl.BlockSpec((2, n_blk, RNN_BLOCK, RNN_BLOCK), const4),
        pl.BlockSpec((2, c), const2),
        pl.BlockSpec((2, n_blk, RNN_BLOCK, RNN_BLOCK), const4),
        pl.BlockSpec((2, c), const2),
        pl.BlockSpec((2, c), const2),
        pl.BlockSpec((None, 2, c), lambda bb, i: (bb, 0, 0)),
    ]
    return pl.pallas_call(
        functools.partial(_rglru_kernel, t=t, n_t=n_t, n_blk=n_blk),
        grid=(b, n_t),
        in_specs=in_specs,
        out_specs=[
            pl.BlockSpec((None, t, c), lambda bb, i: (bb, i, 0)),
            pl.BlockSpec((None, t, c), lambda bb, i: (bb, n_t - 1 - i, 0)),
            pl.BlockSpec((None, 2, c), lambda bb, i: (bb, 0, 0)),
        ],
        out_shape=[
            jax.ShapeDtypeStruct((b, l, c), BF16),
            jax.ShapeDtypeStruct((b, l, c), BF16),
            jax.ShapeDtypeStruct((b, 2, c), F32),
        ],
        scratch_shapes=[
            pltpu.VMEM((t + 24, c), F32),
            pltpu.VMEM((t, c), F32),
            pltpu.VMEM((2, t, c), F32),
            pltpu.VMEM((2, t, c), F32),
            pltpu.VMEM((2, t, c), F32),
            pltpu.VMEM((2, t, c), F32),
            pltpu.VMEM((2, 8, c), F32),
            pltpu.VMEM((2, c), F32),
        ],
        compiler_params=_cparams(("parallel", "arbitrary")),
        name="rglru",
    )(z, z, z, z, z, z, perm, perm.T, cw, cb.reshape(1, c), wa, ba, wx, bx, lam, h0)


def _rope(x, cos_t, sin_t):
    lane = lax.broadcasted_iota(jnp.int32, x.shape, 1)
    partner = jnp.where((lane & 32) == 0, pltpu.roll(x, LANES - 32, 1), pltpu.roll(x, 32, 1))
    return x * cos_t + partner * sin_t


def _attn_kernel(sink_ref, q_ref, *rest, tq, n_kv, seq, local, between=(lambda: None, lambda: None)):
    if local:
        (km_ref, kp_ref, kn_ref, vm_ref, vp_ref, vn_ref, kc_ref, vc_ref,
         cm_ref, cp_ref, cn_ref, sm_ref, sp_ref, sn_ref, o_ref) = rest
    else:
        kc_ref, vc_ref, o_ref = rest
    i = pl.program_id(1)
    q_scale = HEAD_DIM ** -0.5 * LOG2E
    nt = (((1,), (1,)), ((), ()))
    sub = WINDOW
    n_ctx = kc_ref.shape[0]

    if local:
        cos_q, sin_q = cm_ref[...], sm_ref[...]
        cos_k = jnp.concatenate([cp_ref[...], cos_q, cn_ref[...]], axis=0)
        sin_k = jnp.concatenate([sp_ref[...], sin_q, sn_ref[...]], axis=0)
        r = lax.broadcasted_iota(jnp.int32, (sub, 3 * sub + n_ctx), 0)
        c = lax.broadcasted_iota(jnp.int32, (sub, 3 * sub + n_ctx), 1)
        band = (c >= r) & (c <= r + 2 * WINDOW)
        is_ctx = c >= 3 * sub

    units = []
    for kh in range(n_kv):
        ks = slice(kh * HEAD_DIM, (kh + 1) * HEAD_DIM)
        kc = kc_ref[:, ks]
        vc = vc_ref[:, ks]
        heads = [kh * GROUP + g for g in range(GROUP)]
        qh = []
        for h in heads:
            x = q_ref[:, h * HEAD_DIM:(h + 1) * HEAD_DIM].astype(F32)
            if local:
                x = _rope(x, cos_q, sin_q)
            qh.append((x * q_scale).astype(BF16))
        if local:
            k_span = jnp.concatenate([kp_ref[:, ks], km_ref[:, ks], kn_ref[:, ks]], axis=0)
            k_span = _rope(k_span.astype(F32), cos_k, sin_k).astype(BF16)
            v_span = jnp.concatenate([vp_ref[:, ks], vm_ref[:, ks], vn_ref[:, ks]], axis=0)
        sink2 = jnp.concatenate([jnp.full((sub, 1), sink_ref[h] * LOG2E, F32) for h in heads], axis=0)

        for sb in range(tq // sub):
            rows = slice(sb * sub, (sb + 1) * sub)
            qs = jnp.concatenate([x[rows] for x in qh], axis=0)
            if local:
                keys = jnp.concatenate([k_span[sb * sub:(sb + 3) * sub], kc], axis=0)
                vals = jnp.concatenate([v_span[sb * sub:(sb + 3) * sub], vc], axis=0)
                kpos = i * tq + (sb - 1) * sub + c
                valid = is_ctx | (band & (kpos >= 0) & (kpos < seq))
                bias = jnp.concatenate([jnp.where(valid, 0.0, NEG_INF)] * GROUP, axis=0)
            else:
                keys, vals, bias = kc, vc, None
            units.append(dict(rows=rows, heads=heads, qs=qs, keys=keys, vals=vals, bias=bias, sink2=sink2))

    for u in units:
        s = lax.dot_general(u["qs"], u["keys"], nt, preferred_element_type=F32)
        u["s"] = s if u["bias"] is None else s + u["bias"]
    between[0]()
    for u in units:
        m = jnp.maximum(jnp.max(u["s"], axis=-1, keepdims=True), u["sink2"])
        p = jnp.exp2(u["s"] - m)
        u["den"] = jnp.sum(p, axis=-1, keepdims=True) + jnp.exp2(u["sink2"] - m)
        u["p"] = p.astype(BF16)
    between[1]()
    for u in units:
        o = jnp.dot(u["p"], u["vals"], preferred_element_type=F32) / u["den"]
        for g, h in enumerate(u["heads"]):
            o_ref[u["rows"], h * HEAD_DIM:(h + 1) * HEAD_DIM] = o[g * sub:(g + 1) * sub].astype(o_ref.dtype)


def _attn_operands(sink, z, zc, cols, cos_t, sin_t, local, tq):
    l = z.shape[1]
    n_ctx = zc.shape[1]
    n_heads = sink.shape[0]
    n_kv = n_heads // GROUP
    qw, kw = n_heads * HEAD_DIM, n_kv * HEAD_DIM
    col_q, col_k, col_v = cols
    assert tq % WINDOW == 0
    wb = tq // WINDOW
    n_w = l // WINDOW
    prev = lambda bb, i: jnp.maximum(i * wb - 1, 0)
    nxt = lambda bb, i: jnp.minimum((i + 1) * wb, n_w - 1)
    in_specs = [
        pl.BlockSpec(memory_space=pltpu.SMEM),
        pl.BlockSpec((None, tq, qw), lambda bb, i: (bb, i, col_q)),
    ]
    args = [sink, z]
    if local:
        for col in (col_k, col_v):
            in_specs += [
                pl.BlockSpec((None, tq, kw), lambda bb, i, col=col: (bb, i, col)),
                pl.BlockSpec((None, WINDOW, kw), lambda bb, i, col=col: (bb, prev(bb, i), col)),
                pl.BlockSpec((None, WINDOW, kw), lambda bb, i, col=col: (bb, nxt(bb, i), col)),
            ]
            args += [z, z, z]
    in_specs += [
        pl.BlockSpec((None, n_ctx, kw), lambda bb, i: (bb, 0, col_k)),
        pl.BlockSpec((None, n_ctx, kw), lambda bb, i: (bb, 0, col_v)),
    ]
    args += [zc, zc]
    if local:
        for tab in (cos_t, sin_t):
            in_specs += [
                pl.BlockSpec((tq, HEAD_DIM), lambda bb, i: (i, 0)),
                pl.BlockSpec((WINDOW, HEAD_DIM), lambda bb, i: (prev(bb, i), 0)),
                pl.BlockSpec((WINDOW, HEAD_DIM), lambda bb, i: (nxt(bb, i), 0)),
            ]
            args += [tab, tab, tab]
    return in_specs, args


def _merge_kernel(*refs, n_attn, tm, n_t, d, n_kv, seq, local):
    attn_refs = refs[:n_attn]
    (h_ref, gate_ref, hf_ref, hb_ref, rg_ref, sb_ref, cg_ref, cgp_ref, cgn_ref, sx_ref, sxp_ref, sxn_ref,
     g_ref, bm_ref, scw_ref, wb_ref, wo_ref, o_ref, ppad_ref, att_ref) = refs[n_attn:]
    i = pl.program_id(1)
    ya = (hf_ref[...].astype(F32) + hb_ref[...].astype(F32)) * _gelu_tanh(rg_ref[...].astype(F32))

    _fill_padded(ppad_ref,
                 cg_ref[...].astype(F32) * sx_ref[...].astype(F32),
                 (cgp_ref[...].astype(F32) * sxp_ref[...].astype(F32))[8:16],
                 (cgn_ref[...].astype(F32) * sxn_ref[...].astype(F32))[0:8],
                 i > 0, i < n_t - 1, tm)
    conv = None
    for k in range(SC_CONV):
        off = 8 - SC_CONV_LEFT + k
        term = scw_ref[k:k + 1, :] * ppad_ref[off:off + tm, :]
        conv = term if conv is None else conv + term
    yb = sb_ref[...].astype(F32) * conv

    def lift(br, y):
        gates = jax.nn.sigmoid(g_ref[:, br * d:(br + 1) * d].astype(F32) + bm_ref[br:br + 1, :])
        return gates * jnp.dot(y, wb_ref[br], preferred_element_type=F32)

    lifted = []
    _attn_kernel(*attn_refs, att_ref, tq=tm, n_kv=n_kv, seq=seq, local=local,
                 between=(lambda: lifted.append(lift(0, ya.astype(BF16))),
                          lambda: lifted.append(lift(1, yb.astype(BF16)))))
    merged = lifted[0] + lifted[1] + lift(2, att_ref[...])
    y = jnp.dot(merged.astype(BF16), wo_ref[...], preferred_element_type=F32)
    o_ref[...] = h_ref[...] + gate_ref[...] * y


def _merge_call(h, gate, hf, hb, z, zc, cols, attn_cols, sink, cos_t, sin_t, local, b_merge, sc_w,
                w_branch, w_out, layer):
    b, l, d = h.shape
    bw = hf.shape[2]
    col_g, col_rg, col_sb, col_cg, col_sx = cols
    tm = _largest_tile(l, 256, WINDOW)
    n_t = l // tm
    attn_specs, attn_args = _attn_operands(sink, z, zc, attn_cols, cos_t, sin_t, local, tm)
    hb_per = tm // HALO
    n_h = l // HALO
    per_batch = gate.shape[0] == b
    tile = lambda col: (lambda bb, i: (bb, i, col))
    prev = lambda col: (lambda bb, i: (bb, jnp.maximum(i * hb_per - 1, 0), col))
    nxt = lambda col: (lambda bb, i: (bb, jnp.minimum((i + 1) * hb_per, n_h - 1), col))
    once = pl.Buffered(1)
    in_specs = [
        pl.BlockSpec((None, tm, d), tile(0)),
        pl.BlockSpec((None, 1, d), lambda bb, i: (bb if per_batch else 0, 0, 0)),
        pl.BlockSpec((None, tm, bw), tile(0)),
        pl.BlockSpec((None, tm, bw), tile(0)),
        pl.BlockSpec((None, tm, bw), tile(col_rg)),
        pl.BlockSpec((None, tm, bw), tile(col_sb)),
        pl.BlockSpec((None, tm, bw), tile(col_cg)),
        pl.BlockSpec((None, HALO, bw), prev(col_cg)),
        pl.BlockSpec((None, HALO, bw), nxt(col_cg)),
        pl.BlockSpec((None, tm, bw), tile(col_sx)),
        pl.BlockSpec((None, HALO, bw), prev(col_sx)),
        pl.BlockSpec((None, HALO, bw), nxt(col_sx)),
        pl.BlockSpec((None, tm, N_BRANCH * d), tile(col_g)),
        pl.BlockSpec((N_BRANCH, d), lambda bb, i: (0, 0)),
        pl.BlockSpec((SC_CONV, bw), lambda bb, i: (0, 0)),
        pl.BlockSpec((None, N_BRANCH, bw, d), lambda bb, i: (layer, 0, 0, 0), pipeline_mode=once),
        pl.BlockSpec((None, d, d), lambda bb, i: (layer, 0, 0), pipeline_mode=once),
    ]
    return pl.pallas_call(
        functools.partial(_merge_kernel, n_attn=len(attn_specs), tm=tm, n_t=n_t, d=d,
                          n_kv=sink.shape[0] // GROUP, seq=l, local=local),
        grid=(b, n_t),
        in_specs=attn_specs + in_specs,
        out_specs=pl.BlockSpec((None, tm, d), tile(0)),
        out_shape=jax.ShapeDtypeStruct((b, l, d), F32),
        scratch_shapes=[pltpu.VMEM((tm + 16, bw), F32), pltpu.VMEM((tm, bw), BF16)],
        compiler_params=_cparams(("parallel", "parallel")),
        name="merge_local" if local else "merge_ctx",
    )(*attn_args, h, gate, hf, hb, z, z, z, z, z, z, z, z, z, b_merge, sc_w, w_branch, w_out)


def _prep_ffn(w13, w2):
    f = w2.shape[1]
    tf = min(FFN_CHUNK, f // LANES * LANES)
    fm = f // tf * tf
    tails = None
    if fm < f:
        tails = (w13[:, :, fm:f].astype(BF16), w13[:, :, f + fm:].astype(BF16), w2[:, fm:].astype(BF16))
    return w13.astype(BF16), w13[:, :, f:f + fm].astype(BF16), w2.astype(BF16), tails


def _cast_kernel(w_ref, o_ref):
    o_ref[...] = w_ref[...].astype(o_ref.dtype)


def _prep_w_in(w_in, d):
    depth, _, nc = w_in.shape
    n_g = N_BRANCH * d
    tn = _largest_tile(math.gcd(n_g, nc - n_g), 512, LANES)
    n_blk, g_blk = nc // tn, n_g // tn
    return pl.pallas_call(
        _cast_kernel,
        grid=(depth, n_blk),
        in_specs=[pl.BlockSpec((None, d, tn), lambda l, j: (l, 0, (j + n_blk - g_blk) % n_blk))],
        out_specs=pl.BlockSpec((None, d, tn), lambda l, j: (l, 0, j)),
        out_shape=jax.ShapeDtypeStruct(w_in.shape, BF16),
        compiler_params=_cparams(("parallel", "parallel")),
        name="cast_w_in",
    )(w_in)


def _rope_tables(l):
    pos = jnp.arange(l)
    row = (pos // GRID_W).astype(F32)
    col = (pos % GRID_W).astype(F32)
    half = HEAD_DIM // 2
    inv = ROPE_BASE ** (-jnp.arange(0, half, 2, dtype=F32) / half)
    ar, ac = row[:, None] * inv, col[:, None] * inv
    cos_t = jnp.concatenate([jnp.cos(ar), jnp.cos(ar), jnp.cos(ac), jnp.cos(ac)], axis=-1)
    sin_t = jnp.concatenate([-jnp.sin(ar), jnp.sin(ar), -jnp.sin(ac), jnp.sin(ac)], axis=-1)
    return cos_t, sin_t


def kernel(x, c, ctx, c_ctx, ada_w, ada_b, norm_g, ffn1_w13, ffn1_w2, w_in, b_merge, rnn_conv_w,
           rnn_conv_b, lru_w_a, lru_b_a, lru_w_x, lru_b_x, lru_lambda, sc_conv_w, attn_sink, w_branch,
           w_out, ffn2_w13, ffn2_w2, final_norm_g):
    b, l, d = x.shape
    n_ctx = ctx.shape[1]
    depth = ada_w.shape[0]
    bw = w_branch.shape[2]
    n_heads = attn_sink.shape[1]
    kw = (n_heads // GROUP) * HEAD_DIM
    assert bw == n_heads * HEAD_DIM and w_in.shape[2] == 6 * bw + 2 * kw + N_BRANCH * d
    assert (N_BRANCH * d) % bw == 0 and (N_BRANCH * d + 6 * bw) % kw == 0 and b + 1 <= 8
    g_blocks = N_BRANCH * d // bw
    col_rx, col_rg, col_sb, col_cg, col_sx, col_q = (g_blocks + n for n in range(6))
    col_k = (N_BRANCH * d + 6 * bw) // kw
    col_v = col_k + 1

    cvec = jnp.zeros((8, d), F32).at[:b].set(c).at[b].set(c_ctx)
    mod = _ada_call(cvec, ada_w, ada_b).reshape(depth, 8, N_MOD, 1, d)
    cos_t, sin_t = _rope_tables(l)
    w_in_p = _prep_w_in(w_in, d)
    ffn_a = _prep_ffn(ffn1_w13, ffn1_w2)
    ffn_b = _prep_ffn(ffn2_w13, ffn2_w2)
    wbr, wo = w_branch.astype(BF16), w_out.astype(BF16)

    h = x.reshape(b * l, d)
    hc = ctx.reshape(b * n_ctx, d)
    for layer in range(depth):
        last = layer == depth - 1
        ml = mod[layer, :b]
        mc = mod[layer, b:b + 1]
        ng = norm_g[layer]
        wa, wx = lru_w_a[layer].astype(BF16), lru_w_x[layer].astype(BF16)

        h = _ffn_call(h, ng[0], ml[:, 0], ml[:, 1], ml[:, 2], ffn_a, layer)
        hc = _ffn_call(hc, ng[0], mc[:, 0], mc[:, 1], mc[:, 2], ffn_a, layer)

        z = _inproj_call(h, ng[1], ml[:, 3], ml[:, 4], w_in_p, layer).reshape(b, l, -1)
        zc = _inproj_call(hc, ng[1], mc[:, 3], mc[:, 4], w_in_p, layer).reshape(b, n_ctx, -1)
        lru = (rnn_conv_w[layer], rnn_conv_b[layer], wa, lru_b_a[layer], wx, lru_b_x[layer], lru_lambda[layer])
        hcf, hcb, h_last = _rglru_call(zc, col_rx, *lru, jnp.zeros((b, 2, bw), F32))
        hlf, hlb, _ = _rglru_call(z, col_rx, *lru, h_last)
        merge_cols = (0, col_rg, col_sb, col_cg, col_sx)
        attn_cols = (col_q, col_k, col_v)
        mix = (attn_sink[layer], cos_t, sin_t)
        mix_w = (b_merge[layer], sc_conv_w[layer], wbr, wo, layer)
        h = _merge_call(h.reshape(b, l, d), ml[:, 5], hlf, hlb, z, zc, merge_cols, attn_cols, *mix, True,
                        *mix_w).reshape(b * l, d)

        h = _ffn_call(h, ng[2], ml[:, 6], ml[:, 7], ml[:, 8], ffn_b, layer,
                      final_g=final_norm_g if last else None)
        if not last:
            hc = _merge_call(hc.reshape(b, n_ctx, d), mc[:, 5], hcf, hcb, zc, zc, merge_cols, attn_cols, *mix,
                             False, *mix_w).reshape(b * n_ctx, d)
            hc = _ffn_call(hc, ng[2], mc[:, 6], mc[:, 7], mc[:, 8], ffn_b, layer)
    return h.reshape(b, l, d)
```

```python
import functools
import math

import jax
import jax.numpy as jnp
from jax import lax
from jax.experimental import pallas as pl
from jax.experimental.pallas import tpu as pltpu

F32 = jnp.float32
BF16 = jnp.bfloat16

HEAD_DIM = 128
GROUP = 4
WINDOW = 128
GRID_W = 64
ROPE_BASE = 10000.0
RNN_BLOCK = 128
RNN_CONV = 4
RNN_CONV_LEFT = 2
SC_CONV = 3
SC_CONV_LEFT = 1
LRU_C = 8.0
N_BRANCH = 3
N_MOD = 9
EPS = 1e-6
NEG_INF = -1e30
LOG2E = 1.4426950408889634

LANES = 128
SUBLANES_F32 = 8
SUBLANES_BF16 = 16
HALO = SUBLANES_BF16
VMEM_LIMIT_BYTES = 56 * 1024 * 1024
FFN_CHUNK = 1024
FFN_TOKENS = 512
INPROJ_TOKENS = 1024
INPROJ_COLS = 1280
LRU_TOKENS = 512
MIX_TOKENS = 256
CAST_COLS = 512
ADALN_COLS = 1024


def _cparams(semantics):
    return pltpu.CompilerParams(dimension_semantics=semantics, vmem_limit_bytes=VMEM_LIMIT_BYTES)


def _pow2_slices(n_steps, tm):
    n = 0
    while 2 * max(n, 1) <= n_steps and tm % (2 * max(n, 1) * SUBLANES_BF16) == 0:
        n = 2 * max(n, 1)
    return n


def _largest_tile(n, cap, quantum):
    best = None
    t = quantum
    while t <= min(n, cap):
        if n % t == 0:
            best = t
        t += quantum
    assert best is not None, (n, cap, quantum)
    return best


def _rmsnorm(x, g):
    return x * lax.rsqrt(jnp.mean(x * x, axis=-1, keepdims=True) + EPS) * g


def _norm_mod(x, g, shift, scale):
    return _rmsnorm(x, g) * (1.0 + scale) + shift


def _silu(x):
    return x * jax.nn.sigmoid(x)


def _gelu_tanh(x):
    return 0.5 * x * (1.0 + jnp.tanh(0.7978845608028654 * (x + 0.044715 * (x * x * x))))


def _softplus(x):
    return jnp.maximum(x, 0.0) + jnp.log1p(jnp.exp(-jnp.abs(x)))


def _one_minus_exp2(x, ex):
    kahan = (ex - 1.0) * x / jnp.log(ex)
    em1 = jnp.where(x < -1.0, ex - 1.0, jnp.where(ex == 1.0, x, kahan))
    return -em1 * (ex + 1.0)


def _ada_kernel(c_ref, w_ref, b_ref, o_ref):
    s = _silu(c_ref[...]).astype(BF16)
    o_ref[...] = jnp.dot(s, w_ref[...].astype(BF16), preferred_element_type=F32) + b_ref[...]


def _ada_call(cvec, ada_w, ada_b):
    depth, d, nm = ada_w.shape
    tn = _largest_tile(nm, ADALN_COLS, LANES)
    return pl.pallas_call(
        _ada_kernel,
        grid=(depth, nm // tn),
        in_specs=[
            pl.BlockSpec((8, d), lambda l, j: (0, 0)),
            pl.BlockSpec((None, d, tn), lambda l, j: (l, 0, j)),
            pl.BlockSpec((None, 1, tn), lambda l, j: (l, 0, j)),
        ],
        out_specs=pl.BlockSpec((None, 8, tn), lambda l, j: (l, 0, j)),
        out_shape=jax.ShapeDtypeStruct((depth, 8, nm), F32),
        compiler_params=_cparams(("arbitrary", "arbitrary")),
        name="adaln",
    )(cvec, ada_w, ada_b.reshape(depth, 1, nm))


def _swiglu_chunk(u, wg_ref, wu_ref, w2_ref):
    gate = jnp.dot(u, wg_ref[...], preferred_element_type=F32)
    up = jnp.dot(u, wu_ref[...], preferred_element_type=F32)
    act = (_silu(gate) * up).astype(BF16)
    return jnp.dot(act, w2_ref[...], preferred_element_type=F32)


def _ffn_kernel(h_ref, g_ref, shift_ref, scale_ref, gate_ref, wg_ref, wu_ref, w2_ref, *rest, tail, final_norm,
                n_chunks):
    rest = list(rest)
    wgt_ref, wut_ref, w2t_ref = (rest.pop(0), rest.pop(0), rest.pop(0)) if tail else (None, None, None)
    fg_ref = rest.pop(0) if final_norm else None
    o_ref, u_ref = rest
    j = pl.program_id(1)
    last = n_chunks - 1

    def first():
        u = _norm_mod(h_ref[...], g_ref[...], shift_ref[...], scale_ref[...]).astype(BF16)
        u_ref[...] = u
        acc = _swiglu_chunk(u, wg_ref, wu_ref, w2_ref)
        return acc + _swiglu_chunk(u, wgt_ref, wut_ref, w2t_ref) if tail else acc

    def finish(acc):
        hn = h_ref[...] + (0.5 * gate_ref[...]) * acc
        if final_norm:
            hn = _rmsnorm(hn, fg_ref[...])
        o_ref[...] = hn

    if last == 0:
        finish(first())
        return

    @pl.when(j == 0)
    def _():
        o_ref[...] = first()

    @pl.when((j > 0) & (j < last))
    def _():
        o_ref[...] += _swiglu_chunk(u_ref[...], wg_ref, wu_ref, w2_ref)

    @pl.when(j == last)
    def _():
        finish(o_ref[...] + _swiglu_chunk(u_ref[...], wg_ref, wu_ref, w2_ref))


def _ffn_call(h, g, shift, scale, gate, weights, layer, final_g=None):
    wg, wu, w2, tails = weights
    m, d = h.shape
    nb = shift.shape[0]
    tf = min(FFN_CHUNK, wu.shape[2])
    n_chunks = wu.shape[2] // tf
    tm = _largest_tile(m // nb, FFN_TOKENS, SUBLANES_F32)
    tpb = (m // nb) // tm
    row = lambda i, j: (i // tpb, 0, 0)
    const = lambda i, j: (0, 0)
    once = pl.Buffered(1)
    in_specs = [
        pl.BlockSpec((tm, d), lambda i, j: (i, 0)),
        pl.BlockSpec((1, d), const),
        pl.BlockSpec((None, 1, d), row),
        pl.BlockSpec((None, 1, d), row),
        pl.BlockSpec((None, 1, d), row),
        pl.BlockSpec((None, d, tf), lambda i, j: (layer, 0, j)),
        pl.BlockSpec((None, d, tf), lambda i, j: (layer, 0, j)),
        pl.BlockSpec((None, tf, d), lambda i, j: (layer, j, 0)),
    ]
    args = [h, g.reshape(1, d), shift, scale, gate, wg, wu, w2]
    if tails is not None:
        for w in tails:
            in_specs.append(pl.BlockSpec((None,) + w.shape[1:], lambda i, j: (layer, 0, 0), pipeline_mode=once))
            args.append(w)
    if final_g is not None:
        in_specs.append(pl.BlockSpec((1, d), const))
        args.append(final_g.reshape(1, d))
    return pl.pallas_call(
        functools.partial(_ffn_kernel, tail=tails is not None, final_norm=final_g is not None,
                          n_chunks=n_chunks),
        grid=(m // tm, n_chunks),
        in_specs=in_specs,
        out_specs=pl.BlockSpec((tm, d), lambda i, j: (i, 0)),
        out_shape=jax.ShapeDtypeStruct((m, d), F32),
        scratch_shapes=[pltpu.VMEM((tm, d), BF16)],
        compiler_params=_cparams(("parallel", "arbitrary")),
        name="ffn",
    )(*args)


def _inproj_kernel(h_ref, g_ref, shift_ref, scale_ref, w_ref, o_ref, ua_ref, ub_ref, *, n_slices):
    i, j = pl.program_id(0), pl.program_id(1)
    tm = h_ref.shape[0]
    norm = lambda x: _norm_mod(x, g_ref[...], shift_ref[...], scale_ref[...]).astype(BF16)

    def step(cur_ref, nxt_ref):
        @pl.when((j == 0) & ((i == 0) | (n_slices == 0)))
        def _():
            cur_ref[...] = norm(h_ref[...])

        if n_slices:
            rows = tm // n_slices
            r0 = pl.multiple_of(jnp.clip(j - 1, 0, n_slices - 1) * rows, rows)
            nxt_ref[pl.ds(r0, rows), :] = norm(h_ref[pl.ds(r0, rows), :])
        o_ref[...] = jnp.dot(cur_ref[...], w_ref[...], preferred_element_type=F32).astype(o_ref.dtype)

    pl.when(i % 2 == 0)(lambda: step(ua_ref, ub_ref))
    pl.when(i % 2 == 1)(lambda: step(ub_ref, ua_ref))


def _inproj_call(h, g, shift, scale, w, layer):
    m, d = h.shape
    nb = shift.shape[0]
    nc = w.shape[2]
    tm = _largest_tile(m // nb, INPROJ_TOKENS, SUBLANES_BF16)
    tpb = (m // nb) // tm
    tn = _largest_tile(nc, INPROJ_COLS, LANES)
    n_i, n_j = m // tm, nc // tn
    n_slices = _pow2_slices(n_j - 1, tm)
    if n_slices:
        tile = lambda i, j: jnp.minimum(i + jnp.where(j > 0, 1, 0), n_i - 1)
    else:
        tile = lambda i, j: i
    row = lambda i, j: (tile(i, j) // tpb, 0, 0)
    return pl.pallas_call(
        functools.partial(_inproj_kernel, n_slices=n_slices),
        grid=(n_i, n_j),
        in_specs=[
            pl.BlockSpec((tm, d), lambda i, j: (tile(i, j), 0)),
            pl.BlockSpec((1, d), lambda i, j: (0, 0)),
            pl.BlockSpec((None, 1, d), row),
            pl.BlockSpec((None, 1, d), row),
            pl.BlockSpec((None, d, tn), lambda i, j: (layer, 0, j)),
        ],
        out_specs=pl.BlockSpec((tm, tn), lambda i, j: (i, j)),
        out_shape=jax.ShapeDtypeStruct((m, nc), BF16),
        scratch_shapes=[pltpu.VMEM((tm, d), BF16), pltpu.VMEM((tm, d), BF16)],
        compiler_params=_cparams(("arbitrary", "arbitrary")),
        name="in_proj",
    )(h, g.reshape(1, d), shift, scale, w)


def _fill_padded(pad_ref, main, prev_tail, next_head, has_prev, has_next, t):
    pad_ref[0:8, :] = jnp.where(has_prev, prev_tail, 0.0)
    pad_ref[8:8 + t, :] = main
    pad_ref[8 + t:16 + t, :] = jnp.where(has_next, next_head, 0.0)


def _segment_perm(t):
    r = lax.broadcasted_iota(jnp.int32, (t, t), 0)
    c = lax.broadcasted_iota(jnp.int32, (t, t), 1)
    return (c == (r % 8) * (t // 8) + r // 8).astype(BF16)


def _rglru_kernel(zf_ref, zfp_ref, zfn_ref, zb_ref, zbp_ref, zbn_ref, perm_ref, permt_ref, cw_ref, cb_ref,
                  wa_ref, ba_ref, wx_ref, bx_ref, lam_ref, h0_ref, of_ref, ob_ref, last_ref,
                  xp_ref, xa_ref, a_ref, u_ref, hs_ref, pc_ref, c8_ref, car_ref, *, t, n_t, n_blk):
    i = pl.program_id(1)
    seg = t // 8

    @pl.when(i == 0)
    def _():
        car_ref[...] = h0_ref[...]

    sp = _softplus(-lam_ref[...])
    sub = lax.broadcasted_iota(jnp.int32, (8, 1), 0)

    def gates(d, z_ref, zp_ref, zn_ref, tile):
        xq = jnp.dot(perm_ref[...], z_ref[...], preferred_element_type=F32)
        prev = jnp.where(tile > 0, zp_ref[...].astype(F32), 0.0)
        nxt = jnp.where(tile < n_t - 1, zn_ref[...].astype(F32), 0.0)
        xp_ref[0:8, :] = jnp.where(sub == 0, prev[14:15], pltpu.roll(xq[(seg - 2) * 8:(seg - 1) * 8], 1, 0))
        xp_ref[8:16, :] = jnp.where(sub == 0, prev[15:16], pltpu.roll(xq[(seg - 1) * 8:seg * 8], 1, 0))
        xp_ref[16:16 + t, :] = xq
        xp_ref[16 + t:24 + t, :] = jnp.where(sub == 7, nxt[0:1], pltpu.roll(xq[0:8], 7, 0))
        xa = cb_ref[...]
        for k in range(RNN_CONV):
            off = 8 * (2 - RNN_CONV_LEFT + k)
            xa = xa + cw_ref[k:k + 1, :] * xp_ref[off:off + t, :]
        xa_ref[...] = xa
        for n in range(n_blk):
            sl = slice(n * RNN_BLOCK, (n + 1) * RNN_BLOCK)
            xs = xa_ref[:, sl]
            xb = xs.astype(BF16)
            rg = jax.nn.sigmoid(jnp.dot(xb, wa_ref[d, n], preferred_element_type=F32) + ba_ref[d:d + 1, sl])
            ig = jax.nn.sigmoid(jnp.dot(xb, wx_ref[d, n], preferred_element_type=F32) + bx_ref[d:d + 1, sl])
            log_a = (-LRU_C * rg) * sp[d:d + 1, sl]
            a = jnp.exp(log_a)
            a_ref[d, :, sl] = a
            u_ref[d, :, sl] = jnp.sqrt(_one_minus_exp2(log_a, a)) * (ig * xs)

    gates(0, zf_ref, zfp_ref, zfn_ref, i)
    gates(1, zb_ref, zbp_ref, zbn_ref, n_t - 1 - i)

    def step(j, carry):
        hf, pf, hb, pb = carry
        rf = pl.multiple_of(j * 8, 8)
        rb = pl.multiple_of((seg - 1 - j) * 8, 8)
        af, ab = a_ref[0, pl.ds(rf, 8), :], a_ref[1, pl.ds(rb, 8), :]
        hf = af * hf + u_ref[0, pl.ds(rf, 8), :]
        hb = ab * hb + u_ref[1, pl.ds(rb, 8), :]
        pf, pb = pf * af, pb * ab
        hs_ref[0, pl.ds(rf, 8), :] = hf
        hs_ref[1, pl.ds(rb, 8), :] = hb
        pc_ref[0, pl.ds(rf, 8), :] = pf
        pc_ref[1, pl.ds(rb, 8), :] = pb
        return hf, pf, hb, pb

    zeros, ones = jnp.zeros((8, a_ref.shape[2]), F32), jnp.ones((8, a_ref.shape[2]), F32)
    lax.fori_loop(0, seg, step, (zeros, ones, zeros, ones), unroll=4)

    lf, qf = hs_ref[0, (seg - 1) * 8:seg * 8, :], pc_ref[0, (seg - 1) * 8:seg * 8, :]
    lb, qb = hs_ref[1, 0:8, :], pc_ref[1, 0:8, :]
    cf, cb = car_ref[0:1, :], car_ref[1:2, :]
    for s in range(8):
        c8_ref[0, s:s + 1, :] = cf
        cf = lf[s:s + 1] + qf[s:s + 1] * cf
        c8_ref[1, 7 - s:8 - s, :] = cb
        cb = lb[7 - s:8 - s] + qb[7 - s:8 - s] * cb
    car_ref[0:1, :] = cf
    car_ref[1:2, :] = cb
    last_ref[0:1, :] = cf
    last_ref[1:2, :] = cb

    for d, o_ref in ((0, of_ref), (1, ob_ref)):
        h = hs_ref[d] + pc_ref[d] * jnp.tile(c8_ref[d], (seg, 1))
        o_ref[...] = jnp.dot(permt_ref[...], h.astype(BF16), preferred_element_type=F32).astype(o_ref.dtype)


def _rglru_call(z, col_rx, cw, cb, wa, ba, wx, bx, lam, h0):
    b, l, _ = z.shape
    c = cw.shape[1]
    n_blk = c // RNN_BLOCK
    t = _largest_tile(l, LRU_TOKENS, HALO)
    n_t = l // t
    hb_per = t // HALO
    n_h = l // HALO

    def main(rev):
        return lambda bb, i: (bb, (n_t - 1 - i) if rev else i, col_rx)

    def prev(rev):
        return lambda bb, i: (bb, jnp.maximum(((n_t - 1 - i) if rev else i) * hb_per - 1, 0), col_rx)

    def nxt(rev):
        return lambda bb, i: (bb, jnp.minimum((((n_t - 1 - i) if rev else i) + 1) * hb_per, n_h - 1), col_rx)

    const2 = lambda bb, i: (0, 0)
    const4 = lambda bb, i: (0, 0, 0, 0)
    in_specs = []
    for rev in (False, True):
        in_specs += [pl.BlockSpec((None, t, c), main(rev)),
                     pl.BlockSpec((None, HALO, c), prev(rev)),
                     pl.BlockSpec((None, HALO, c), nxt(rev))]
    perm = _segment_perm(t)
    in_specs += [
        pl.BlockSpec((t, t), const2, pipeline_mode=pl.Buffered(1)),
        pl.BlockSpec((t, t), const2, pipeline_mode=pl.Buffered(1)),
        pl.BlockSpec((RNN_CONV, c), const2),
        pl.BlockSpec((1, c), const2),
        pl.BlockSpec((2, n_blk, RNN_BLOCK, RNN_BLOCK), const4),
        pl.BlockSpec((2, c), const2),
        pl.BlockSpec((2, n_blk, RNN_BLOCK, RNN_BLOCK), const4),
        pl.BlockSpec((2, c), const2),
        pl.BlockSpec((2, c), const2),
        pl.BlockSpec((None, 2, c), lambda bb, i: (bb, 0, 0)),
    ]
    return pl.pallas_call(
        functools.partial(_rglru_kernel, t=t, n_t=n_t, n_blk=n_blk),
        grid=(b, n_t),
        in_specs=in_specs,
        out_specs=[
            pl.BlockSpec((None, t, c), lambda bb, i: (bb, i, 0)),
            pl.BlockSpec((None, t, c), lambda bb, i: (bb, n_t - 1 - i, 0)),
            pl.BlockSpec((None, 2, c), lambda bb, i: (bb, 0, 0)),
        ],
        out_shape=[
            jax.ShapeDtypeStruct((b, l, c), BF16),
            jax.ShapeDtypeStruct((b, l, c), BF16),
            jax.ShapeDtypeStruct((b, 2, c), F32),
        ],
        scratch_shapes=[
            pltpu.VMEM((t + 24, c), F32),
            pltpu.VMEM((t, c), F32),
            pltpu.VMEM((2, t, c), F32),
            pltpu.VMEM((2, t, c), F32),
            pltpu.VMEM((2, t, c), F32),
            pltpu.VMEM((2, t, c), F32),
            pltpu.VMEM((2, 8, c), F32),
            pltpu.VMEM((2, c), F32),
        ],
        compiler_params=_cparams(("parallel", "arbitrary")),
        name="rglru",
    )(z, z, z, z, z, z, perm, perm.T, cw, cb.reshape(1, c), wa, ba, wx, bx, lam, h0)


def _rope(x, cos_t, sin_t):
    lane = lax.broadcasted_iota(jnp.int32, x.shape, 1)
    partner = jnp.where((lane & 32) == 0, pltpu.roll(x, LANES - 32, 1), pltpu.roll(x, 32, 1))
    return x * cos_t + partner * sin_t


def _attn_kernel(sink_ref, q_ref, *rest, tq, n_kv, seq, local, between=(lambda: None, lambda: None)):
    if local:
        (km_ref, kp_ref, kn_ref, vm_ref, vp_ref, vn_ref, kc_ref, vc_ref,
         cm_ref, cp_ref, cn_ref, sm_ref, sp_ref, sn_ref, o_ref) = rest
    else:
        kc_ref, vc_ref, o_ref = rest
    i = pl.program_id(1)
    q_scale = HEAD_DIM ** -0.5 * LOG2E
    nt = (((1,), (1,)), ((), ()))
    sub = WINDOW
    n_ctx = kc_ref.shape[0]

    if local:
        cos_q, sin_q = cm_ref[...], sm_ref[...]
        cos_k = jnp.concatenate([cp_ref[...], cos_q, cn_ref[...]], axis=0)
        sin_k = jnp.concatenate([sp_ref[...], sin_q, sn_ref[...]], axis=0)
        r = lax.broadcasted_iota(jnp.int32, (sub, 3 * sub + n_ctx), 0)
        c = lax.broadcasted_iota(jnp.int32, (sub, 3 * sub + n_ctx), 1)
        band = (c >= r) & (c <= r + 2 * WINDOW)
        is_ctx = c >= 3 * sub

    units = []
    for kh in range(n_kv):
        ks = slice(kh * HEAD_DIM, (kh + 1) * HEAD_DIM)
        kc = kc_ref[:, ks]
        vc = vc_ref[:, ks]
        heads = [kh * GROUP + g for g in range(GROUP)]
        qh = []
        for h in heads:
            x = q_ref[:, h * HEAD_DIM:(h + 1) * HEAD_DIM].astype(F32)
            if local:
                x = _rope(x, cos_q, sin_q)
            qh.append((x * q_scale).astype(BF16))
        if local:
            k_span = jnp.concatenate([kp_ref[:, ks], km_ref[:, ks], kn_ref[:, ks]], axis=0)
            k_span = _rope(k_span.astype(F32), cos_k, sin_k).astype(BF16)
            v_span = jnp.concatenate([vp_ref[:, ks], vm_ref[:, ks], vn_ref[:, ks]], axis=0)
        sink2 = jnp.concatenate([jnp.full((sub, 1), sink_ref[h] * LOG2E, F32) for h in heads], axis=0)

        for sb in range(tq // sub):
            rows = slice(sb * sub, (sb + 1) * sub)
            qs = jnp.concatenate([x[rows] for x in qh], axis=0)
            if local:
                keys = jnp.concatenate([k_span[sb * sub:(sb + 3) * sub], kc], axis=0)
                vals = jnp.concatenate([v_span[sb * sub:(sb + 3) * sub], vc], axis=0)
                kpos = i * tq + (sb - 1) * sub + c
                valid = is_ctx | (band & (kpos >= 0) & (kpos < seq))
                bias = jnp.concatenate([jnp.where(valid, 0.0, NEG_INF)] * GROUP, axis=0)
            else:
                keys, vals, bias = kc, vc, None
            units.append(dict(rows=rows, heads=heads, qs=qs, keys=keys, vals=vals, bias=bias, sink2=sink2))

    for u in units:
        s = lax.dot_general(u["qs"], u["keys"], nt, preferred_element_type=F32)
        u["s"] = s if u["bias"] is None else s + u["bias"]
    between[0]()
    for u in units:
        m = jnp.maximum(jnp.max(u["s"], axis=-1, keepdims=True), u["sink2"])
        p = jnp.exp2(u["s"] - m)
        u["den"] = jnp.sum(p, axis=-1, keepdims=True) + jnp.exp2(u["sink2"] - m)
        u["p"] = p.astype(BF16)
    between[1]()
    for u in units:
        o = jnp.dot(u["p"], u["vals"], preferred_element_type=F32) / u["den"]
        for g, h in enumerate(u["heads"]):
            o_ref[u["rows"], h * HEAD_DIM:(h + 1) * HEAD_DIM] = o[g * sub:(g + 1) * sub].astype(o_ref.dtype)


def _attn_operands(sink, z, zc, cols, cos_t, sin_t, local, tq):
    l = z.shape[1]
    n_ctx = zc.shape[1]
    n_heads = sink.shape[0]
    n_kv = n_heads // GROUP
    qw, kw = n_heads * HEAD_DIM, n_kv * HEAD_DIM
    col_q, col_k, col_v = cols
    assert tq % WINDOW == 0
    wb = tq // WINDOW
    n_w = l // WINDOW
    prev = lambda bb, i: jnp.maximum(i * wb - 1, 0)
    nxt = lambda bb, i: jnp.minimum((i + 1) * wb, n_w - 1)
    in_specs = [
        pl.BlockSpec(memory_space=pltpu.SMEM),
        pl.BlockSpec((None, tq, qw), lambda bb, i: (bb, i, col_q)),
    ]
    args = [sink, z]
    if local:
        for col in (col_k, col_v):
            in_specs += [
                pl.BlockSpec((None, tq, kw), lambda bb, i, col=col: (bb, i, col)),
                pl.BlockSpec((None, WINDOW, kw), lambda bb, i, col=col: (bb, prev(bb, i), col)),
                pl.BlockSpec((None, WINDOW, kw), lambda bb, i, col=col: (bb, nxt(bb, i), col)),
            ]
            args += [z, z, z]
    in_specs += [
        pl.BlockSpec((None, n_ctx, kw), lambda bb, i: (bb, 0, col_k)),
        pl.BlockSpec((None, n_ctx, kw), lambda bb, i: (bb, 0, col_v)),
    ]
    args += [zc, zc]
    if local:
        for tab in (cos_t, sin_t):
            in_specs += [
                pl.BlockSpec((tq, HEAD_DIM), lambda bb, i: (i, 0)),
                pl.BlockSpec((WINDOW, HEAD_DIM), lambda bb, i: (prev(bb, i), 0)),
                pl.BlockSpec((WINDOW, HEAD_DIM), lambda bb, i: (nxt(bb, i), 0)),
            ]
            args += [tab, tab, tab]
    return in_specs, args


def _merge_kernel(*refs, n_attn, tm, n_t, d, n_kv, seq, local):
    attn_refs = refs[:n_attn]
    (h_ref, gate_ref, hf_ref, hb_ref, rg_ref, sb_ref, cg_ref, cgp_ref, cgn_ref, sx_ref, sxp_ref, sxn_ref,
     g_ref, bm_ref, scw_ref, wb_ref, wo_ref, o_ref, ppad_ref, att_ref) = refs[n_attn:]
    i = pl.program_id(1)
    ya = (hf_ref[...].astype(F32) + hb_ref[...].astype(F32)) * _gelu_tanh(rg_ref[...].astype(F32))

    _fill_padded(ppad_ref,
                 cg_ref[...].astype(F32) * sx_ref[...].astype(F32),
                 (cgp_ref[...].astype(F32) * sxp_ref[...].astype(F32))[8:16],
                 (cgn_ref[...].astype(F32) * sxn_ref[...].astype(F32))[0:8],
                 i > 0, i < n_t - 1, tm)
    conv = None
    for k in range(SC_CONV):
        off = 8 - SC_CONV_LEFT + k
        term = scw_ref[k:k + 1, :] * ppad_ref[off:off + tm, :]
        conv = term if conv is None else conv + term
    yb = sb_ref[...].astype(F32) * conv

    def lift(br, y):
        gates = jax.nn.sigmoid(g_ref[:, br * d:(br + 1) * d].astype(F32) + bm_ref[br:br + 1, :])
        return gates * jnp.dot(y, wb_ref[br], preferred_element_type=F32)

    lifted = []
    _attn_kernel(*attn_refs, att_ref, tq=tm, n_kv=n_kv, seq=seq, local=local,
                 between=(lambda: lifted.append(lift(0, ya.astype(BF16))),
                          lambda: lifted.append(lift(1, yb.astype(BF16)))))
    merged = lifted[0] + lifted[1] + lift(2, att_ref[...])
    y = jnp.dot(merged.astype(BF16), wo_ref[...], preferred_element_type=F32)
    o_ref[...] = h_ref[...] + gate_ref[...] * y


def _merge_call(h, gate, hf, hb, z, zc, cols, attn_cols, sink, cos_t, sin_t, local, b_merge, sc_w,
                w_branch, w_out, layer):
    b, l, d = h.shape
    bw = hf.shape[2]
    col_g, col_rg, col_sb, col_cg, col_sx = cols
    tm = _largest_tile(l, MIX_TOKENS, WINDOW)
    n_t = l // tm
    attn_specs, attn_args = _attn_operands(sink, z, zc, attn_cols, cos_t, sin_t, local, tm)
    hb_per = tm // HALO
    n_h = l // HALO
    per_batch = gate.shape[0] == b
    tile = lambda col: (lambda bb, i: (bb, i, col))
    prev = lambda col: (lambda bb, i: (bb, jnp.maximum(i * hb_per - 1, 0), col))
    nxt = lambda col: (lambda bb, i: (bb, jnp.minimum((i + 1) * hb_per, n_h - 1), col))
    once = pl.Buffered(1)
    in_specs = [
        pl.BlockSpec((None, tm, d), tile(0)),
        pl.BlockSpec((None, 1, d), lambda bb, i: (bb if per_batch else 0, 0, 0)),
        pl.BlockSpec((None, tm, bw), tile(0)),
        pl.BlockSpec((None, tm, bw), tile(0)),
        pl.BlockSpec((None, tm, bw), tile(col_rg)),
        pl.BlockSpec((None, tm, bw), tile(col_sb)),
        pl.BlockSpec((None, tm, bw), tile(col_cg)),
        pl.BlockSpec((None, HALO, bw), prev(col_cg)),
        pl.BlockSpec((None, HALO, bw), nxt(col_cg)),
        pl.BlockSpec((None, tm, bw), tile(col_sx)),
        pl.BlockSpec((None, HALO, bw), prev(col_sx)),
        pl.BlockSpec((None, HALO, bw), nxt(col_sx)),
        pl.BlockSpec((None, tm, N_BRANCH * d), tile(col_g)),
        pl.BlockSpec((N_BRANCH, d), lambda bb, i: (0, 0)),
        pl.BlockSpec((SC_CONV, bw), lambda bb, i: (0, 0)),
        pl.BlockSpec((None, N_BRANCH, bw, d), lambda bb, i: (layer, 0, 0, 0), pipeline_mode=once),
        pl.BlockSpec((None, d, d), lambda bb, i: (layer, 0, 0), pipeline_mode=once),
    ]
    return pl.pallas_call(
        functools.partial(_merge_kernel, n_attn=len(attn_specs), tm=tm, n_t=n_t, d=d,
                          n_kv=sink.shape[0] // GROUP, seq=l, local=local),
        grid=(b, n_t),
        in_specs=attn_specs + in_specs,
        out_specs=pl.BlockSpec((None, tm, d), tile(0)),
        out_shape=jax.ShapeDtypeStruct((b, l, d), F32),
        scratch_shapes=[pltpu.VMEM((tm + 16, bw), F32), pltpu.VMEM((tm, bw), BF16)],
        compiler_params=_cparams(("parallel", "parallel")),
        name="merge_local" if local else "merge_ctx",
    )(*attn_args, h, gate, hf, hb, z, z, z, z, z, z, z, z, z, b_merge, sc_w, w_branch, w_out)


def _prep_ffn(w13, w2):
    f = w2.shape[1]
    tf = min(FFN_CHUNK, f // LANES * LANES)
    fm = f // tf * tf
    tails = None
    if fm < f:
        tails = (w13[:, :, fm:f].astype(BF16), w13[:, :, f + fm:].astype(BF16), w2[:, fm:].astype(BF16))
    return w13.astype(BF16), w13[:, :, f:f + fm].astype(BF16), w2.astype(BF16), tails


def _cast_kernel(w_ref, o_ref):
    o_ref[...] = w_ref[...].astype(o_ref.dtype)


def _prep_w_in(w_in, d):
    depth, _, nc = w_in.shape
    n_g = N_BRANCH * d
    tn = _largest_tile(math.gcd(n_g, nc - n_g), CAST_COLS, LANES)
    n_blk, g_blk = nc // tn, n_g // tn
    return pl.pallas_call(
        _cast_kernel,
        grid=(depth, n_blk),
        in_specs=[pl.BlockSpec((None, d, tn), lambda l, j: (l, 0, (j + n_blk - g_blk) % n_blk))],
        out_specs=pl.BlockSpec((None, d, tn), lambda l, j: (l, 0, j)),
        out_shape=jax.ShapeDtypeStruct(w_in.shape, BF16),
        compiler_params=_cparams(("parallel", "parallel")),
        name="cast_w_in",
    )(w_in)


def _rope_tables(l):
    pos = jnp.arange(l)
    row = (pos // GRID_W).astype(F32)
    col = (pos % GRID_W).astype(F32)
    half = HEAD_DIM // 2
    inv = ROPE_BASE ** (-jnp.arange(0, half, 2, dtype=F32) / half)
    ar, ac = row[:, None] * inv, col[:, None] * inv
    cos_t = jnp.concatenate([jnp.cos(ar), jnp.cos(ar), jnp.cos(ac), jnp.cos(ac)], axis=-1)
    sin_t = jnp.concatenate([-jnp.sin(ar), jnp.sin(ar), -jnp.sin(ac), jnp.sin(ac)], axis=-1)
    return cos_t, sin_t


def kernel(x, c, ctx, c_ctx, ada_w, ada_b, norm_g, ffn1_w13, ffn1_w2, w_in, b_merge, rnn_conv_w,
           rnn_conv_b, lru_w_a, lru_b_a, lru_w_x, lru_b_x, lru_lambda, sc_conv_w, attn_sink, w_branch,
           w_out, ffn2_w13, ffn2_w2, final_norm_g):
    b, l, d = x.shape
    n_ctx = ctx.shape[1]
    depth = ada_w.shape[0]
    bw = w_branch.shape[2]
    n_heads = attn_sink.shape[1]
    kw = (n_heads // GROUP) * HEAD_DIM
    assert bw == n_heads * HEAD_DIM and w_in.shape[2] == 6 * bw + 2 * kw + N_BRANCH * d
    assert (N_BRANCH * d) % bw == 0 and (N_BRANCH * d + 6 * bw) % kw == 0 and b + 1 <= 8
    g_blocks = N_BRANCH * d // bw
    col_rx, col_rg, col_sb, col_cg, col_sx, col_q = (g_blocks + n for n in range(6))
    col_k = (N_BRANCH * d + 6 * bw) // kw
    col_v = col_k + 1

    cvec = jnp.zeros((8, d), F32).at[:b].set(c).at[b].set(c_ctx)
    mod = _ada_call(cvec, ada_w, ada_b).reshape(depth, 8, N_MOD, 1, d)
    cos_t, sin_t = _rope_tables(l)
    w_in_p = _prep_w_in(w_in, d)
    ffn_a = _prep_ffn(ffn1_w13, ffn1_w2)
    ffn_b = _prep_ffn(ffn2_w13, ffn2_w2)
    wbr, wo = w_branch.astype(BF16), w_out.astype(BF16)

    h = x.reshape(b * l, d)
    hc = ctx.reshape(b * n_ctx, d)
    for layer in range(depth):
        last = layer == depth - 1
        ml = mod[layer, :b]
        mc = mod[layer, b:b + 1]
        ng = norm_g[layer]
        wa, wx = lru_w_a[layer].astype(BF16), lru_w_x[layer].astype(BF16)

        h = _ffn_call(h, ng[0], ml[:, 0], ml[:, 1], ml[:, 2], ffn_a, layer)
        hc = _ffn_call(hc, ng[0], mc[:, 0], mc[:, 1], mc[:, 2], ffn_a, layer)

        z = _inproj_call(h, ng[1], ml[:, 3], ml[:, 4], w_in_p, layer).reshape(b, l, -1)
        zc = _inproj_call(hc, ng[1], mc[:, 3], mc[:, 4], w_in_p, layer).reshape(b, n_ctx, -1)
        lru = (rnn_conv_w[layer], rnn_conv_b[layer], wa, lru_b_a[layer], wx, lru_b_x[layer], lru_lambda[layer])
        hcf, hcb, h_last = _rglru_call(zc, col_rx, *lru, jnp.zeros((b, 2, bw), F32))
        hlf, hlb, _ = _rglru_call(z, col_rx, *lru, h_last)
        merge_cols = (0, col_rg, col_sb, col_cg, col_sx)
        attn_cols = (col_q, col_k, col_v)
        mix = (attn_sink[layer], cos_t, sin_t)
        mix_w = (b_merge[layer], sc_conv_w[layer], wbr, wo, layer)
        h = _merge_call(h.reshape(b, l, d), ml[:, 5], hlf, hlb, z, zc, merge_cols, attn_cols, *mix, True,
                        *mix_w).reshape(b * l, d)

        h = _ffn_call(h, ng[2], ml[:, 6], ml[:, 7], ml[:, 8], ffn_b, layer,
                      final_g=final_norm_g if last else None)
        if not last:
            hc = _merge_call(hc.reshape(b, n_ctx, d), mc[:, 5], hcf, hcb, zc, zc, merge_cols, attn_cols, *mix,
                             False, *mix_w).reshape(b * n_ctx, d)
            hc = _ffn_call(hc, ng[2], mc[:, 6], mc[:, 7], mc[:, 8], ffn_b, layer)
    return h.reshape(b, l, d)
```

```python
import functools
import math

import jax
import jax.numpy as jnp
from jax import lax
from jax.experimental import pallas as pl
from jax.experimental.pallas import tpu as pltpu

F32 = jnp.float32
BF16 = jnp.bfloat16

HEAD_DIM = 128
GROUP = 4
WINDOW = 128
GRID_W = 64
ROPE_BASE = 10000.0
RNN_BLOCK = 128
RNN_CONV = 4
RNN_CONV_LEFT = 2
SC_CONV = 3
SC_CONV_LEFT = 1
LRU_C = 8.0
N_BRANCH = 3
N_MOD = 9
EPS = 1e-6
NEG_INF = -1e30
LOG2E = 1.4426950408889634

LANES = 128
SUBLANES_F32 = 8
SUBLANES_BF16 = 16
HALO = SUBLANES_BF16
VMEM_LIMIT_BYTES = 56 * 1024 * 1024
FFN_CHUNK = 1024
FFN_TOKENS = 512
INPROJ_TOKENS = 1024
INPROJ_COLS = 1280
LRU_TOKENS = 512
MIX_TOKENS = 256
CAST_COLS = 512
ADALN_COLS = 1024


def _cparams(semantics):
    return pltpu.CompilerParams(dimension_semantics=semantics, vmem_limit_bytes=VMEM_LIMIT_BYTES)


def _pow2_slices(n_steps, tm):
    n = 0
    while 2 * max(n, 1) <= n_steps and tm % (2 * max(n, 1) * SUBLANES_BF16) == 0:
        n = 2 * max(n, 1)
    return n


def _largest_tile(n, cap, quantum):
    best = None
    t = quantum
    while t <= min(n, cap):
        if n % t == 0:
            best = t
        t += quantum
    assert best is not None, (n, cap, quantum)
    return best


def _rmsnorm(x, g):
    return x * lax.rsqrt(jnp.mean(x * x, axis=-1, keepdims=True) + EPS) * g


def _norm_mod(x, g, shift, scale):
    return _rmsnorm(x, g) * (1.0 + scale) + shift


def _silu(x):
    return x * jax.nn.sigmoid(x)


def _gelu_tanh(x):
    return 0.5 * x * (1.0 + jnp.tanh(0.7978845608028654 * (x + 0.044715 * (x * x * x))))


def _softplus(x):
    return jnp.maximum(x, 0.0) + jnp.log1p(jnp.exp(-jnp.abs(x)))


def _one_minus_exp2(x, ex):
    kahan = (ex - 1.0) * x / jnp.log(ex)
    em1 = jnp.where(x < -1.0, ex - 1.0, jnp.where(ex == 1.0, x, kahan))
    return -em1 * (ex + 1.0)


def _ada_kernel(c_ref, w_ref, b_ref, o_ref):
    s = _silu(c_ref[...]).astype(BF16)
    o_ref[...] = jnp.dot(s, w_ref[...].astype(BF16), preferred_element_type=F32) + b_ref[...]


def _ada_call(cvec, ada_w, ada_b):
    depth, d, nm = ada_w.shape
    tn = _largest_tile(nm, ADALN_COLS, LANES)
    return pl.pallas_call(
        _ada_kernel,
        grid=(depth, nm // tn),
        in_specs=[
            pl.BlockSpec((8, d), lambda l, j: (0, 0)),
            pl.BlockSpec((None, d, tn), lambda l, j: (l, 0, j)),
            pl.BlockSpec((None, 1, tn), lambda l, j: (l, 0, j)),
        ],
        out_specs=pl.BlockSpec((None, 8, tn), lambda l, j: (l, 0, j)),
        out_shape=jax.ShapeDtypeStruct((depth, 8, nm), F32),
        compiler_params=_cparams(("arbitrary", "arbitrary")),
        name="adaln",
    )(cvec, ada_w, ada_b.reshape(depth, 1, nm))


def _swiglu_chunk(u, wg_ref, wu_ref, w2_ref):
    gate = jnp.dot(u, wg_ref[...], preferred_element_type=F32)
    up = jnp.dot(u, wu_ref[...], preferred_element_type=F32)
    act = (_silu(gate) * up).astype(BF16)
    return jnp.dot(act, w2_ref[...], preferred_element_type=F32)


def _ffn_kernel(h_ref, g_ref, shift_ref, scale_ref, gate_ref, wg_ref, wu_ref, w2_ref, *rest, tail, final_norm,
                n_chunks):
    rest = list(rest)
    wgt_ref, wut_ref, w2t_ref = (rest.pop(0), rest.pop(0), rest.pop(0)) if tail else (None, None, None)
    fg_ref = rest.pop(0) if final_norm else None
    o_ref, u_ref = rest
    j = pl.program_id(1)
    last = n_chunks - 1

    def first():
        u = _norm_mod(h_ref[...], g_ref[...], shift_ref[...], scale_ref[...]).astype(BF16)
        u_ref[...] = u
        acc = _swiglu_chunk(u, wg_ref, wu_ref, w2_ref)
        return acc + _swiglu_chunk(u, wgt_ref, wut_ref, w2t_ref) if tail else acc

    def finish(acc):
        hn = h_ref[...] + (0.5 * gate_ref[...]) * acc
        if final_norm:
            hn = _rmsnorm(hn, fg_ref[...])
        o_ref[...] = hn

    if last == 0:
        finish(first())
        return

    @pl.when(j == 0)
    def _():
        o_ref[...] = first()

    @pl.when((j > 0) & (j < last))
    def _():
        o_ref[...] += _swiglu_chunk(u_ref[...], wg_ref, wu_ref, w2_ref)

    @pl.when(j == last)
    def _():
        finish(o_ref[...] + _swiglu_chunk(u_ref[...], wg_ref, wu_ref, w2_ref))


def _ffn_call(h, g, shift, scale, gate, weights, layer, final_g=None):
    wg, wu, w2, tails = weights
    m, d = h.shape
    nb = shift.shape[0]
    tf = min(FFN_CHUNK, wu.shape[2])
    n_chunks = wu.shape[2] // tf
    tm = _largest_tile(m // nb, FFN_TOKENS, SUBLANES_F32)
    tpb = (m // nb) // tm
    row = lambda i, j: (i // tpb, 0, 0)
    const = lambda i, j: (0, 0)
    once = pl.Buffered(1)
    in_specs = [
        pl.BlockSpec((tm, d), lambda i, j: (i, 0)),
        pl.BlockSpec((1, d), const),
        pl.BlockSpec((None, 1, d), row),
        pl.BlockSpec((None, 1, d), row),
        pl.BlockSpec((None, 1, d), row),
        pl.BlockSpec((None, d, tf), lambda i, j: (layer, 0, j)),
        pl.BlockSpec((None, d, tf), lambda i, j: (layer, 0, j)),
        pl.BlockSpec((None, tf, d), lambda i, j: (layer, j, 0)),
    ]
    args = [h, g.reshape(1, d), shift, scale, gate, wg, wu, w2]
    if tails is not None:
        for w in tails:
            in_specs.append(pl.BlockSpec((None,) + w.shape[1:], lambda i, j: (layer, 0, 0), pipeline_mode=once))
            args.append(w)
    if final_g is not None:
        in_specs.append(pl.BlockSpec((1, d), const))
        args.append(final_g.reshape(1, d))
    return pl.pallas_call(
        functools.partial(_ffn_kernel, tail=tails is not None, final_norm=final_g is not None,
                          n_chunks=n_chunks),
        grid=(m // tm, n_chunks),
        in_specs=in_specs,
        out_specs=pl.BlockSpec((tm, d), lambda i, j: (i, 0)),
        out_shape=jax.ShapeDtypeStruct((m, d), F32),
        scratch_shapes=[pltpu.VMEM((tm, d), BF16)],
        compiler_params=_cparams(("parallel", "arbitrary")),
        name="ffn",
    )(*args)


def _inproj_kernel(h_ref, g_ref, shift_ref, scale_ref, w_ref, o_ref, ua_ref, ub_ref, *, n_slices):
    i, j = pl.program_id(0), pl.program_id(1)
    tm = h_ref.shape[0]
    norm = lambda x: _norm_mod(x, g_ref[...], shift_ref[...], scale_ref[...]).astype(BF16)

    def step(cur_ref, nxt_ref):
        @pl.when((j == 0) & ((i == 0) | (n_slices == 0)))
        def _():
            cur_ref[...] = norm(h_ref[...])

        if n_slices:
            rows = tm // n_slices
            r0 = pl.multiple_of(jnp.clip(j - 1, 0, n_slices - 1) * rows, rows)
            nxt_ref[pl.ds(r0, rows), :] = norm(h_ref[pl.ds(r0, rows), :])
        o_ref[...] = jnp.dot(cur_ref[...], w_ref[...], preferred_element_type=F32).astype(o_ref.dtype)

    pl.when(i % 2 == 0)(lambda: step(ua_ref, ub_ref))
    pl.when(i % 2 == 1)(lambda: step(ub_ref, ua_ref))


def _inproj_call(h, g, shift, scale, w, layer):
    m, d = h.shape
    nb = shift.shape[0]
    nc = w.shape[2]
    tm = _largest_tile(m // nb, INPROJ_TOKENS, SUBLANES_BF16)
    tpb = (m // nb) // tm
    tn = _largest_tile(nc, INPROJ_COLS, LANES)
    n_i, n_j = m // tm, nc // tn
    n_slices = _pow2_slices(n_j - 1, tm)
    if n_slices:
        tile = lambda i, j: jnp.minimum(i + jnp.where(j > 0, 1, 0), n_i - 1)
    else:
        tile = lambda i, j: i
    row = lambda i, j: (tile(i, j) // tpb, 0, 0)
    return pl.pallas_call(
        functools.partial(_inproj_kernel, n_slices=n_slices),
        grid=(n_i, n_j),
        in_specs=[
            pl.BlockSpec((tm, d), lambda i, j: (tile(i, j), 0)),
            pl.BlockSpec((1, d), lambda i, j: (0, 0)),
            pl.BlockSpec((None, 1, d), row),
            pl.BlockSpec((None, 1, d), row),
            pl.BlockSpec((None, d, tn), lambda i, j: (layer, 0, j)),
        ],
        out_specs=pl.BlockSpec((tm, tn), lambda i, j: (i, j)),
        out_shape=jax.ShapeDtypeStruct((m, nc), BF16),
        scratch_shapes=[pltpu.VMEM((tm, d), BF16), pltpu.VMEM((tm, d), BF16)],
        compiler_params=_cparams(("arbitrary", "arbitrary")),
        name="in_proj",
    )(h, g.reshape(1, d), shift, scale, w)


def _fill_padded(pad_ref, main, prev_tail, next_head, has_prev, has_next, t):
    pad_ref[0:8, :] = jnp.where(has_prev, prev_tail, 0.0)
    pad_ref[8:8 + t, :] = main
    pad_ref[8 + t:16 + t, :] = jnp.where(has_next, next_head, 0.0)


def _segment_perm(t):
    r = lax.broadcasted_iota(jnp.int32, (t, t), 0)
    c = lax.broadcasted_iota(jnp.int32, (t, t), 1)
    return (c == (r % 8) * (t // 8) + r // 8).astype(BF16)


def _rglru_kernel(zf_ref, zfp_ref, zfn_ref, zb_ref, zbp_ref, zbn_ref, perm_ref, permt_ref, cw_ref, cb_ref,
                  wa_ref, ba_ref, wx_ref, bx_ref, lam_ref, h0_ref, of_ref, ob_ref, last_ref,
                  xp_ref, xa_ref, a_ref, u_ref, hs_ref, pc_ref, c8_ref, car_ref, *, t, n_t, n_blk):
    i = pl.program_id(1)
    seg = t // 8

    @pl.when(i == 0)
    def _():
        car_ref[...] = h0_ref[...]

    sp = _softplus(-lam_ref[...])
    sub = lax.broadcasted_iota(jnp.int32, (8, 1), 0)

    def gates(d, z_ref, zp_ref, zn_ref, tile):
        xq = jnp.dot(perm_ref[...], z_ref[...], preferred_element_type=F32)
        prev = jnp.where(tile > 0, zp_ref[...].astype(F32), 0.0)
        nxt = jnp.where(tile < n_t - 1, zn_ref[...].astype(F32), 0.0)
        xp_ref[0:8, :] = jnp.where(sub == 0, prev[14:15], pltpu.roll(xq[(seg - 2) * 8:(seg - 1) * 8], 1, 0))
        xp_ref[8:16, :] = jnp.where(sub == 0, prev[15:16], pltpu.roll(xq[(seg - 1) * 8:seg * 8], 1, 0))
        xp_ref[16:16 + t, :] = xq
        xp_ref[16 + t:24 + t, :] = jnp.where(sub == 7, nxt[0:1], pltpu.roll(xq[0:8], 7, 0))
        xa = cb_ref[...]
        for k in range(RNN_CONV):
            off = 8 * (2 - RNN_CONV_LEFT + k)
            xa = xa + cw_ref[k:k + 1, :] * xp_ref[off:off + t, :]
        xa_ref[...] = xa
        for n in range(n_blk):
            sl = slice(n * RNN_BLOCK, (n + 1) * RNN_BLOCK)
            xs = xa_ref[:, sl]
            xb = xs.astype(BF16)
            rg = jax.nn.sigmoid(jnp.dot(xb, wa_ref[d, n], preferred_element_type=F32) + ba_ref[d:d + 1, sl])
            ig = jax.nn.sigmoid(jnp.dot(xb, wx_ref[d, n], preferred_element_type=F32) + bx_ref[d:d + 1, sl])
            log_a = (-LRU_C * rg) * sp[d:d + 1, sl]
            a = jnp.exp(log_a)
            a_ref[d, :, sl] = a
            u_ref[d, :, sl] = jnp.sqrt(_one_minus_exp2(log_a, a)) * (ig * xs)

    gates(0, zf_ref, zfp_ref, zfn_ref, i)
    gates(1, zb_ref, zbp_ref, zbn_ref, n_t - 1 - i)

    def step(j, carry):
        hf, pf, hb, pb = carry
        rf = pl.multiple_of(j * 8, 8)
        rb = pl.multiple_of((seg - 1 - j) * 8, 8)
        af, ab = a_ref[0, pl.ds(rf, 8), :], a_ref[1, pl.ds(rb, 8), :]
        hf = af * hf + u_ref[0, pl.ds(rf, 8), :]
        hb = ab * hb + u_ref[1, pl.ds(rb, 8), :]
        pf, pb = pf * af, pb * ab
        hs_ref[0, pl.ds(rf, 8), :] = hf
        hs_ref[1, pl.ds(rb, 8), :] = hb
        pc_ref[0, pl.ds(rf, 8), :] = pf
        pc_ref[1, pl.ds(rb, 8), :] = pb
        return hf, pf, hb, pb

    zeros, ones = jnp.zeros((8, a_ref.shape[2]), F32), jnp.ones((8, a_ref.shape[2]), F32)
    lax.fori_loop(0, seg, step, (zeros, ones, zeros, ones), unroll=4)

    lf, qf = hs_ref[0, (seg - 1) * 8:seg * 8, :], pc_ref[0, (seg - 1) * 8:seg * 8, :]
    lb, qb = hs_ref[1, 0:8, :], pc_ref[1, 0:8, :]
    cf, cb = car_ref[0:1, :], car_ref[1:2, :]
    for s in range(8):
        c8_ref[0, s:s + 1, :] = cf
        cf = lf[s:s + 1] + qf[s:s + 1] * cf
        c8_ref[1, 7 - s:8 - s, :] = cb
        cb = lb[7 - s:8 - s] + qb[7 - s:8 - s] * cb
    car_ref[0:1, :] = cf
    car_ref[1:2, :] = cb
    last_ref[0:1, :] = cf
    last_ref[1:2, :] = cb

    for d, o_ref in ((0, of_ref), (1, ob_ref)):
        h = hs_ref[d] + pc_ref[d] * jnp.tile(c8_ref[d], (seg, 1))
        o_ref[...] = jnp.dot(permt_ref[...], h.astype(BF16), preferred_element_type=F32).astype(o_ref.dtype)


def _rglru_call(z, col_rx, cw, cb, wa, ba, wx, bx, lam, h0):
    b, l, _ = z.shape
    c = cw.shape[1]
    n_blk = c // RNN_BLOCK
    t = _largest_tile(l, LRU_TOKENS, HALO)
    n_t = l // t
    hb_per = t // HALO
    n_h = l // HALO

    def main(rev):
        return lambda bb, i: (bb, (n_t - 1 - i) if rev else i, col_rx)

    def prev(rev):
        return lambda bb, i: (bb, jnp.maximum(((n_t - 1 - i) if rev else i) * hb_per - 1, 0), col_rx)

    def nxt(rev):
        return lambda bb, i: (bb, jnp.minimum((((n_t - 1 - i) if rev else i) + 1) * hb_per, n_h - 1), col_rx)

    const2 = lambda bb, i: (0, 0)
    const4 = lambda bb, i: (0, 0, 0, 0)
    in_specs = []
    for rev in (False, True):
        in_specs += [pl.BlockSpec((None, t, c), main(rev)),
                     pl.BlockSpec((None, HALO, c), prev(rev)),
                     pl.BlockSpec((None, HALO, c), nxt(rev))]
    perm = _segment_perm(t)
    in_specs += [
        pl.BlockSpec((t, t), const2, pipeline_mode=pl.Buffered(1)),
        pl.BlockSpec((t, t), const2, pipeline_mode=pl.Buffered(1)),
        pl.BlockSpec((RNN_CONV, c), const2),
        pl.BlockSpec((1, c), const2),
        pl.BlockSpec((2, n_blk, RNN_BLOCK, RNN_BLOCK), const4),
        pl.BlockSpec((2, c), const2),
        pl.BlockSpec((2, n_blk, RNN_BLOCK, RNN_BLOCK), const4),
        pl.BlockSpec((2, c), const2),
        pl.BlockSpec((2, c), const2),
        pl.BlockSpec((None, 2, c), lambda bb, i: (bb, 0, 0)),
    ]
    return pl.pallas_call(
        functools.partial(_rglru_kernel, t=t, n_t=n_t, n_blk=n_blk),
        grid=(b, n_t),
        in_specs=in_specs,
        out_specs=[
            pl.BlockSpec((None, t, c), lambda bb, i: (bb, i, 0)),
            pl.BlockSpec((None, t, c), lambda bb, i: (bb, n_t - 1 - i, 0)),
            pl.BlockSpec((None, 2, c), lambda bb, i: (bb, 0, 0)),
        ],
        out_shape=[
            jax.ShapeDtypeStruct((b, l, c), BF16),
            jax.ShapeDtypeStruct((b, l, c), BF16),
            jax.ShapeDtypeStruct((b, 2, c), F32),
        ],
        scratch_shapes=[
            pltpu.VMEM((t + 24, c), F32),
            pltpu.VMEM((t, c), F32),
            pltpu.VMEM((2, t, c), F32),
            pltpu.VMEM((2, t, c), F32),
            pltpu.VMEM((2, t, c), F32),
            pltpu.VMEM((2, t, c), F32),
            pltpu.VMEM((2, 8, c), F32),
            pltpu.VMEM((2, c), F32),
        ],
        compiler_params=_cparams(("parallel", "arbitrary")),
        name="rglru",
    )(z, z, z, z, z, z, perm, perm.T, cw, cb.reshape(1, c), wa, ba, wx, bx, lam, h0)


def _rope(x, cos_t, sin_t):
    lane = lax.broadcasted_iota(jnp.int32, x.shape, 1)
    partner = jnp.where((lane & 32) == 0, pltpu.roll(x, LANES - 32, 1), pltpu.roll(x, 32, 1))
    return x * cos_t + partner * sin_t


def _attn_kernel(sink_ref, q_ref, *rest, tq, n_kv, seq, local, between=(lambda: None, lambda: None)):
    if local:
        kvm_ref, kvp_ref, kvn_ref, kvc_ref, tabm_ref, tabp_ref, tabn_ref, o_ref = rest
    else:
        kvc_ref, o_ref = rest
    i = pl.program_id(1)
    q_scale = HEAD_DIM ** -0.5 * LOG2E
    nt = (((1,), (1,)), ((), ()))
    sub = WINDOW
    n_ctx = kvc_ref.shape[0]
    kw = n_kv * HEAD_DIM

    if local:
        cos_q, sin_q = tabm_ref[:, :HEAD_DIM], tabm_ref[:, HEAD_DIM:]
        cos_k = jnp.concatenate([tabp_ref[:, :HEAD_DIM], cos_q, tabn_ref[:, :HEAD_DIM]], axis=0)
        sin_k = jnp.concatenate([tabp_ref[:, HEAD_DIM:], sin_q, tabn_ref[:, HEAD_DIM:]], axis=0)
        r = lax.broadcasted_iota(jnp.int32, (sub, 3 * sub + n_ctx), 0)
        c = lax.broadcasted_iota(jnp.int32, (sub, 3 * sub + n_ctx), 1)
        band = (c >= r) & (c <= r + 2 * WINDOW)
        is_ctx = c >= 3 * sub

    units = []
    for kh in range(n_kv):
        ks = slice(kh * HEAD_DIM, (kh + 1) * HEAD_DIM)
        vs = slice(kw + kh * HEAD_DIM, kw + (kh + 1) * HEAD_DIM)
        kc = kvc_ref[:, ks]
        vc = kvc_ref[:, vs]
        heads = [kh * GROUP + g for g in range(GROUP)]
        qh = []
        for h in heads:
            x = q_ref[:, h * HEAD_DIM:(h + 1) * HEAD_DIM].astype(F32)
            if local:
                x = _rope(x, cos_q, sin_q)
            qh.append((x * q_scale).astype(BF16))
        if local:
            k_span = jnp.concatenate([kvp_ref[:, ks], kvm_ref[:, ks], kvn_ref[:, ks]], axis=0)
            k_span = _rope(k_span.astype(F32), cos_k, sin_k).astype(BF16)
            v_span = jnp.concatenate([kvp_ref[:, vs], kvm_ref[:, vs], kvn_ref[:, vs]], axis=0)
        sink2 = jnp.concatenate([jnp.full((sub, 1), sink_ref[h] * LOG2E, F32) for h in heads], axis=0)

        for sb in range(tq // sub):
            rows = slice(sb * sub, (sb + 1) * sub)
            qs = jnp.concatenate([x[rows] for x in qh], axis=0)
            if local:
                keys = jnp.concatenate([k_span[sb * sub:(sb + 3) * sub], kc], axis=0)
                vals = jnp.concatenate([v_span[sb * sub:(sb + 3) * sub], vc], axis=0)
                kpos = i * tq + (sb - 1) * sub + c
                valid = is_ctx | (band & (kpos >= 0) & (kpos < seq))
                bias = jnp.concatenate([jnp.where(valid, 0.0, NEG_INF)] * GROUP, axis=0)
            else:
                keys, vals, bias = kc, vc, None
            units.append(dict(rows=rows, heads=heads, qs=qs, keys=keys, vals=vals, bias=bias, sink2=sink2))

    for u in units:
        s = lax.dot_general(u["qs"], u["keys"], nt, preferred_element_type=F32)
        u["s"] = s if u["bias"] is None else s + u["bias"]
    between[0]()
    for u in units:
        m = jnp.maximum(jnp.max(u["s"], axis=-1, keepdims=True), u["sink2"])
        p = jnp.exp2(u["s"] - m)
        u["den"] = jnp.sum(p, axis=-1, keepdims=True) + jnp.exp2(u["sink2"] - m)
        u["p"] = p.astype(BF16)
    between[1]()
    for u in units:
        o = jnp.dot(u["p"], u["vals"], preferred_element_type=F32) / u["den"]
        for g, h in enumerate(u["heads"]):
            o_ref[u["rows"], h * HEAD_DIM:(h + 1) * HEAD_DIM] = o[g * sub:(g + 1) * sub].astype(o_ref.dtype)


def _attn_operands(sink, z, zc, cols, cos_t, sin_t, local, tq):
    l = z.shape[1]
    n_ctx = zc.shape[1]
    n_heads = sink.shape[0]
    n_kv = n_heads // GROUP
    qw, kw = n_heads * HEAD_DIM, n_kv * HEAD_DIM
    col_q, col_k, col_v = cols
    assert tq % WINDOW == 0 and col_v == col_k + 1 and col_k % 2 == 0
    col_kv = col_k // 2
    wb = tq // WINDOW
    n_w = l // WINDOW
    prev = lambda bb, i: jnp.maximum(i * wb - 1, 0)
    nxt = lambda bb, i: jnp.minimum((i + 1) * wb, n_w - 1)
    in_specs = [
        pl.BlockSpec(memory_space=pltpu.SMEM),
        pl.BlockSpec((None, tq, qw), lambda bb, i: (bb, i, col_q)),
    ]
    args = [sink, z]
    if local:
        in_specs += [
            pl.BlockSpec((None, tq, 2 * kw), lambda bb, i: (bb, i, col_kv)),
            pl.BlockSpec((None, WINDOW, 2 * kw), lambda bb, i: (bb, prev(bb, i), col_kv)),
            pl.BlockSpec((None, WINDOW, 2 * kw), lambda bb, i: (bb, nxt(bb, i), col_kv)),
        ]
        args += [z, z, z]
    in_specs.append(pl.BlockSpec((None, n_ctx, 2 * kw), lambda bb, i: (bb, 0, col_kv)))
    args.append(zc)
    if local:
        tab = jnp.concatenate([cos_t, sin_t], axis=1)
        in_specs += [
            pl.BlockSpec((tq, 2 * HEAD_DIM), lambda bb, i: (i, 0)),
            pl.BlockSpec((WINDOW, 2 * HEAD_DIM), lambda bb, i: (prev(bb, i), 0)),
            pl.BlockSpec((WINDOW, 2 * HEAD_DIM), lambda bb, i: (nxt(bb, i), 0)),
        ]
        args += [tab, tab, tab]
    return in_specs, args


def _merge_kernel(*refs, n_attn, tm, n_t, d, n_kv, seq, local):
    attn_refs = refs[:n_attn]
    (h_ref, gate_ref, hf_ref, hb_ref, rg_ref, sb_ref, cg_ref, cgp_ref, cgn_ref, sx_ref, sxp_ref, sxn_ref,
     g_ref, bm_ref, scw_ref, wb_ref, wo_ref, o_ref, ppad_ref, att_ref) = refs[n_attn:]
    i = pl.program_id(1)
    ya = (hf_ref[...].astype(F32) + hb_ref[...].astype(F32)) * _gelu_tanh(rg_ref[...].astype(F32))

    _fill_padded(ppad_ref,
                 cg_ref[...].astype(F32) * sx_ref[...].astype(F32),
                 (cgp_ref[...].astype(F32) * sxp_ref[...].astype(F32))[8:16],
                 (cgn_ref[...].astype(F32) * sxn_ref[...].astype(F32))[0:8],
                 i > 0, i < n_t - 1, tm)
    conv = None
    for k in range(SC_CONV):
        off = 8 - SC_CONV_LEFT + k
        term = scw_ref[k:k + 1, :] * ppad_ref[off:off + tm, :]
        conv = term if conv is None else conv + term
    yb = sb_ref[...].astype(F32) * conv

    def lift(br, y):
        gates = jax.nn.sigmoid(g_ref[:, br * d:(br + 1) * d].astype(F32) + bm_ref[br:br + 1, :])
        return gates * jnp.dot(y, wb_ref[br], preferred_element_type=F32)

    lifted = []
    _attn_kernel(*attn_refs, att_ref, tq=tm, n_kv=n_kv, seq=seq, local=local,
                 between=(lambda: lifted.append(lift(0, ya.astype(BF16))),
                          lambda: lifted.append(lift(1, yb.astype(BF16)))))
    merged = lifted[0] + lifted[1] + lift(2, att_ref[...])
    y = jnp.dot(merged.astype(BF16), wo_ref[...], preferred_element_type=F32)
    o_ref[...] = h_ref[...] + gate_ref[...] * y


def _merge_call(h, gate, hf, hb, z, zc, cols, attn_cols, sink, cos_t, sin_t, local, b_merge, sc_w,
                w_branch, w_out, layer):
    b, l, d = h.shape
    bw = hf.shape[2]
    col_g, col_rg, col_sb, col_cg, col_sx = cols
    tm = _largest_tile(l, MIX_TOKENS, WINDOW)
    n_t = l // tm
    attn_specs, attn_args = _attn_operands(sink, z, zc, attn_cols, cos_t, sin_t, local, tm)
    hb_per = tm // HALO
    n_h = l // HALO
    per_batch = gate.shape[0] == b
    tile = lambda col: (lambda bb, i: (bb, i, col))
    prev = lambda col: (lambda bb, i: (bb, jnp.maximum(i * hb_per - 1, 0), col))
    nxt = lambda col: (lambda bb, i: (bb, jnp.minimum((i + 1) * hb_per, n_h - 1), col))
    once = pl.Buffered(1)
    in_specs = [
        pl.BlockSpec((None, tm, d), tile(0)),
        pl.BlockSpec((None, 1, d), lambda bb, i: (bb if per_batch else 0, 0, 0)),
        pl.BlockSpec((None, tm, bw), tile(0)),
        pl.BlockSpec((None, tm, bw), tile(0)),
        pl.BlockSpec((None, tm, bw), tile(col_rg)),
        pl.BlockSpec((None, tm, bw), tile(col_sb)),
        pl.BlockSpec((None, tm, bw), tile(col_cg)),
        pl.BlockSpec((None, HALO, bw), prev(col_cg)),
        pl.BlockSpec((None, HALO, bw), nxt(col_cg)),
        pl.BlockSpec((None, tm, bw), tile(col_sx)),
        pl.BlockSpec((None, HALO, bw), prev(col_sx)),
        pl.BlockSpec((None, HALO, bw), nxt(col_sx)),
        pl.BlockSpec((None, tm, N_BRANCH * d), tile(col_g)),
        pl.BlockSpec((N_BRANCH, d), lambda bb, i: (0, 0)),
        pl.BlockSpec((SC_CONV, bw), lambda bb, i: (0, 0)),
        pl.BlockSpec((None, N_BRANCH, bw, d), lambda bb, i: (layer, 0, 0, 0), pipeline_mode=once),
        pl.BlockSpec((None, d, d), lambda bb, i: (layer, 0, 0), pipeline_mode=once),
    ]
    return pl.pallas_call(
        functools.partial(_merge_kernel, n_attn=len(attn_specs), tm=tm, n_t=n_t, d=d,
                          n_kv=sink.shape[0] // GROUP, seq=l, local=local),
        grid=(b, n_t),
        in_specs=attn_specs + in_specs,
        out_specs=pl.BlockSpec((None, tm, d), tile(0)),
        out_shape=jax.ShapeDtypeStruct((b, l, d), F32),
        scratch_shapes=[pltpu.VMEM((tm + 16, bw), F32), pltpu.VMEM((tm, bw), BF16)],
        compiler_params=_cparams(("parallel", "parallel")),
        name="merge_local" if local else "merge_ctx",
    )(*attn_args, h, gate, hf, hb, z, z, z, z, z, z, z, z, z, b_merge, sc_w, w_branch, w_out)


def _prep_ffn(w13, w2):
    f = w2.shape[1]
    tf = min(FFN_CHUNK, f // LANES * LANES)
    fm = f // tf * tf
    tails = None
    if fm < f:
        tails = (w13[:, :, fm:f].astype(BF16), w13[:, :, f + fm:].astype(BF16), w2[:, fm:].astype(BF16))
    return w13.astype(BF16), w13[:, :, f:f + fm].astype(BF16), w2.astype(BF16), tails


def _cast_kernel(w_ref, o_ref):
    o_ref[...] = w_ref[...].astype(o_ref.dtype)


def _prep_w_in(w_in, d):
    depth, _, nc = w_in.shape
    n_g = N_BRANCH * d
    tn = _largest_tile(math.gcd(n_g, nc - n_g), CAST_COLS, LANES)
    n_blk, g_blk = nc // tn, n_g // tn
    return pl.pallas_call(
        _cast_kernel,
        grid=(depth, n_blk),
        in_specs=[pl.BlockSpec((None, d, tn), lambda l, j: (l, 0, (j + n_blk - g_blk) % n_blk))],
        out_specs=pl.BlockSpec((None, d, tn), lambda l, j: (l, 0, j)),
        out_shape=jax.ShapeDtypeStruct(w_in.shape, BF16),
        compiler_params=_cparams(("parallel", "parallel")),
        name="cast_w_in",
    )(w_in)


def _rope_tables(l):
    pos = jnp.arange(l)
    row = (pos // GRID_W).astype(F32)
    col = (pos % GRID_W).astype(F32)
    half = HEAD_DIM // 2
    inv = ROPE_BASE ** (-jnp.arange(0, half, 2, dtype=F32) / half)
    ar, ac = row[:, None] * inv, col[:, None] * inv
    cos_t = jnp.concatenate([jnp.cos(ar), jnp.cos(ar), jnp.cos(ac), jnp.cos(ac)], axis=-1)
    sin_t = jnp.concatenate([-jnp.sin(ar), jnp.sin(ar), -jnp.sin(ac), jnp.sin(ac)], axis=-1)
    return cos_t, sin_t


def kernel(x, c, ctx, c_ctx, ada_w, ada_b, norm_g, ffn1_w13, ffn1_w2, w_in, b_merge, rnn_conv_w,
           rnn_conv_b, lru_w_a, lru_b_a, lru_w_x, lru_b_x, lru_lambda, sc_conv_w, attn_sink, w_branch,
           w_out, ffn2_w13, ffn2_w2, final_norm_g):
    b, l, d = x.shape
    n_ctx = ctx.shape[1]
    depth = ada_w.shape[0]
    bw = w_branch.shape[2]
    n_heads = attn_sink.shape[1]
    kw = (n_heads // GROUP) * HEAD_DIM
    assert bw == n_heads * HEAD_DIM and w_in.shape[2] == 6 * bw + 2 * kw + N_BRANCH * d
    assert (N_BRANCH * d) % bw == 0 and (N_BRANCH * d + 6 * bw) % kw == 0 and b + 1 <= 8
    g_blocks = N_BRANCH * d // bw
    col_rx, col_rg, col_sb, col_cg, col_sx, col_q = (g_blocks + n for n in range(6))
    col_k = (N_BRANCH * d + 6 * bw) // kw
    col_v = col_k + 1

    cvec = jnp.zeros((8, d), F32).at[:b].set(c).at[b].set(c_ctx)
    mod = _ada_call(cvec, ada_w, ada_b).reshape(depth, 8, N_MOD, 1, d)
    cos_t, sin_t = _rope_tables(l)
    w_in_p = _prep_w_in(w_in, d)
    ffn_a = _prep_ffn(ffn1_w13, ffn1_w2)
    ffn_b = _prep_ffn(ffn2_w13, ffn2_w2)
    wbr, wo = w_branch.astype(BF16), w_out.astype(BF16)

    h = x.reshape(b * l, d)
    hc = ctx.reshape(b * n_ctx, d)
    for layer in range(depth):
        last = layer == depth - 1
        ml = mod[layer, :b]
        mc = mod[layer, b:b + 1]
        ng = norm_g[layer]
        wa, wx = lru_w_a[layer].astype(BF16), lru_w_x[layer].astype(BF16)

        h = _ffn_call(h, ng[0], ml[:, 0], ml[:, 1], ml[:, 2], ffn_a, layer)
        hc = _ffn_call(hc, ng[0], mc[:, 0], mc[:, 1], mc[:, 2], ffn_a, layer)

        z = _inproj_call(h, ng[1], ml[:, 3], ml[:, 4], w_in_p, layer).reshape(b, l, -1)
        zc = _inproj_call(hc, ng[1], mc[:, 3], mc[:, 4], w_in_p, layer).reshape(b, n_ctx, -1)
        lru = (rnn_conv_w[layer], rnn_conv_b[layer], wa, lru_b_a[layer], wx, lru_b_x[layer], lru_lambda[layer])
        hcf, hcb, h_last = _rglru_call(zc, col_rx, *lru, jnp.zeros((b, 2, bw), F32))
        hlf, hlb, _ = _rglru_call(z, col_rx, *lru, h_last)
        merge_cols = (0, col_rg, col_sb, col_cg, col_sx)
        attn_cols = (col_q, col_k, col_v)
        mix = (attn_sink[layer], cos_t, sin_t)
        mix_w = (b_merge[layer], sc_conv_w[layer], wbr, wo, layer)
        h = _merge_call(h.reshape(b, l, d), ml[:, 5], hlf, hlb, z, zc, merge_cols, attn_cols, *mix, True,
                        *mix_w).reshape(b * l, d)

        h = _ffn_call(h, ng[2], ml[:, 6], ml[:, 7], ml[:, 8], ffn_b, layer,
                      final_g=final_norm_g if last else None)
        if not last:
            hc = _merge_call(hc.reshape(b, n_ctx, d), mc[:, 5], hcf, hcb, zc, zc, merge_cols, attn_cols, *mix,
                             False, *mix_w).reshape(b * n_ctx, d)
            hc = _ffn_call(hc, ng[2], mc[:, 6], mc[:, 7], mc[:, 8], ffn_b, layer)
    return h.reshape(b, l, d)
```

```python
import functools
import math

import jax
import jax.numpy as jnp
from jax import lax
from jax.experimental import pallas as pl
from jax.experimental.pallas import tpu as pltpu

F32 = jnp.float32
BF16 = jnp.bfloat16

HEAD_DIM = 128
GROUP = 4
WINDOW = 128
GRID_W = 64
ROPE_BASE = 10000.0
RNN_BLOCK = 128
RNN_CONV = 4
RNN_CONV_LEFT = 2
SC_CONV = 3
SC_CONV_LEFT = 1
LRU_C = 8.0
N_BRANCH = 3
N_MOD = 9
EPS = 1e-6
NEG_INF = -1e30
LOG2E = 1.4426950408889634

LANES = 128
SUBLANES_F32 = 8
SUBLANES_BF16 = 16
HALO = SUBLANES_BF16
VMEM_LIMIT_BYTES = 56 * 1024 * 1024
FFN_CHUNK = 1024
FFN_TOKENS = 512
INPROJ_TOKENS = 1024
INPROJ_COLS = 1280
LRU_TOKENS = 512
MIX_TOKENS = 256
CAST_COLS = 512
ADALN_COLS = 1024


def _cparams(semantics):
    return pltpu.CompilerParams(dimension_semantics=semantics, vmem_limit_bytes=VMEM_LIMIT_BYTES)


def _pow2_slices(n_steps, tm):
    n = 0
    while 2 * max(n, 1) <= n_steps and tm % (2 * max(n, 1) * SUBLANES_BF16) == 0:
        n = 2 * max(n, 1)
    return n


def _largest_tile(n, cap, quantum):
    best = None
    t = quantum
    while t <= min(n, cap):
        if n % t == 0:
            best = t
        t += quantum
    assert best is not None, (n, cap, quantum)
    return best


def _rmsnorm(x, g):
    return x * lax.rsqrt(jnp.mean(x * x, axis=-1, keepdims=True) + EPS) * g


def _norm_mod(x, g, shift, scale):
    return _rmsnorm(x, g) * (1.0 + scale) + shift


def _silu(x):
    return x * jax.nn.sigmoid(x)


def _gelu_tanh(x):
    return 0.5 * x * (1.0 + jnp.tanh(0.7978845608028654 * (x + 0.044715 * (x * x * x))))


def _softplus(x):
    return jnp.maximum(x, 0.0) + jnp.log1p(jnp.exp(-jnp.abs(x)))


def _one_minus_exp2(x, ex):
    kahan = (ex - 1.0) * x / jnp.log(ex)
    em1 = jnp.where(x < -1.0, ex - 1.0, jnp.where(ex == 1.0, x, kahan))
    return -em1 * (ex + 1.0)


def _ada_kernel(c_ref, w_ref, b_ref, o_ref):
    s = _silu(c_ref[...]).astype(BF16)
    o_ref[...] = jnp.dot(s, w_ref[...].astype(BF16), preferred_element_type=F32) + b_ref[...]


def _ada_call(cvec, ada_w, ada_b):
    depth, d, nm = ada_w.shape
    tn = _largest_tile(nm, ADALN_COLS, LANES)
    return pl.pallas_call(
        _ada_kernel,
        grid=(depth, nm // tn),
        in_specs=[
            pl.BlockSpec((8, d), lambda l, j: (0, 0)),
            pl.BlockSpec((None, d, tn), lambda l, j: (l, 0, j)),
            pl.BlockSpec((None, 1, tn), lambda l, j: (l, 0, j)),
        ],
        out_specs=pl.BlockSpec((None, 8, tn), lambda l, j: (l, 0, j)),
        out_shape=jax.ShapeDtypeStruct((depth, 8, nm), F32),
        compiler_params=_cparams(("arbitrary", "arbitrary")),
        name="adaln",
    )(cvec, ada_w, ada_b.reshape(depth, 1, nm))


def _swiglu_chunk(u, wg_ref, wu_ref, w2_ref):
    gate = jnp.dot(u, wg_ref[...], preferred_element_type=F32)
    up = jnp.dot(u, wu_ref[...], preferred_element_type=F32)
    act = (_silu(gate) * up).astype(BF16)
    return jnp.dot(act, w2_ref[...], preferred_element_type=F32)


def _ffn_kernel(h_ref, g_ref, shift_ref, scale_ref, gate_ref, wg_ref, wu_ref, w2_ref, *rest, tail, final_norm,
                n_chunks):
    rest = list(rest)
    wgt_ref, wut_ref, w2t_ref = (rest.pop(0), rest.pop(0), rest.pop(0)) if tail else (None, None, None)
    fg_ref = rest.pop(0) if final_norm else None
    o_ref, u_ref = rest
    j = pl.program_id(1)
    last = n_chunks - 1

    def first():
        u = _norm_mod(h_ref[...], g_ref[...], shift_ref[...], scale_ref[...]).astype(BF16)
        u_ref[...] = u
        acc = _swiglu_chunk(u, wg_ref, wu_ref, w2_ref)
        return acc + _swiglu_chunk(u, wgt_ref, wut_ref, w2t_ref) if tail else acc

    def finish(acc):
        hn = h_ref[...] + (0.5 * gate_ref[...]) * acc
        if final_norm:
            hn = _rmsnorm(hn, fg_ref[...])
        o_ref[...] = hn

    if last == 0:
        finish(first())
        return

    @pl.when(j == 0)
    def _():
        o_ref[...] = first()

    @pl.when((j > 0) & (j < last))
    def _():
        o_ref[...] += _swiglu_chunk(u_ref[...], wg_ref, wu_ref, w2_ref)

    @pl.when(j == last)
    def _():
        finish(o_ref[...] + _swiglu_chunk(u_ref[...], wg_ref, wu_ref, w2_ref))


def _ffn_call(h, g, shift, scale, gate, weights, layer, final_g=None):
    wg, wu, w2, tails = weights
    m, d = h.shape
    nb = shift.shape[0]
    tf = min(FFN_CHUNK, wu.shape[2])
    n_chunks = wu.shape[2] // tf
    tm = _largest_tile(m // nb, FFN_TOKENS, SUBLANES_F32)
    tpb = (m // nb) // tm
    row = lambda i, j: (i // tpb, 0, 0)
    const = lambda i, j: (0, 0)
    once = pl.Buffered(1)
    in_specs = [
        pl.BlockSpec((tm, d), lambda i, j: (i, 0)),
        pl.BlockSpec((1, d), const),
        pl.BlockSpec((None, 1, d), row),
        pl.BlockSpec((None, 1, d), row),
        pl.BlockSpec((None, 1, d), row),
        pl.BlockSpec((None, d, tf), lambda i, j: (layer, 0, j)),
        pl.BlockSpec((None, d, tf), lambda i, j: (layer, 0, j)),
        pl.BlockSpec((None, tf, d), lambda i, j: (layer, j, 0)),
    ]
    args = [h, g.reshape(1, d), shift, scale, gate, wg, wu, w2]
    if tails is not None:
        for w in tails:
            in_specs.append(pl.BlockSpec((None,) + w.shape[1:], lambda i, j: (layer, 0, 0), pipeline_mode=once))
            args.append(w)
    if final_g is not None:
        in_specs.append(pl.BlockSpec((1, d), const))
        args.append(final_g.reshape(1, d))
    return pl.pallas_call(
        functools.partial(_ffn_kernel, tail=tails is not None, final_norm=final_g is not None,
                          n_chunks=n_chunks),
        grid=(m // tm, n_chunks),
        in_specs=in_specs,
        out_specs=pl.BlockSpec((tm, d), lambda i, j: (i, 0)),
        out_shape=jax.ShapeDtypeStruct((m, d), F32),
        scratch_shapes=[pltpu.VMEM((tm, d), BF16)],
        compiler_params=_cparams(("parallel", "arbitrary")),
        name="ffn",
    )(*args)


def _inproj_kernel(h_ref, g_ref, shift_ref, scale_ref, w_ref, o_ref, ua_ref, ub_ref, *, n_slices):
    i, j = pl.program_id(0), pl.program_id(1)
    tm = h_ref.shape[0]
    norm = lambda x: _norm_mod(x, g_ref[...], shift_ref[...], scale_ref[...]).astype(BF16)

    def step(cur_ref, nxt_ref):
        @pl.when((j == 0) & ((i == 0) | (n_slices == 0)))
        def _():
            cur_ref[...] = norm(h_ref[...])

        if n_slices:
            rows = tm // n_slices
            r0 = pl.multiple_of(jnp.clip(j - 1, 0, n_slices - 1) * rows, rows)
            nxt_ref[pl.ds(r0, rows), :] = norm(h_ref[pl.ds(r0, rows), :])
        o_ref[...] = jnp.dot(cur_ref[...], w_ref[...], preferred_element_type=F32).astype(o_ref.dtype)

    pl.when(i % 2 == 0)(lambda: step(ua_ref, ub_ref))
    pl.when(i % 2 == 1)(lambda: step(ub_ref, ua_ref))


def _inproj_call(h, g, shift, scale, w, layer, col_ranges=None):
    m, d = h.shape
    nb = shift.shape[0]
    nc = w.shape[2]
    tm = _largest_tile(m // nb, INPROJ_TOKENS, SUBLANES_BF16)
    tpb = (m // nb) // tm
    tn = _largest_tile(nc, INPROJ_COLS, LANES)
    tiles = tuple(range(nc // tn))
    if col_ranges is not None:
        tiles = tuple(sorted({t for a, b in col_ranges for t in range(a // tn, (b - 1) // tn + 1)}))

    def col(j):
        c = tiles[-1]
        for k in range(len(tiles) - 2, -1, -1):
            c = jnp.where(j == k, tiles[k], c)
        return j if col_ranges is None else c

    n_i, n_j = m // tm, len(tiles)
    n_slices = _pow2_slices(n_j - 1, tm)
    if n_slices:
        tile = lambda i, j: jnp.minimum(i + jnp.where(j > 0, 1, 0), n_i - 1)
    else:
        tile = lambda i, j: i
    row = lambda i, j: (tile(i, j) // tpb, 0, 0)
    return pl.pallas_call(
        functools.partial(_inproj_kernel, n_slices=n_slices),
        grid=(n_i, n_j),
        in_specs=[
            pl.BlockSpec((tm, d), lambda i, j: (tile(i, j), 0)),
            pl.BlockSpec((1, d), lambda i, j: (0, 0)),
            pl.BlockSpec((None, 1, d), row),
            pl.BlockSpec((None, 1, d), row),
            pl.BlockSpec((None, d, tn), lambda i, j: (layer, 0, col(j))),
        ],
        out_specs=pl.BlockSpec((tm, tn), lambda i, j: (i, col(j))),
        out_shape=jax.ShapeDtypeStruct((m, nc), BF16),
        scratch_shapes=[pltpu.VMEM((tm, d), BF16), pltpu.VMEM((tm, d), BF16)],
        compiler_params=_cparams(("arbitrary", "arbitrary")),
        name="in_proj",
    )(h, g.reshape(1, d), shift, scale, w)


def _fill_padded(pad_ref, main, prev_tail, next_head, has_prev, has_next, t):
    pad_ref[0:8, :] = jnp.where(has_prev, prev_tail, 0.0)
    pad_ref[8:8 + t, :] = main
    pad_ref[8 + t:16 + t, :] = jnp.where(has_next, next_head, 0.0)


def _segment_perm(t):
    r = lax.broadcasted_iota(jnp.int32, (t, t), 0)
    c = lax.broadcasted_iota(jnp.int32, (t, t), 1)
    return (c == (r % 8) * (t // 8) + r // 8).astype(BF16)


def _rglru_kernel(zf_ref, zfp_ref, zfn_ref, zb_ref, zbp_ref, zbn_ref, perm_ref, permt_ref, cw_ref, cb_ref,
                  wa_ref, ba_ref, wx_ref, bx_ref, lam_ref, h0_ref, of_ref, ob_ref, last_ref,
                  xp_ref, xa_ref, a_ref, u_ref, hs_ref, pc_ref, c8_ref, car_ref, *, t, n_t, n_blk):
    i = pl.program_id(1)
    seg = t // 8

    @pl.when(i == 0)
    def _():
        car_ref[...] = h0_ref[...]

    sp = _softplus(-lam_ref[...])
    sub = lax.broadcasted_iota(jnp.int32, (8, 1), 0)

    def gates(d, z_ref, zp_ref, zn_ref, tile):
        xq = jnp.dot(perm_ref[...], z_ref[...], preferred_element_type=F32)
        prev = jnp.where(tile > 0, zp_ref[...].astype(F32), 0.0)
        nxt = jnp.where(tile < n_t - 1, zn_ref[...].astype(F32), 0.0)
        xp_ref[0:8, :] = jnp.where(sub == 0, prev[14:15], pltpu.roll(xq[(seg - 2) * 8:(seg - 1) * 8], 1, 0))
        xp_ref[8:16, :] = jnp.where(sub == 0, prev[15:16], pltpu.roll(xq[(seg - 1) * 8:seg * 8], 1, 0))
        xp_ref[16:16 + t, :] = xq
        xp_ref[16 + t:24 + t, :] = jnp.where(sub == 7, nxt[0:1], pltpu.roll(xq[0:8], 7, 0))
        xa = cb_ref[...]
        for k in range(RNN_CONV):
            off = 8 * (2 - RNN_CONV_LEFT + k)
            xa = xa + cw_ref[k:k + 1, :] * xp_ref[off:off + t, :]
        xa_ref[...] = xa
        for n in range(n_blk):
            sl = slice(n * RNN_BLOCK, (n + 1) * RNN_BLOCK)
            xs = xa_ref[:, sl]
            xb = xs.astype(BF16)
            rg = jax.nn.sigmoid(jnp.dot(xb, wa_ref[d, n], preferred_element_type=F32) + ba_ref[d:d + 1, sl])
            ig = jax.nn.sigmoid(jnp.dot(xb, wx_ref[d, n], preferred_element_type=F32) + bx_ref[d:d + 1, sl])
            log_a = (-LRU_C * rg) * sp[d:d + 1, sl]
            a = jnp.exp(log_a)
            a_ref[d, :, sl] = a
            u_ref[d, :, sl] = jnp.sqrt(_one_minus_exp2(log_a, a)) * (ig * xs)

    gates(0, zf_ref, zfp_ref, zfn_ref, i)
    gates(1, zb_ref, zbp_ref, zbn_ref, n_t - 1 - i)

    def step(j, carry):
        hf, pf, hb, pb = carry
        rf = pl.multiple_of(j * 8, 8)
        rb = pl.multiple_of((seg - 1 - j) * 8, 8)
        af, ab = a_ref[0, pl.ds(rf, 8), :], a_ref[1, pl.ds(rb, 8), :]
        hf = af * hf + u_ref[0, pl.ds(rf, 8), :]
        hb = ab * hb + u_ref[1, pl.ds(rb, 8), :]
        pf, pb = pf * af, pb * ab
        hs_ref[0, pl.ds(rf, 8), :] = hf
        hs_ref[1, pl.ds(rb, 8), :] = hb
        pc_ref[0, pl.ds(rf, 8), :] = pf
        pc_ref[1, pl.ds(rb, 8), :] = pb
        return hf, pf, hb, pb

    zeros, ones = jnp.zeros((8, a_ref.shape[2]), F32), jnp.ones((8, a_ref.shape[2]), F32)
    lax.fori_loop(0, seg, step, (zeros, ones, zeros, ones), unroll=4)

    lf, qf = hs_ref[0, (seg - 1) * 8:seg * 8, :], pc_ref[0, (seg - 1) * 8:seg * 8, :]
    lb, qb = hs_ref[1, 0:8, :], pc_ref[1, 0:8, :]
    cf, cb = car_ref[0:1, :], car_ref[1:2, :]
    for s in range(8):
        c8_ref[0, s:s + 1, :] = cf
        cf = lf[s:s + 1] + qf[s:s + 1] * cf
        c8_ref[1, 7 - s:8 - s, :] = cb
        cb = lb[7 - s:8 - s] + qb[7 - s:8 - s] * cb
    car_ref[0:1, :] = cf
    car_ref[1:2, :] = cb
    last_ref[0:1, :] = cf
    last_ref[1:2, :] = cb

    for d, o_ref in ((0, of_ref), (1, ob_ref)):
        h = hs_ref[d] + pc_ref[d] * jnp.tile(c8_ref[d], (seg, 1))
        o_ref[...] = jnp.dot(permt_ref[...], h.astype(BF16), preferred_element_type=F32).astype(o_ref.dtype)


def _rglru_call(z, col_rx, cw, cb, wa, ba, wx, bx, lam, h0):
    b, l, _ = z.shape
    c = cw.shape[1]
    n_blk = c // RNN_BLOCK
    t = _largest_tile(l, LRU_TOKENS, HALO)
    n_t = l // t
    hb_per = t // HALO
    n_h = l // HALO

    def main(rev):
        return lambda bb, i: (bb, (n_t - 1 - i) if rev else i, col_rx)

    def prev(rev):
        return lambda bb, i: (bb, jnp.maximum(((n_t - 1 - i) if rev else i) * hb_per - 1, 0), col_rx)

    def nxt(rev):
        return lambda bb, i: (bb, jnp.minimum((((n_t - 1 - i) if rev else i) + 1) * hb_per, n_h - 1), col_rx)

    const2 = lambda bb, i: (0, 0)
    const4 = lambda bb, i: (0, 0, 0, 0)
    in_specs = []
    for rev in (False, True):
        in_specs += [pl.BlockSpec((None, t, c), main(rev)),
                     pl.BlockSpec((None, HALO, c), prev(rev)),
                     pl.BlockSpec((None, HALO, c), nxt(rev))]
    perm = _segment_perm(t)
    in_specs += [
        pl.BlockSpec((t, t), const2, pipeline_mode=pl.Buffered(1)),
        pl.BlockSpec((t, t), const2, pipeline_mode=pl.Buffered(1)),
        pl.BlockSpec((RNN_CONV, c), const2),
        pl.BlockSpec((1, c), const2),
        pl.BlockSpec((2, n_blk, RNN_BLOCK, RNN_BLOCK), const4),
        pl.BlockSpec((2, c), const2),
        pl.BlockSpec((2, n_blk, RNN_BLOCK, RNN_BLOCK), const4),
        pl.BlockSpec((2, c), const2),
        pl.BlockSpec((2, c), const2),
        pl.BlockSpec((None, 2, c), lambda bb, i: (bb, 0, 0)),
    ]
    return pl.pallas_call(
        functools.partial(_rglru_kernel, t=t, n_t=n_t, n_blk=n_blk),
        grid=(b, n_t),
        in_specs=in_specs,
        out_specs=[
            pl.BlockSpec((None, t, c), lambda bb, i: (bb, i, 0)),
            pl.BlockSpec((None, t, c), lambda bb, i: (bb, n_t - 1 - i, 0)),
            pl.BlockSpec((None, 2, c), lambda bb, i: (bb, 0, 0)),
        ],
        out_shape=[
            jax.ShapeDtypeStruct((b, l, c), BF16),
            jax.ShapeDtypeStruct((b, l, c), BF16),
            jax.ShapeDtypeStruct((b, 2, c), F32),
        ],
        scratch_shapes=[
            pltpu.VMEM((t + 24, c), F32),
            pltpu.VMEM((t, c), F32),
            pltpu.VMEM((2, t, c), F32),
            pltpu.VMEM((2, t, c), F32),
            pltpu.VMEM((2, t, c), F32),
            pltpu.VMEM((2, t, c), F32),
            pltpu.VMEM((2, 8, c), F32),
            pltpu.VMEM((2, c), F32),
        ],
        compiler_params=_cparams(("parallel", "arbitrary")),
        name="rglru",
    )(z, z, z, z, z, z, perm, perm.T, cw, cb.reshape(1, c), wa, ba, wx, bx, lam, h0)


def _rope(x, cos_t, sin_t):
    lane = lax.broadcasted_iota(jnp.int32, x.shape, 1)
    partner = jnp.where((lane & 32) == 0, pltpu.roll(x, LANES - 32, 1), pltpu.roll(x, 32, 1))
    return x * cos_t + partner * sin_t


def _attn_kernel(sink_ref, q_ref, *rest, tq, n_kv, seq, local, between=(lambda: None, lambda: None)):
    if local:
        (km_ref, kp_ref, kn_ref, vm_ref, vp_ref, vn_ref, kc_ref, vc_ref,
         cm_ref, cp_ref, cn_ref, sm_ref, sp_ref, sn_ref, o_ref) = rest
    else:
        kc_ref, vc_ref, o_ref = rest
    i = pl.program_id(1)
    q_scale = HEAD_DIM ** -0.5 * LOG2E
    nt = (((1,), (1,)), ((), ()))
    sub = WINDOW
    n_ctx = kc_ref.shape[0]

    if local:
        cos_q, sin_q = cm_ref[...], sm_ref[...]
        cos_k = jnp.concatenate([cp_ref[...], cos_q, cn_ref[...]], axis=0)
        sin_k = jnp.concatenate([sp_ref[...], sin_q, sn_ref[...]], axis=0)
        r = lax.broadcasted_iota(jnp.int32, (sub, 3 * sub + n_ctx), 0)
        c = lax.broadcasted_iota(jnp.int32, (sub, 3 * sub + n_ctx), 1)
        band = (c >= r) & (c <= r + 2 * WINDOW)
        is_ctx = c >= 3 * sub

    units = []
    for kh in range(n_kv):
        ks = slice(kh * HEAD_DIM, (kh + 1) * HEAD_DIM)
        kc = kc_ref[:, ks]
        vc = vc_ref[:, ks]
        heads = [kh * GROUP + g for g in range(GROUP)]
        qh = []
        for h in heads:
            x = q_ref[:, h * HEAD_DIM:(h + 1) * HEAD_DIM].astype(F32)
            if local:
                x = _rope(x, cos_q, sin_q)
            qh.append((x * q_scale).astype(BF16))
        if local:
            k_span = jnp.concatenate([kp_ref[:, ks], km_ref[:, ks], kn_ref[:, ks]], axis=0)
            k_span = _rope(k_span.astype(F32), cos_k, sin_k).astype(BF16)
            v_span = jnp.concatenate([vp_ref[:, ks], vm_ref[:, ks], vn_ref[:, ks]], axis=0)
        sink2 = jnp.concatenate([jnp.full((sub, 1), sink_ref[h] * LOG2E, F32) for h in heads], axis=0)

        for sb in range(tq // sub):
            rows = slice(sb * sub, (sb + 1) * sub)
            qs = jnp.concatenate([x[rows] for x in qh], axis=0)
            if local:
                keys = jnp.concatenate([k_span[sb * sub:(sb + 3) * sub], kc], axis=0)
                vals = jnp.concatenate([v_span[sb * sub:(sb + 3) * sub], vc], axis=0)
                kpos = i * tq + (sb - 1) * sub + c
                valid = is_ctx | (band & (kpos >= 0) & (kpos < seq))
                bias = jnp.concatenate([jnp.where(valid, 0.0, NEG_INF)] * GROUP, axis=0)
            else:
                keys, vals, bias = kc, vc, None
            units.append(dict(rows=rows, heads=heads, qs=qs, keys=keys, vals=vals, bias=bias, sink2=sink2))

    for u in units:
        s = lax.dot_general(u["qs"], u["keys"], nt, preferred_element_type=F32)
        u["s"] = s if u["bias"] is None else s + u["bias"]
    between[0]()
    for u in units:
        m = jnp.maximum(jnp.max(u["s"], axis=-1, keepdims=True), u["sink2"])
        p = jnp.exp2(u["s"] - m)
        u["den"] = jnp.sum(p, axis=-1, keepdims=True) + jnp.exp2(u["sink2"] - m)
        u["p"] = p.astype(BF16)
    between[1]()
    for u in units:
        o = jnp.dot(u["p"], u["vals"], preferred_element_type=F32) / u["den"]
        for g, h in enumerate(u["heads"]):
            o_ref[u["rows"], h * HEAD_DIM:(h + 1) * HEAD_DIM] = o[g * sub:(g + 1) * sub].astype(o_ref.dtype)


def _attn_operands(sink, z, zc, cols, cos_t, sin_t, local, tq):
    l = z.shape[1]
    n_ctx = zc.shape[1]
    n_heads = sink.shape[0]
    n_kv = n_heads // GROUP
    qw, kw = n_heads * HEAD_DIM, n_kv * HEAD_DIM
    col_q, col_k, col_v = cols
    assert tq % WINDOW == 0
    wb = tq // WINDOW
    n_w = l // WINDOW
    prev = lambda bb, i: jnp.maximum(i * wb - 1, 0)
    nxt = lambda bb, i: jnp.minimum((i + 1) * wb, n_w - 1)
    in_specs = [
        pl.BlockSpec(memory_space=pltpu.SMEM),
        pl.BlockSpec((None, tq, qw), lambda bb, i: (bb, i, col_q)),
    ]
    args = [sink, z]
    if local:
        for col in (col_k, col_v):
            in_specs += [
                pl.BlockSpec((None, tq, kw), lambda bb, i, col=col: (bb, i, col)),
                pl.BlockSpec((None, WINDOW, kw), lambda bb, i, col=col: (bb, prev(bb, i), col)),
                pl.BlockSpec((None, WINDOW, kw), lambda bb, i, col=col: (bb, nxt(bb, i), col)),
            ]
            args += [z, z, z]
    in_specs += [
        pl.BlockSpec((None, n_ctx, kw), lambda bb, i: (bb, 0, col_k)),
        pl.BlockSpec((None, n_ctx, kw), lambda bb, i: (bb, 0, col_v)),
    ]
    args += [zc, zc]
    if local:
        for tab in (cos_t, sin_t):
            in_specs += [
                pl.BlockSpec((tq, HEAD_DIM), lambda bb, i: (i, 0)),
                pl.BlockSpec((WINDOW, HEAD_DIM), lambda bb, i: (prev(bb, i), 0)),
                pl.BlockSpec((WINDOW, HEAD_DIM), lambda bb, i: (nxt(bb, i), 0)),
            ]
            args += [tab, tab, tab]
    return in_specs, args


def _merge_kernel(*refs, n_attn, tm, n_t, d, n_kv, seq, local):
    attn_refs = refs[:n_attn]
    (h_ref, gate_ref, hf_ref, hb_ref, rg_ref, sb_ref, cg_ref, cgp_ref, cgn_ref, sx_ref, sxp_ref, sxn_ref,
     g_ref, bm_ref, scw_ref, wb_ref, wo_ref, o_ref, ppad_ref, att_ref) = refs[n_attn:]
    i = pl.program_id(1)
    ya = (hf_ref[...].astype(F32) + hb_ref[...].astype(F32)) * _gelu_tanh(rg_ref[...].astype(F32))

    _fill_padded(ppad_ref,
                 cg_ref[...].astype(F32) * sx_ref[...].astype(F32),
                 (cgp_ref[...].astype(F32) * sxp_ref[...].astype(F32))[8:16],
                 (cgn_ref[...].astype(F32) * sxn_ref[...].astype(F32))[0:8],
                 i > 0, i < n_t - 1, tm)
    conv = None
    for k in range(SC_CONV):
        off = 8 - SC_CONV_LEFT + k
        term = scw_ref[k:k + 1, :] * ppad_ref[off:off + tm, :]
        conv = term if conv is None else conv + term
    yb = sb_ref[...].astype(F32) * conv

    def lift(br, y):
        gates = jax.nn.sigmoid(g_ref[:, br * d:(br + 1) * d].astype(F32) + bm_ref[br:br + 1, :])
        return gates * jnp.dot(y, wb_ref[br], preferred_element_type=F32)

    lifted = []
    _attn_kernel(*attn_refs, att_ref, tq=tm, n_kv=n_kv, seq=seq, local=local,
                 between=(lambda: lifted.append(lift(0, ya.astype(BF16))),
                          lambda: lifted.append(lift(1, yb.astype(BF16)))))
    merged = lifted[0] + lifted[1] + lift(2, att_ref[...])
    y = jnp.dot(merged.astype(BF16), wo_ref[...], preferred_element_type=F32)
    o_ref[...] = h_ref[...] + gate_ref[...] * y


def _merge_call(h, gate, hf, hb, z, zc, cols, attn_cols, sink, cos_t, sin_t, local, b_merge, sc_w,
                w_branch, w_out, layer):
    b, l, d = h.shape
    bw = hf.shape[2]
    col_g, col_rg, col_sb, col_cg, col_sx = cols
    tm = _largest_tile(l, MIX_TOKENS, WINDOW)
    n_t = l // tm
    attn_specs, attn_args = _attn_operands(sink, z, zc, attn_cols, cos_t, sin_t, local, tm)
    hb_per = tm // HALO
    n_h = l // HALO
    per_batch = gate.shape[0] == b
    tile = lambda col: (lambda bb, i: (bb, i, col))
    prev = lambda col: (lambda bb, i: (bb, jnp.maximum(i * hb_per - 1, 0), col))
    nxt = lambda col: (lambda bb, i: (bb, jnp.minimum((i + 1) * hb_per, n_h - 1), col))
    once = pl.Buffered(1)
    in_specs = [
        pl.BlockSpec((None, tm, d), tile(0)),
        pl.BlockSpec((None, 1, d), lambda bb, i: (bb if per_batch else 0, 0, 0)),
        pl.BlockSpec((None, tm, bw), tile(0)),
        pl.BlockSpec((None, tm, bw), tile(0)),
        pl.BlockSpec((None, tm, bw), tile(col_rg)),
        pl.BlockSpec((None, tm, bw), tile(col_sb)),
        pl.BlockSpec((None, tm, bw), tile(col_cg)),
        pl.BlockSpec((None, HALO, bw), prev(col_cg)),
        pl.BlockSpec((None, HALO, bw), nxt(col_cg)),
        pl.BlockSpec((None, tm, bw), tile(col_sx)),
        pl.BlockSpec((None, HALO, bw), prev(col_sx)),
        pl.BlockSpec((None, HALO, bw), nxt(col_sx)),
        pl.BlockSpec((None, tm, N_BRANCH * d), tile(col_g)),
        pl.BlockSpec((N_BRANCH, d), lambda bb, i: (0, 0)),
        pl.BlockSpec((SC_CONV, bw), lambda bb, i: (0, 0)),
        pl.BlockSpec((None, N_BRANCH, bw, d), lambda bb, i: (layer, 0, 0, 0), pipeline_mode=once),
        pl.BlockSpec((None, d, d), lambda bb, i: (layer, 0, 0), pipeline_mode=once),
    ]
    return pl.pallas_call(
        functools.partial(_merge_kernel, n_attn=len(attn_specs), tm=tm, n_t=n_t, d=d,
                          n_kv=sink.shape[0] // GROUP, seq=l, local=local),
        grid=(b, n_t),
        in_specs=attn_specs + in_specs,
        out_specs=pl.BlockSpec((None, tm, d), tile(0)),
        out_shape=jax.ShapeDtypeStruct((b, l, d), F32),
        scratch_shapes=[pltpu.VMEM((tm + 16, bw), F32), pltpu.VMEM((tm, bw), BF16)],
        compiler_params=_cparams(("parallel", "parallel")),
        name="merge_local" if local else "merge_ctx",
    )(*attn_args, h, gate, hf, hb, z, z, z, z, z, z, z, z, z, b_merge, sc_w, w_branch, w_out)


def _prep_ffn(w13, w2):
    f = w2.shape[1]
    tf = min(FFN_CHUNK, f // LANES * LANES)
    fm = f // tf * tf
    tails = None
    if fm < f:
        tails = (w13[:, :, fm:f].astype(BF16), w13[:, :, f + fm:].astype(BF16), w2[:, fm:].astype(BF16))
    return w13.astype(BF16), w13[:, :, f:f + fm].astype(BF16), w2.astype(BF16), tails


def _cast_kernel(w_ref, o_ref):
    o_ref[...] = w_ref[...].astype(o_ref.dtype)


def _prep_w_in(w_in, d):
    depth, _, nc = w_in.shape
    n_g = N_BRANCH * d
    tn = _largest_tile(math.gcd(n_g, nc - n_g), CAST_COLS, LANES)
    n_blk, g_blk = nc // tn, n_g // tn
    return pl.pallas_call(
        _cast_kernel,
        grid=(depth, n_blk),
        in_specs=[pl.BlockSpec((None, d, tn), lambda l, j: (l, 0, (j + n_blk - g_blk) % n_blk))],
        out_specs=pl.BlockSpec((None, d, tn), lambda l, j: (l, 0, j)),
        out_shape=jax.ShapeDtypeStruct(w_in.shape, BF16),
        compiler_params=_cparams(("parallel", "parallel")),
        name="cast_w_in",
    )(w_in)


def _rope_tables(l):
    pos = jnp.arange(l)
    row = (pos // GRID_W).astype(F32)
    col = (pos % GRID_W).astype(F32)
    half = HEAD_DIM // 2
    inv = ROPE_BASE ** (-jnp.arange(0, half, 2, dtype=F32) / half)
    ar, ac = row[:, None] * inv, col[:, None] * inv
    cos_t = jnp.concatenate([jnp.cos(ar), jnp.cos(ar), jnp.cos(ac), jnp.cos(ac)], axis=-1)
    sin_t = jnp.concatenate([-jnp.sin(ar), jnp.sin(ar), -jnp.sin(ac), jnp.sin(ac)], axis=-1)
    return cos_t, sin_t


def kernel(x, c, ctx, c_ctx, ada_w, ada_b, norm_g, ffn1_w13, ffn1_w2, w_in, b_merge, rnn_conv_w,
           rnn_conv_b, lru_w_a, lru_b_a, lru_w_x, lru_b_x, lru_lambda, sc_conv_w, attn_sink, w_branch,
           w_out, ffn2_w13, ffn2_w2, final_norm_g):
    b, l, d = x.shape
    n_ctx = ctx.shape[1]
    depth = ada_w.shape[0]
    bw = w_branch.shape[2]
    n_heads = attn_sink.shape[1]
    kw = (n_heads // GROUP) * HEAD_DIM
    assert bw == n_heads * HEAD_DIM and w_in.shape[2] == 6 * bw + 2 * kw + N_BRANCH * d
    assert (N_BRANCH * d) % bw == 0 and (N_BRANCH * d + 6 * bw) % kw == 0 and b + 1 <= 8
    g_blocks = N_BRANCH * d // bw
    col_rx, col_rg, col_sb, col_cg, col_sx, col_q = (g_blocks + n for n in range(6))
    col_k = (N_BRANCH * d + 6 * bw) // kw
    col_v = col_k + 1

    cvec = jnp.zeros((8, d), F32).at[:b].set(c).at[b].set(c_ctx)
    mod = _ada_call(cvec, ada_w, ada_b).reshape(depth, 8, N_MOD, 1, d)
    cos_t, sin_t = _rope_tables(l)
    w_in_p = _prep_w_in(w_in, d)
    ffn_a = _prep_ffn(ffn1_w13, ffn1_w2)
    ffn_b = _prep_ffn(ffn2_w13, ffn2_w2)
    wbr, wo = w_branch.astype(BF16), w_out.astype(BF16)

    h = x.reshape(b * l, d)
    hc = ctx.reshape(b * n_ctx, d)
    for layer in range(depth):
        last = layer == depth - 1
        ml = mod[layer, :b]
        mc = mod[layer, b:b + 1]
        ng = norm_g[layer]
        wa, wx = lru_w_a[layer].astype(BF16), lru_w_x[layer].astype(BF16)

        h = _ffn_call(h, ng[0], ml[:, 0], ml[:, 1], ml[:, 2], ffn_a, layer)
        hc = _ffn_call(hc, ng[0], mc[:, 0], mc[:, 1], mc[:, 2], ffn_a, layer)

        z = _inproj_call(h, ng[1], ml[:, 3], ml[:, 4], w_in_p, layer).reshape(b, l, -1)
        ctx_cols = ((col_rx * bw, (col_rx + 1) * bw), (col_k * kw, (col_v + 1) * kw)) if last else None
        zc = _inproj_call(hc, ng[1], mc[:, 3], mc[:, 4], w_in_p, layer, ctx_cols).reshape(b, n_ctx, -1)
        lru = (rnn_conv_w[layer], rnn_conv_b[layer], wa, lru_b_a[layer], wx, lru_b_x[layer], lru_lambda[layer])
        hcf, hcb, h_last = _rglru_call(zc, col_rx, *lru, jnp.zeros((b, 2, bw), F32))
        hlf, hlb, _ = _rglru_call(z, col_rx, *lru, h_last)
        merge_cols = (0, col_rg, col_sb, col_cg, col_sx)
        attn_cols = (col_q, col_k, col_v)
        mix = (attn_sink[layer], cos_t, sin_t)
        mix_w = (b_merge[layer], sc_conv_w[layer], wbr, wo, layer)
        h = _merge_call(h.reshape(b, l, d), ml[:, 5], hlf, hlb, z, zc, merge_cols, attn_cols, *mix, True,
                        *mix_w).reshape(b * l, d)

        h = _ffn_call(h, ng[2], ml[:, 6], ml[:, 7], ml[:, 8], ffn_b, layer,
                      final_g=final_norm_g if last else None)
        if not last:
            hc = _merge_call(hc.reshape(b, n_ctx, d), mc[:, 5], hcf, hcb, zc, zc, merge_cols, attn_cols, *mix,
                             False, *mix_w).reshape(b * n_ctx, d)
            hc = _ffn_call(hc, ng[2], mc[:, 6], mc[:, 7], mc[:, 8], ffn_b, layer)
    return h.reshape(b, l, d)
```
